```python
import jax, jax.numpy as jnp
from jax import lax
import numpy as np

D_MODEL = 4096
BATCH = 1
SEQ = 16384
DEPTH = 1

N_META = 16
ATT_HEADS = 32
ATT_KV_HEADS = 4
ATT_HEAD_DIM = 64
ATT_GROUP = ATT_HEADS // ATT_KV_HEADS
WINDOW = 128
ROPE_THETA = 500000.0
ROPE_DIM = ATT_HEAD_DIM // 4
RET_HEADS = 8
RET_KEY_DIM = 128
RET_VALUE_DIM = 256
RET_CHUNK = 128
RET_ROPE_THETA = 10000.0
ATT_WIDTH = ATT_HEADS * ATT_HEAD_DIM
KV_WIDTH = ATT_KV_HEADS * ATT_HEAD_DIM
RET_QK_WIDTH = RET_HEADS * RET_KEY_DIM
RET_WIDTH = RET_HEADS * RET_VALUE_DIM
MIX_WIDTH = ATT_WIDTH + RET_WIDTH
IN_WIDTH = ATT_WIDTH + 2 * KV_WIDTH + 2 * RET_QK_WIDTH + 2 * RET_WIDTH
IN_SPLITS = [ATT_WIDTH,
             ATT_WIDTH + KV_WIDTH,
             ATT_WIDTH + 2 * KV_WIDTH,
             ATT_WIDTH + 2 * KV_WIDTH + RET_QK_WIDTH,
             ATT_WIDTH + 2 * KV_WIDTH + 2 * RET_QK_WIDTH,
             ATT_WIDTH + 2 * KV_WIDTH + 2 * RET_QK_WIDTH + RET_WIDTH]
N_EXPERTS = 32
TOP_K = 4
D_EXPERT = D_MODEL // 2
SWIGLU_ALPHA = 1.702
SWIGLU_LIMIT = 7.0
EXPERT_BLOCK = 256
NORM_EPS = 1e-5
GN_EPS = 1e-6

kernel_name = "hymba_swa_retention_moe_layer"

F32 = jnp.float32


def rms_norm(x, w):
    xf = x.astype(F32)
    y = xf * lax.rsqrt(jnp.mean(xf * xf, axis=-1, keepdims=True) + NORM_EPS)
    return (y * w.astype(F32)).astype(x.dtype)


def partial_rotary(x, pos, rot_dim, theta):
    half = rot_dim // 2
    inv_freq = theta ** (-jnp.arange(half, dtype=F32) / half)
    ang = pos.astype(F32)[:, None] * inv_freq[None, :]
    cos = jnp.cos(ang)[None, :, None, :]
    sin = jnp.sin(ang)[None, :, None, :]
    xf = x.astype(F32)
    x1, x2 = xf[..., :half], xf[..., half:rot_dim]
    out = jnp.concatenate([x1 * cos - x2 * sin, x2 * cos + x1 * sin, xf[..., rot_dim:]], axis=-1)
    return out.astype(x.dtype)


def sink_softmax(scores, mask, sinks):
    s = jnp.where(mask, scores, -jnp.inf)
    m = jnp.maximum(jnp.max(s, axis=-1, keepdims=True), sinks)
    p = jnp.exp(s - m)
    return p / (jnp.sum(p, axis=-1, keepdims=True) + jnp.exp(sinks - m))


def sliding_window_attention(q, k, v, sinks):
    B, T = q.shape[0], q.shape[1]
    S = T - N_META
    nb = S // WINDOW
    scale = ATT_HEAD_DIM ** -0.5
    sink = sinks.astype(F32).reshape(ATT_KV_HEADS, ATT_GROUP)[:, :, None, None]
    qg = q.reshape(B, T, ATT_KV_HEADS, ATT_GROUP, ATT_HEAD_DIM)
    sm = jnp.einsum('bqkgd,bjkd->bkgqj', qg[:, :N_META], k[:, :N_META]).astype(F32) * scale
    causal = jnp.tril(jnp.ones((N_META, N_META), dtype=bool))
    pm = sink_softmax(sm, causal, sink)
    om = jnp.einsum('bkgqj,bjkd->bqkgd', pm.astype(v.dtype), v[:, :N_META])
    qb = qg[:, N_META:].reshape(B, nb, WINDOW, ATT_KV_HEADS, ATT_GROUP, ATT_HEAD_DIM)

    def band(t):
        tb = t[:, N_META:].reshape(B, nb, WINDOW, ATT_KV_HEADS, ATT_HEAD_DIM)
        prev = jnp.concatenate([jnp.zeros_like(tb[:, :1]), tb[:, :-1]], axis=1)
        meta = jnp.broadcast_to(t[:, None, :N_META], (B, nb, N_META, ATT_KV_HEADS, ATT_HEAD_DIM))
        return jnp.concatenate([meta, prev, tb], axis=2)

    kb, vb = band(k), band(v)
    sb = jnp.einsum('bnqkgd,bnjkd->bnkgqj', qb, kb).astype(F32) * scale
    qi = jnp.arange(WINDOW)[:, None]
    ji = jnp.arange(N_META + 2 * WINDOW)[None, :]
    rel = ji - N_META - WINDOW
    in_window = (qi - rel >= 0) & (qi - rel < WINDOW)
    has_prev = (jnp.arange(nb) > 0)[:, None, None]
    mask = (ji < N_META) | (in_window & (has_prev | (rel >= 0)))
    pb = sink_softmax(sb, mask[None, :, None, None], sink)
    ob = jnp.einsum('bnkgqj,bnjkd->bnqkgd', pb.astype(v.dtype), vb)
    return jnp.concatenate([om.reshape(B, N_META, ATT_WIDTH), ob.reshape(B, S, ATT_WIDTH)], axis=1)


def decay_matrix(n, log_g):
    i = jnp.arange(n, dtype=F32)
    diff = i[:, None] - i[None, :]
    return jnp.where(diff >= 0, jnp.exp(jnp.maximum(diff, 0.0)[None] * log_g[:, None, None]), 0.0)


def retention(q, k, v):
    B, T = q.shape[0], q.shape[1]
    S = T - N_META
    nc = S // RET_CHUNK
    q, k, v = (t.astype(F32).transpose(0, 2, 1, 3) for t in (q, k, v))
    k = k * RET_KEY_DIM ** -0.5
    log_g = jnp.log1p(-(2.0 ** (-5.0 - jnp.arange(RET_HEADS, dtype=F32))))
    qm, km, vm = q[:, :, :N_META], k[:, :, :N_META], v[:, :, :N_META]
    ym = jnp.einsum('bhqj,bhje->bhqe', jnp.einsum('bhqd,bhjd->bhqj', qm, km) * decay_matrix(N_META, log_g)[None], vm)
    jm = jnp.arange(N_META, dtype=F32)
    w_meta = jnp.exp((N_META - 1 - jm)[None, :] * log_g[:, None])
    state0 = jnp.einsum('bhjd,bhje->bhde', km * w_meta[None, :, :, None], vm)
    def chunks(t):
        return t[:, :, N_META:].reshape(B, RET_HEADS, nc, RET_CHUNK, t.shape[-1])
    qc, kc, vc = chunks(q), chunks(k), chunks(v)
    inner = jnp.einsum('bhcqd,bhcjd->bhcqj', qc, kc) * decay_matrix(RET_CHUNK, log_g)[None, :, None]
    intra = jnp.einsum('bhcqj,bhcje->bhcqe', inner, vc)
    r = jnp.arange(RET_CHUNK, dtype=F32)
    q_decay = jnp.exp((r + 1.0)[None, :] * log_g[:, None])[None, :, :, None]
    k_decay = jnp.exp((RET_CHUNK - 1.0 - r)[None, :] * log_g[:, None])[None, :, :, None]
    chunk_decay = jnp.exp(RET_CHUNK * log_g)[None, :, None, None]

    def step(state, xs):
        qx, kx, vx = xs
        cross = jnp.einsum('bhqd,bhde->bhqe', qx, state) * q_decay
        state = state * chunk_decay + jnp.einsum('bhjd,bhje->bhde', kx * k_decay, vx)
        return state, cross

    _, cross = lax.scan(step, state0, (qc.transpose(2, 0, 1, 3, 4), kc.transpose(2, 0, 1, 3, 4), vc.transpose(2, 0, 1, 3, 4)))
    yr = (intra + cross.transpose(1, 2, 0, 3, 4)).reshape(B, RET_HEADS, S, RET_VALUE_DIM)
    return jnp.concatenate([ym, yr], axis=2)


def hybrid_mixer(hn, pos, w_in, b_in, attn_sinks, ret_norm_w, w_out, b_out):
    B, T = hn.shape[0], hn.shape[1]
    z = jnp.einsum('btd,de->bte', hn, w_in) + b_in
    aq, ak, av, rq, rk, rv, rg = jnp.split(z, IN_SPLITS, axis=-1)
    aq = partial_rotary(aq.reshape(B, T, ATT_HEADS, ATT_HEAD_DIM), pos, ROPE_DIM, ROPE_THETA)
    ak = partial_rotary(ak.reshape(B, T, ATT_KV_HEADS, ATT_HEAD_DIM), pos, ROPE_DIM, ROPE_THETA)
    att = sliding_window_attention(aq, ak, av.reshape(B, T, ATT_KV_HEADS, ATT_HEAD_DIM), attn_sinks)
    rq = partial_rotary(rq.reshape(B, T, RET_HEADS, RET_KEY_DIM), pos, RET_KEY_DIM, RET_ROPE_THETA)
    rk = partial_rotary(rk.reshape(B, T, RET_HEADS, RET_KEY_DIM), pos, RET_KEY_DIM, RET_ROPE_THETA)
    y = retention(rq, rk, rv.reshape(B, T, RET_HEADS, RET_VALUE_DIM))
    mu = jnp.mean(y, axis=-1, keepdims=True)
    var = jnp.mean(jnp.square(y - mu), axis=-1, keepdims=True)
    yn = ((y - mu) * lax.rsqrt(var + GN_EPS)).transpose(0, 2, 1, 3).reshape(B, T, RET_WIDTH)
    ret = (jax.nn.silu(rg.astype(F32)) * yn * ret_norm_w.astype(F32)).astype(hn.dtype)
    mixed = jnp.concatenate([att.astype(hn.dtype), ret], axis=-1)
    return jnp.einsum('bte,ed->btd', mixed, w_out) + b_out


def moe_ffn(h, router_w, router_b, w_gu, b_gu, w_down, b_down):
    n_tok = h.shape[0]
    logits = (h @ router_w + router_b).astype(F32)
    top_vals, top_idx = lax.top_k(logits, TOP_K)
    gates = jax.nn.softmax(top_vals, axis=-1)
    n_assign = n_tok * TOP_K
    e_flat = top_idx.reshape(n_assign).astype(jnp.int32)
    tok_flat = jnp.arange(n_assign, dtype=jnp.int32) // TOP_K
    order = jnp.argsort(e_flat)
    e_sorted = e_flat[order]
    counts = jnp.bincount(e_flat, length=N_EXPERTS).astype(jnp.int32)
    padded = ((counts + EXPERT_BLOCK - 1) // EXPERT_BLOCK) * EXPERT_BLOCK
    start_sorted = jnp.cumsum(counts) - counts
    start_pad = jnp.cumsum(padded) - padded
    dest = start_pad[e_sorted] + (jnp.arange(n_assign, dtype=jnp.int32) - start_sorted[e_sorted])
    n_blocks = -(-(n_assign + N_EXPERTS * (EXPERT_BLOCK - 1)) // EXPERT_BLOCK)
    n_rows = n_blocks * EXPERT_BLOCK
    row_tok = jnp.full((n_rows,), n_tok, jnp.int32).at[dest].set(tok_flat[order])
    row_gate = jnp.zeros((n_rows,), h.dtype).at[dest].set(gates.reshape(n_assign)[order].astype(h.dtype))
    block_start = jnp.arange(n_blocks, dtype=jnp.int32) * EXPERT_BLOCK
    block_exp = jnp.clip(jnp.searchsorted(start_pad, block_start, side='right') - 1, 0, N_EXPERTS - 1)
    h_pad = jnp.concatenate([h, jnp.zeros((1, h.shape[1]), h.dtype)], axis=0)
    xb = h_pad[row_tok].reshape(n_blocks, EXPERT_BLOCK, h.shape[1])

    def expert_block(args):
        xe, e = args
        gu = xe @ w_gu[e] + b_gu[e]
        glu, lin = gu[:, :D_EXPERT], gu[:, D_EXPERT:]
        glu = jnp.minimum(glu, SWIGLU_LIMIT)
        lin = jnp.clip(lin, -SWIGLU_LIMIT, SWIGLU_LIMIT)
        act = glu * jax.nn.sigmoid(SWIGLU_ALPHA * glu) * (lin + 1.0)
        return act @ w_down[e] + b_down[e]

    yb = lax.map(expert_block, (xb, block_exp))
    y = yb.reshape(n_rows, h.shape[1]) * row_gate[:, None]
    return jax.ops.segment_sum(y, row_tok, num_segments=n_tok + 1)[:n_tok]


def setup_inputs(seed: int = 0) -> dict:
    key = jax.random.key(seed)
    ks = jax.random.split(key, 18)
    D, E, F = D_MODEL, N_EXPERTS, D_EXPERT

    def nrm(k, shape, scale):
        return jax.random.normal(k, shape, F32) * scale

    return {
        "x": nrm(ks[0], (BATCH, SEQ, D), 1.0),
        "meta_tokens": nrm(ks[1], (N_META, D), 1.0),
        "attn_norm_w": 1.0 + nrm(ks[2], (DEPTH, D), 0.02),
        "w_in": nrm(ks[3], (DEPTH, D, IN_WIDTH), D ** -0.5),
        "b_in": nrm(ks[4], (DEPTH, IN_WIDTH), 0.02),
        "attn_sinks": nrm(ks[5], (DEPTH, ATT_HEADS), 0.5),
        "ret_norm_w": 1.0 + nrm(ks[6], (DEPTH, RET_WIDTH), 0.02),
        "w_out": nrm(ks[7], (DEPTH, MIX_WIDTH, D), MIX_WIDTH ** -0.5),
        "b_out": nrm(ks[8], (DEPTH, D), 0.02),
        "ffn_norm_w": 1.0 + nrm(ks[9], (DEPTH, D), 0.02),
        "router_w": nrm(ks[10], (DEPTH, D, E), D ** -0.5),
        "router_b": nrm(ks[11], (DEPTH, E), 0.01),
        "w_gu": nrm(ks[12], (DEPTH, E, D, 2 * F), D ** -0.5),
        "b_gu": nrm(ks[13], (DEPTH, E, 2 * F), 0.02),
        "w_down": nrm(ks[14], (DEPTH, E, F, D), F ** -0.5),
        "b_down": nrm(ks[15], (DEPTH, E, D), 0.02),
        "final_norm_w": 1.0 + nrm(ks[16], (D,), 0.02),
    }


def reference(x, meta_tokens, attn_norm_w, w_in, b_in, attn_sinks, ret_norm_w, w_out, b_out,
              ffn_norm_w, router_w, router_b, w_gu, b_gu, w_down, b_down, final_norm_w):
    B = x.shape[0]
    meta = jnp.broadcast_to(meta_tokens[None].astype(x.dtype), (B, N_META, D_MODEL))
    h = jnp.concatenate([meta, x], axis=1)
    T = h.shape[1]
    pos = jnp.arange(T, dtype=jnp.int32)
    for l in range(DEPTH):
        h = h + hybrid_mixer(rms_norm(h, attn_norm_w[l]), pos, w_in[l], b_in[l], attn_sinks[l],
                             ret_norm_w[l], w_out[l], b_out[l])
        hn = rms_norm(h, ffn_norm_w[l]).reshape(B * T, D_MODEL)
        h = h + moe_ffn(hn, router_w[l], router_b[l], w_gu[l], b_gu[l], w_down[l], b_down[l]).reshape(B, T, D_MODEL)
    return rms_norm(h, final_norm_w)[:, N_META:]
```

```python
import functools
import math

import jax
import jax.numpy as jnp
import numpy as np
from jax import lax
from jax.experimental import pallas as pl
from jax.experimental.pallas import tpu as pltpu

F32 = jnp.float32
BF16 = jnp.bfloat16

D_MODEL = 4096
N_META = 16
ATT_HEADS = 32
ATT_KV_HEADS = 4
ATT_HEAD_DIM = 64
ATT_GROUP = ATT_HEADS // ATT_KV_HEADS
WINDOW = 128
ROPE_THETA = 500000.0
ROPE_DIM = ATT_HEAD_DIM // 4
RET_HEADS = 8
RET_KEY_DIM = 128
RET_VALUE_DIM = 256
RET_CHUNK = 128
RET_ROPE_THETA = 10000.0
ATT_WIDTH = ATT_HEADS * ATT_HEAD_DIM
KV_WIDTH = ATT_KV_HEADS * ATT_HEAD_DIM
RET_QK_WIDTH = RET_HEADS * RET_KEY_DIM
RET_WIDTH = RET_HEADS * RET_VALUE_DIM
IN_WIDTH = ATT_WIDTH + 2 * KV_WIDTH + 2 * RET_QK_WIDTH + 2 * RET_WIDTH
OFF_AQ = 0
OFF_AK = ATT_WIDTH
OFF_AV = OFF_AK + KV_WIDTH
OFF_RQ = OFF_AV + KV_WIDTH
OFF_RK = OFF_RQ + RET_QK_WIDTH
OFF_RV = OFF_RK + RET_QK_WIDTH
OFF_RG = OFF_RV + RET_WIDTH
N_EXPERTS = 32
TOP_K = 4
D_EXPERT = D_MODEL // 2
SWIGLU_ALPHA = 1.702
SWIGLU_LIMIT = 7.0
NORM_EPS = 1e-5
GN_EPS = 1e-6

LANES = 128
VMEM_LIMIT = 56 * 1024 * 1024

IN_TN = 512
EXPERT_TM = 256
GU_TN = 512
DOWN_TN = 1024
KEYS_PAD = 3 * WINDOW
NEG_BIG = -1e30


def _cparams(sem):
    return pltpu.CompilerParams(dimension_semantics=sem, vmem_limit_bytes=VMEM_LIMIT)


def _chunk_classes():
    cls = []
    for c in range(IN_WIDTH // LANES):
        off = c * LANES
        if off < OFF_AV:
            cls.append("a")
        elif OFF_RQ <= off < OFF_RV:
            cls.append("r")
        else:
            cls.append("n")
    return cls


def _inproj_kernel(x_ref, nw_ref, w_ref, b_ref, aa_ref, ab_ref, ac_ref, ra_ref, rb_ref, o_ref, hn_ref):
    j = pl.program_id(1)

    @pl.when(j == 0)
    def _():
        x = x_ref[...]
        ms = jnp.mean(x * x, axis=-1, keepdims=True)
        hn_ref[...] = (x * lax.rsqrt(ms + NORM_EPS) * nw_ref[...]).astype(BF16)

    y = jnp.dot(hn_ref[...], w_ref[...], preferred_element_type=F32) + b_ref[...]

    cpt = IN_TN // LANES
    classes = _chunk_classes()
    n_tiles = IN_WIDTH // IN_TN
    patterns = {}
    for t in range(n_tiles):
        patterns.setdefault(tuple(classes[t * cpt:(t + 1) * cpt]), []).append(t)

    def emit(pattern):
        for c, kind in enumerate(pattern):
            yc = y[:, c * LANES:(c + 1) * LANES]
            if kind == "a":
                yc = (yc * aa_ref[...] + pltpu.roll(yc, LANES - ROPE_DIM // 2, 1) * ab_ref[...]
                      + pltpu.roll(yc, ROPE_DIM // 2, 1) * ac_ref[...])
            elif kind == "r":
                yc = yc * ra_ref[...] + pltpu.roll(yc, RET_KEY_DIM // 2, 1) * rb_ref[...]
            o_ref[:, c * LANES:(c + 1) * LANES] = yc.astype(o_ref.dtype)

    for pattern, tiles in patterns.items():
        cond = functools.reduce(jnp.logical_or, [j == t for t in tiles])
        pl.when(cond)(functools.partial(emit, pattern))


def _inproj(x, norm_w, w_bf, b, tabs, tm):
    m = x.shape[0]
    aa, ab, ac, ra, rb = tabs
    row = lambda i, j: (i, 0)
    tab_spec = pl.BlockSpec((tm, LANES), row)
    return pl.pallas_call(
        _inproj_kernel,
        grid=(m // tm, IN_WIDTH // IN_TN),
        in_specs=[
            pl.BlockSpec((tm, D_MODEL), row),
            pl.BlockSpec((1, D_MODEL), lambda i, j: (0, 0)),
            pl.BlockSpec((D_MODEL, IN_TN), lambda i, j: (0, j)),
            pl.BlockSpec((1, IN_TN), lambda i, j: (0, j)),
            tab_spec, tab_spec, tab_spec, tab_spec, tab_spec,
        ],
        out_specs=pl.BlockSpec((tm, IN_TN), lambda i, j: (i, j)),
        out_shape=jax.ShapeDtypeStruct((m, IN_WIDTH), BF16),
        scratch_shapes=[pltpu.VMEM((tm, D_MODEL), BF16)],
        compiler_params=_cparams(("parallel", "arbitrary")),
        name="inproj",
    )(x, norm_w, w_bf, b, aa, ab, ac, ra, rb)


def _rope_tables(n_pos):
    pos = jnp.arange(n_pos, dtype=jnp.int32).astype(F32)[:, None]
    lane = np.arange(LANES)
    half = ROPE_DIM // 2
    inv = ROPE_THETA ** (-jnp.arange(half, dtype=F32) / half)
    ang = pos * inv[None, :]
    cos, sin = jnp.cos(ang), jnp.sin(ang)
    c = lane % ATT_HEAD_DIM
    f = c % half
    is_lo = jnp.asarray(c < half)[None, :]
    is_hi = jnp.asarray((c >= half) & (c < ROPE_DIM))[None, :]
    cos_l, sin_l = cos[:, f], sin[:, f]
    aa = jnp.where(is_lo | is_hi, cos_l, 1.0)
    ab = jnp.where(is_lo, -sin_l, 0.0)
    ac = jnp.where(is_hi, sin_l, 0.0)
    rhalf = RET_KEY_DIM // 2
    rinv = RET_ROPE_THETA ** (-jnp.arange(rhalf, dtype=F32) / rhalf)
    rang = pos * rinv[None, :]
    rcos, rsin = jnp.cos(rang), jnp.sin(rang)
    rf = lane % rhalf
    ra = rcos[:, rf]
    rb = jnp.where(jnp.asarray(lane < rhalf)[None, :], -rsin[:, rf], rsin[:, rf])
    return aa, ab, ac, ra, rb


def _attn_kernel(sink_ref, q_ref, ko_ref, vo_ref, kp_ref, vp_ref, km_ref, vm_ref, o_ref):
    n = pl.program_id(0)
    w = WINDOW
    qi = lax.broadcasted_iota(jnp.int32, (w, KEYS_PAD), 0)
    ji = lax.broadcasted_iota(jnp.int32, (w, KEYS_PAD), 1)
    vis_prev = (ji < w) & (ji > qi) & (n > 0)
    vis_own = (ji >= w) & (ji < 2 * w) & (ji - w <= qi)
    vis_meta = (ji >= 2 * w) & (ji < 2 * w + N_META)
    mask = vis_prev | vis_own | vis_meta
    lane = lax.broadcasted_iota(jnp.int32, (KEYS_PAD, LANES), 1)
    lo_half = lane < ATT_HEAD_DIM
    olane = lax.broadcasted_iota(jnp.int32, (w, LANES), 1) < ATT_HEAD_DIM
    scale = ATT_HEAD_DIM ** -0.5

    def spread(prev_ref, own_ref, meta_ref, g):
        c0 = (g // 2) * LANES
        chunk = jnp.concatenate([prev_ref[:, c0:c0 + LANES], own_ref[:, c0:c0 + LANES],
                                 meta_ref[:, c0:c0 + LANES]], axis=0).astype(F32)
        swapped = pltpu.roll(chunk, ATT_HEAD_DIM, 1)
        in_lo, in_hi = (chunk, swapped) if g % 2 == 0 else (swapped, chunk)
        even = jnp.where(lo_half, in_lo, 0.0)
        odd = jnp.where(lo_half, 0.0, in_hi)
        return jnp.concatenate([even, odd], axis=0).astype(BF16)

    for g in range(ATT_KV_HEADS):
        kz = spread(kp_ref, ko_ref, km_ref, g)
        vz = spread(vp_ref, vo_ref, vm_ref, g)
        for r in range(ATT_GROUP // 2):
            c = g * (ATT_GROUP // 2) + r
            qp = q_ref[:, c * LANES:(c + 1) * LANES]
            s2 = lax.dot_general(qp, kz, (((1,), (1,)), ((), ())), preferred_element_type=F32) * scale
            ps, dens = [], []
            for half in range(2):
                sink = sink_ref[2 * c + half]
                s = jnp.where(mask, s2[:, half * KEYS_PAD:(half + 1) * KEYS_PAD], NEG_BIG)
                m = jnp.maximum(jnp.max(s, axis=-1, keepdims=True), sink)
                p = jnp.exp(s - m)
                dens.append(jnp.sum(p, axis=-1, keepdims=True) + jnp.exp(sink - m))
                ps.append(p.astype(BF16))
            o = jnp.dot(jnp.concatenate(ps, axis=1), vz, preferred_element_type=F32)
            o = o / jnp.where(olane, dens[0], dens[1])
            o_ref[:, c * LANES:(c + 1) * LANES] = o.astype(o_ref.dtype)


def _attention(z, zm, sinks):
    s = z.shape[0]
    nb = s // WINDOW
    kcol, vcol = OFF_AK // KV_WIDTH, OFF_AV // KV_WIDTH
    prev = lambda n: jnp.maximum(n - 1, 0)
    return pl.pallas_call(
        _attn_kernel,
        grid=(nb,),
        in_specs=[
            pl.BlockSpec(memory_space=pltpu.SMEM),
            pl.BlockSpec((WINDOW, ATT_WIDTH), lambda n: (n, 0)),
            pl.BlockSpec((WINDOW, KV_WIDTH), lambda n: (n, kcol)),
            pl.BlockSpec((WINDOW, KV_WIDTH), lambda n: (n, vcol)),
            pl.BlockSpec((WINDOW, KV_WIDTH), lambda n: (prev(n), kcol)),
            pl.BlockSpec((WINDOW, KV_WIDTH), lambda n: (prev(n), vcol)),
            pl.BlockSpec((WINDOW, KV_WIDTH), lambda n: (0, kcol)),
            pl.BlockSpec((WINDOW, KV_WIDTH), lambda n: (0, vcol)),
        ],
        out_specs=pl.BlockSpec((WINDOW, ATT_WIDTH), lambda n: (n, 0)),
        out_shape=jax.ShapeDtypeStruct((s, ATT_WIDTH), BF16),
        compiler_params=_cparams(("parallel",)),
        name="swa_attention",
    )(sinks, z, z, z, z, z, zm, zm)


def _ret_kernel(gc_ref, q_ref, k_ref, v_ref, g_ref, km_ref, vm_ref, dm_ref, kd_ref, qd_ref, wm_ref, nw_ref,
                o_ref, u_ref):
    hh = pl.program_id(0)
    c = pl.program_id(1)

    @pl.when(c == 0)
    def _():
        kmw = (km_ref[...].astype(F32) * wm_ref[...]).T.astype(BF16)
        u_ref[...] = jnp.dot(kmw, vm_ref[...], preferred_element_type=F32)

    q, k, v = q_ref[...], k_ref[...], v_ref[...]
    inner = lax.dot_general(q, k, (((1,), (1,)), ((), ())), preferred_element_type=F32) * dm_ref[...]
    intra = jnp.dot(inner.astype(BF16), v, preferred_element_type=F32)
    cross = jnp.dot(q, u_ref[...].astype(BF16), preferred_element_type=F32) * qd_ref[...]
    y = intra + cross
    mu = jnp.mean(y, axis=-1, keepdims=True)
    yc = y - mu
    var = jnp.mean(yc * yc, axis=-1, keepdims=True)
    yn = yc * lax.rsqrt(var + GN_EPS)
    g = g_ref[...].astype(F32)
    silu = g * (1.0 / (1.0 + jnp.exp(-g)))
    o_ref[...] = (silu * yn * nw_ref[...]).astype(o_ref.dtype)
    kdt = (k.astype(F32) * kd_ref[...]).T.astype(BF16)
    u_ref[...] = u_ref[...] * gc_ref[hh] + jnp.dot(kdt, v, preferred_element_type=F32)


def _ret_tables():
    scale = RET_KEY_DIM ** -0.5
    log_g = jnp.log1p(-(2.0 ** (-5.0 - jnp.arange(RET_HEADS, dtype=F32))))
    i = jnp.arange(RET_CHUNK, dtype=F32)
    diff = i[:, None] - i[None, :]
    dm = jnp.where(diff >= 0, jnp.exp(jnp.maximum(diff, 0.0)[None] * log_g[:, None, None]), 0.0) * scale
    qd = (jnp.exp((i + 1.0)[None, :] * log_g[:, None]) * scale)[:, :, None]
    kd = jnp.exp((RET_CHUNK - 1.0 - i)[None, :] * log_g[:, None])[:, :, None]
    jm = jnp.arange(RET_CHUNK, dtype=F32)
    wm = jnp.where(jm[None, :] < N_META, jnp.exp((N_META - 1 - jm)[None, :] * log_g[:, None]), 0.0)[:, :, None]
    gc = jnp.exp(RET_CHUNK * log_g)
    return gc, dm, kd, qd, wm


def _retention(z, zm, ret_norm_w):
    s = z.shape[0]
    nc = s // RET_CHUNK
    gc, dm, kd, qd, wm = _ret_tables()
    qc, kc = OFF_RQ // RET_KEY_DIM, OFF_RK // RET_KEY_DIM
    vc, gcol = OFF_RV // RET_VALUE_DIM, OFF_RG // RET_VALUE_DIM
    vec_spec = pl.BlockSpec((None, RET_CHUNK, 1), lambda h, c: (h, 0, 0))
    return pl.pallas_call(
        _ret_kernel,
        grid=(RET_HEADS, nc),
        in_specs=[
            pl.BlockSpec(memory_space=pltpu.SMEM),
            pl.BlockSpec((RET_CHUNK, RET_KEY_DIM), lambda h, c: (c, qc + h)),
            pl.BlockSpec((RET_CHUNK, RET_KEY_DIM), lambda h, c: (c, kc + h)),
            pl.BlockSpec((RET_CHUNK, RET_VALUE_DIM), lambda h, c: (c, vc + h)),
            pl.BlockSpec((RET_CHUNK, RET_VALUE_DIM), lambda h, c: (c, gcol + h)),
            pl.BlockSpec((RET_CHUNK, RET_KEY_DIM), lambda h, c: (0, kc + h)),
            pl.BlockSpec((RET_CHUNK, RET_VALUE_DIM), lambda h, c: (0, vc + h)),
            pl.BlockSpec((None, RET_CHUNK, RET_CHUNK), lambda h, c: (h, 0, 0)),
            vec_spec, vec_spec, vec_spec,
            pl.BlockSpec((1, RET_VALUE_DIM), lambda h, c: (0, h)),
        ],
        out_specs=pl.BlockSpec((RET_CHUNK, RET_VALUE_DIM), lambda h, c: (c, h)),
        out_shape=jax.ShapeDtypeStruct((s, RET_WIDTH), BF16),
        scratch_shapes=[pltpu.VMEM((RET_KEY_DIM, RET_VALUE_DIM), F32)],
        compiler_params=_cparams(("parallel", "arbitrary")),
        name="retention",
    )(gc, z, z, z, z, zm, zm, dm, kd, qd, wm, ret_norm_w)


def _outproj_kernel(a_ref, r_ref, wt_ref, wb_ref, b_ref, x_ref, o_ref):
    acc = jnp.dot(a_ref[...], wt_ref[...], preferred_element_type=F32)
    acc += jnp.dot(r_ref[...], wb_ref[...], preferred_element_type=F32)
    o_ref[...] = x_ref[...] + acc + b_ref[...]


def _outproj(att, ret, w_bf, b, x, tm, tn):
    s = x.shape[0]
    return pl.pallas_call(
        _outproj_kernel,
        grid=(s // tm, D_MODEL // tn),
        in_specs=[
            pl.BlockSpec((tm, ATT_WIDTH), lambda i, j: (i, 0)),
            pl.BlockSpec((tm, RET_WIDTH), lambda i, j: (i, 0)),
            pl.BlockSpec((ATT_WIDTH, tn), lambda i, j: (0, j)),
            pl.BlockSpec((RET_WIDTH, tn), lambda i, j: (1, j)),
            pl.BlockSpec((1, tn), lambda i, j: (0, j)),
            pl.BlockSpec((tm, tn), lambda i, j: (i, j)),
        ],
        out_specs=pl.BlockSpec((tm, tn), lambda i, j: (i, j)),
        out_shape=jax.ShapeDtypeStruct((s, D_MODEL), F32),
        compiler_params=_cparams(("parallel", "arbitrary")),
        name="outproj",
    )(att, ret, w_bf, w_bf, b, x)


def _router_kernel(h_ref, nw_ref, rw_ref, rb_ref, idx_ref, gate_ref):
    x = h_ref[...]
    ms = jnp.mean(x * x, axis=-1, keepdims=True)
    hn = x * lax.rsqrt(ms + NORM_EPS) * nw_ref[...]
    logits = lax.dot_general(rw_ref[...], hn, (((1,), (1,)), ((), ())), preferred_element_type=F32,
                             precision=lax.Precision.HIGHEST) + rb_ref[...]
    eid = lax.broadcasted_iota(jnp.int32, logits.shape, 0)
    vals, idxs = [], []
    for _ in range(TOP_K):
        m = jnp.max(logits, axis=0, keepdims=True)
        sel = jnp.min(jnp.where(logits == m, eid, N_EXPERTS), axis=0, keepdims=True)
        vals.append(m)
        idxs.append(sel)
        logits = jnp.where(eid == sel, -jnp.inf, logits)
    es = [jnp.exp(v - vals[0]) for v in vals]
    tot = functools.reduce(lambda a, b: a + b, es)
    idx_ref[...] = jnp.concatenate(idxs, axis=0)
    gate_ref[...] = jnp.concatenate([e / tot for e in es], axis=0)


def _router(h1, norm_w, rw_t, rb, tm):
    s = h1.shape[0]
    return pl.pallas_call(
        _router_kernel,
        grid=(s // tm,),
        in_specs=[
            pl.BlockSpec((tm, D_MODEL), lambda i: (i, 0)),
            pl.BlockSpec((1, D_MODEL), lambda i: (0, 0)),
            pl.BlockSpec((N_EXPERTS, D_MODEL), lambda i: (0, 0)),
            pl.BlockSpec((N_EXPERTS, 1), lambda i: (0, 0)),
        ],
        out_specs=[pl.BlockSpec((TOP_K, tm), lambda i: (0, i)), pl.BlockSpec((TOP_K, tm), lambda i: (0, i))],
        out_shape=[jax.ShapeDtypeStruct((TOP_K, s), jnp.int32), jax.ShapeDtypeStruct((TOP_K, s), F32)],
        compiler_params=_cparams(("parallel",)),
        name="router",
    )(h1, norm_w, rw_t, rb)


def _gather_norm_kernel(nused_ref, tok_ref, h_hbm, nw_ref, o_ref, buf_ref, sem):
    b = pl.program_id(0)
    tm = buf_ref.shape[0]

    @pl.when(b < nused_ref[0])
    def _():
        def issue(r, carry):
            tok = tok_ref[0, r]
            pltpu.make_async_copy(h_hbm.at[pl.ds(tok, 1)], buf_ref.at[pl.ds(r, 1)], sem).start()
            return carry
        lax.fori_loop(0, tm, issue, 0)
        pltpu.make_async_copy(h_hbm.at[pl.ds(0, tm)], buf_ref, sem).wait()
        x = buf_ref[...]
        ms = jnp.mean(x * x, axis=-1, keepdims=True)
        o_ref[...] = (x * lax.rsqrt(ms + NORM_EPS) * nw_ref[...]).astype(o_ref.dtype)

    @pl.when(b >= nused_ref[0])
    def _():
        o_ref[...] = jnp.zeros_like(o_ref)


def _gather_norm(n_used, row_tok3, h1, norm_w, tm):
    nblk = row_tok3.shape[0]
    last = lambda b, nu: jnp.minimum(b, nu[0] - 1)
    return pl.pallas_call(
        _gather_norm_kernel,
        grid_spec=pltpu.PrefetchScalarGridSpec(
            num_scalar_prefetch=1,
            grid=(nblk,),
            in_specs=[
                pl.BlockSpec((None, 1, tm), lambda b, nu: (last(b, nu), 0, 0), memory_space=pltpu.SMEM),
                pl.BlockSpec(memory_space=pl.ANY),
                pl.BlockSpec((1, D_MODEL), lambda b, nu: (0, 0)),
            ],
            out_specs=pl.BlockSpec((tm, D_MODEL), lambda b, nu: (b, 0)),
            scratch_shapes=[pltpu.VMEM((tm, D_MODEL), F32), pltpu.SemaphoreType.DMA(())],
        ),
        out_shape=jax.ShapeDtypeStruct((nblk * tm, D_MODEL), BF16),
        compiler_params=_cparams(("arbitrary",)),
        name="gather_norm",
    )(n_used, row_tok3, h1, norm_w)


def _gateup_kernel(nused_ref, be_ref, x_ref, wg_ref, wl_ref, bg_ref, bl_ref, o_ref, wgb_ref, wlb_ref):
    b = pl.program_id(1)
    nu = nused_ref[0]
    new_expert = (b == 0) | (be_ref[b] != be_ref[jnp.maximum(b - 1, 0)])

    @pl.when((b < nu) & new_expert)
    def _():
        wgb_ref[...] = wg_ref[...].astype(BF16)
        wlb_ref[...] = wl_ref[...].astype(BF16)

    @pl.when(b < nu)
    def _():
        x = x_ref[...]
        glu = jnp.dot(x, wgb_ref[...], preferred_element_type=F32) + bg_ref[...]
        lin = jnp.dot(x, wlb_ref[...], preferred_element_type=F32) + bl_ref[...]
        glu = jnp.minimum(glu, SWIGLU_LIMIT)
        lin = jnp.clip(lin, -SWIGLU_LIMIT, SWIGLU_LIMIT)
        act = glu * (1.0 / (1.0 + jnp.exp(-SWIGLU_ALPHA * glu))) * (lin + 1.0)
        o_ref[...] = act.astype(o_ref.dtype)

    @pl.when(b >= nu)
    def _():
        o_ref[...] = jnp.zeros_like(o_ref)


def _gateup(n_used, block_exp, xb, w_gu, b_gu3, tm):
    nblk = xb.shape[0] // tm
    nj = D_EXPERT // GU_TN
    last = lambda b, nu: jnp.minimum(b, nu[0] - 1)
    return pl.pallas_call(
        _gateup_kernel,
        grid_spec=pltpu.PrefetchScalarGridSpec(
            num_scalar_prefetch=2,
            grid=(nj, nblk),
            in_specs=[
                pl.BlockSpec((tm, D_MODEL), lambda j, b, nu, be: (last(b, nu), 0)),
                pl.BlockSpec((None, D_MODEL, GU_TN), lambda j, b, nu, be: (be[b], 0, j)),
                pl.BlockSpec((None, D_MODEL, GU_TN), lambda j, b, nu, be: (be[b], 0, j + nj)),
                pl.BlockSpec((None, 1, GU_TN), lambda j, b, nu, be: (be[b], 0, j)),
                pl.BlockSpec((None, 1, GU_TN), lambda j, b, nu, be: (be[b], 0, j + nj)),
            ],
            out_specs=pl.BlockSpec((tm, GU_TN), lambda j, b, nu, be: (b, j)),
            scratch_shapes=[pltpu.VMEM((D_MODEL, GU_TN), BF16), pltpu.VMEM((D_MODEL, GU_TN), BF16)],
        ),
        out_shape=jax.ShapeDtypeStruct((nblk * tm, D_EXPERT), BF16),
        compiler_params=_cparams(("arbitrary", "arbitrary")),
        name="expert_gateup",
    )(n_used, block_exp, xb, w_gu, w_gu, b_gu3, b_gu3)


def _down_kernel(nused_ref, be_ref, a_ref, w_ref, bias_ref, o_ref, wb_ref):
    b = pl.program_id(1)
    nu = nused_ref[0]
    new_expert = (b == 0) | (be_ref[b] != be_ref[jnp.maximum(b - 1, 0)])

    @pl.when((b < nu) & new_expert)
    def _():
        wb_ref[...] = w_ref[...].astype(BF16)

    @pl.when(b < nu)
    def _():
        o_ref[...] = jnp.dot(a_ref[...], wb_ref[...], preferred_element_type=F32) + bias_ref[...]

    @pl.when(b >= nu)
    def _():
        o_ref[...] = jnp.zeros_like(o_ref)


def _down(n_used, block_exp, act, w_down, b_down3, tm):
    nblk = act.shape[0] // tm
    nj = D_MODEL // DOWN_TN
    last = lambda b, nu: jnp.minimum(b, nu[0] - 1)
    return pl.pallas_call(
        _down_kernel,
        grid_spec=pltpu.PrefetchScalarGridSpec(
            num_scalar_prefetch=2,
            grid=(nj, nblk),
            in_specs=[
                pl.BlockSpec((tm, D_EXPERT), lambda j, b, nu, be: (last(b, nu), 0)),
                pl.BlockSpec((None, D_EXPERT, DOWN_TN), lambda j, b, nu, be: (be[b], 0, j)),
                pl.BlockSpec((None, 1, DOWN_TN), lambda j, b, nu, be: (be[b], 0, j)),
            ],
            out_specs=pl.BlockSpec((tm, DOWN_TN), lambda j, b, nu, be: (b, j)),
            scratch_shapes=[pltpu.VMEM((D_EXPERT, DOWN_TN), BF16)],
        ),
        out_shape=jax.ShapeDtypeStruct((nblk * tm, D_MODEL), F32),
        compiler_params=_cparams(("arbitrary", "arbitrary")),
        name="expert_down",
    )(n_used, block_exp, act, w_down, b_down3)


def _combine_kernel(pos_ref, g_ref, h_ref, nw_ref, y_hbm, o_ref, buf_ref, sem):
    tq = h_ref.shape[0]
    n = TOP_K * tq

    def issue(r, carry):
        row = pos_ref[0, r]
        pltpu.make_async_copy(y_hbm.at[pl.ds(row, 1)], buf_ref.at[pl.ds(r, 1)], sem).start()
        return carry
    lax.fori_loop(0, n, issue, 0)
    pltpu.make_async_copy(y_hbm.at[pl.ds(0, n)], buf_ref, sem).wait()
    acc = h_ref[...]
    g = g_ref[...]
    for k in range(TOP_K):
        acc = acc + g[:, k:k + 1] * buf_ref[k * tq:(k + 1) * tq, :]
    ms = jnp.mean(acc * acc, axis=-1, keepdims=True)
    o_ref[...] = acc * lax.rsqrt(ms + NORM_EPS) * nw_ref[...]


def _combine(pos3, gates, h1, norm_w, yb, tq):
    s = h1.shape[0]
    return pl.pallas_call(
        _combine_kernel,
        grid=(s // tq,),
        in_specs=[
            pl.BlockSpec((None, 1, TOP_K * tq), lambda i: (i, 0, 0), memory_space=pltpu.SMEM),
            pl.BlockSpec((tq, TOP_K), lambda i: (i, 0)),
            pl.BlockSpec((tq, D_MODEL), lambda i: (i, 0)),
            pl.BlockSpec((1, D_MODEL), lambda i: (0, 0)),
            pl.BlockSpec(memory_space=pl.ANY),
        ],
        out_specs=pl.BlockSpec((tq, D_MODEL), lambda i: (i, 0)),
        out_shape=jax.ShapeDtypeStruct((s, D_MODEL), F32),
        scratch_shapes=[pltpu.VMEM((TOP_K * tq, D_MODEL), F32), pltpu.SemaphoreType.DMA(())],
        compiler_params=_cparams(("arbitrary",)),
        name="combine_norm",
    )(pos3, gates, h1, norm_w, yb)


def _routing_metadata(top_idx, tm):
    s = top_idx.shape[0]
    n_assign = s * TOP_K
    nblk = -(-(n_assign + N_EXPERTS * (tm - 1)) // tm)
    e_flat = top_idx.reshape(n_assign)
    order = jnp.argsort(e_flat)
    e_sorted = e_flat[order]
    counts = jnp.bincount(e_flat, length=N_EXPERTS).astype(jnp.int32)
    padded = ((counts + tm - 1) // tm) * tm
    start_sorted = jnp.cumsum(counts) - counts
    start_pad = jnp.cumsum(padded) - padded
    dest = start_pad[e_sorted] + (jnp.arange(n_assign, dtype=jnp.int32) - start_sorted[e_sorted])
    row_tok = jnp.zeros((nblk * tm,), jnp.int32).at[dest].set((order // TOP_K).astype(jnp.int32))
    pos = jnp.zeros((n_assign,), jnp.int32).at[order].set(dest.astype(jnp.int32))
    n_used = (jnp.sum(padded) // tm).astype(jnp.int32)
    block_start = jnp.arange(nblk, dtype=jnp.int32) * tm
    block_exp = jnp.clip(jnp.searchsorted(start_pad, block_start, side="right") - 1, 0, N_EXPERTS - 1)
    block_exp = block_exp[jnp.minimum(jnp.arange(nblk), n_used - 1)].astype(jnp.int32)
    return n_used.reshape(1), block_exp, row_tok.reshape(nblk, 1, tm), pos.reshape(s, TOP_K)


def _pick(n, pref):
    t = pref
    while n % t:
        t //= 2
    return t


def kernel(x, meta_tokens, attn_norm_w, w_in, b_in, attn_sinks, ret_norm_w, w_out, b_out, ffn_norm_w, router_w,
           router_b, w_gu, b_gu, w_down, b_down, final_norm_w):
    assert x.shape[0] == 1 and attn_norm_w.shape[0] == 1
    xs = x[0]
    s = xs.shape[0]
    assert s % WINDOW == 0
    tm = _pick(s, 512)

    tabs = _rope_tables(N_META + s)
    w_in_bf = w_in[0].astype(BF16)
    b_in2 = b_in[0][None, :]
    nw = attn_norm_w[0][None, :]
    z = _inproj(xs, nw, w_in_bf, b_in2, [t[N_META:] for t in tabs], tm)
    zm = _inproj(meta_tokens, nw, w_in_bf, b_in2, [t[:N_META] for t in tabs], N_META)
    zm = jnp.pad(zm, ((0, WINDOW - N_META), (0, 0)))

    att = _attention(z, zm, attn_sinks[0])
    ret = _retention(z, zm, ret_norm_w[0][None, :])
    h1 = _outproj(att, ret, w_out[0].astype(BF16), b_out[0][None, :], xs, tm, 1024)

    ffn_w = ffn_norm_w[0][None, :]
    idx_t, gate_t = _router(h1, ffn_w, router_w[0].T, router_b[0][:, None], tm)
    gates = gate_t.T
    n_used, block_exp, row_tok3, pos = _routing_metadata(idx_t.T, EXPERT_TM)

    xb = _gather_norm(n_used, row_tok3, h1, ffn_w, EXPERT_TM)
    act = _gateup(n_used, block_exp, xb, w_gu[0], b_gu[0][:, None, :], EXPERT_TM)
    yb = _down(n_used, block_exp, act, w_down[0], b_down[0][:, None, :], EXPERT_TM)

    tq = WINDOW
    pos3 = pos.reshape(s // tq, tq, TOP_K).transpose(0, 2, 1).reshape(s // tq, 1, TOP_K * tq)
    out = _combine(pos3, gates, h1, final_norm_w[None, :], yb, tq)
    return out[None]
```

```python
import functools

import jax
import jax.numpy as jnp
import numpy as np
from jax import lax
from jax.experimental import pallas as pl
from jax.experimental.pallas import tpu as pltpu

F32 = jnp.float32
BF16 = jnp.bfloat16

D_MODEL = 4096
N_META = 16
ATT_HEADS = 32
ATT_KV_HEADS = 4
ATT_HEAD_DIM = 64
ATT_GROUP = ATT_HEADS // ATT_KV_HEADS
WINDOW = 128
ROPE_THETA = 500000.0
ROPE_DIM = ATT_HEAD_DIM // 4
RET_HEADS = 8
RET_KEY_DIM = 128
RET_VALUE_DIM = 256
RET_CHUNK = 128
RET_ROPE_THETA = 10000.0
ATT_WIDTH = ATT_HEADS * ATT_HEAD_DIM
KV_WIDTH = ATT_KV_HEADS * ATT_HEAD_DIM
RET_QK_WIDTH = RET_HEADS * RET_KEY_DIM
RET_WIDTH = RET_HEADS * RET_VALUE_DIM
IN_WIDTH = ATT_WIDTH + 2 * KV_WIDTH + 2 * RET_QK_WIDTH + 2 * RET_WIDTH
OFF_AQ = 0
OFF_RQ = OFF_AQ + ATT_WIDTH
OFF_RK = OFF_RQ + RET_QK_WIDTH
OFF_RV = OFF_RK + RET_QK_WIDTH
OFF_RG = OFF_RV + RET_WIDTH
OFF_AK = OFF_RG + RET_WIDTH
OFF_AV = OFF_AK + KV_WIDTH
N_EXPERTS = 32
TOP_K = 4
D_EXPERT = D_MODEL // 2
SWIGLU_ALPHA = 1.702
SWIGLU_LIMIT = 7.0
NORM_EPS = 1e-5
GN_EPS = 1e-6

LANES = 128
VMEM_LIMIT = 56 * 1024 * 1024

IN_TN = 512
EXPERT_TM = 512
EXPERT_HALF = EXPERT_TM // 2
GU_TN = 512
DOWN_TN = 1024
KEYS_PAD = 3 * WINDOW
NEG_BIG = -1e30
DMA_ROWS = 16


def _cparams(sem):
    return pltpu.CompilerParams(dimension_semantics=sem, vmem_limit_bytes=VMEM_LIMIT)


def _permute_in_columns(t):
    a0 = ATT_WIDTH
    a1 = a0 + 2 * KV_WIDTH
    return jnp.concatenate([t[..., :a0], t[..., a1:], t[..., a0:a1]], axis=-1)


def _chunk_classes():
    cls = []
    for c in range(IN_WIDTH // LANES):
        off = c * LANES
        if off < OFF_RQ or OFF_AK <= off < OFF_AV:
            cls.append("a")
        elif OFF_RQ <= off < OFF_RV:
            cls.append("r")
        else:
            cls.append("n")
    return cls


def _inproj_kernel(x_ref, nw_ref, w_ref, b_ref, aa_ref, ab_ref, ac_ref, ra_ref, rb_ref, o_ref, hn_ref):
    j = pl.program_id(1)

    @pl.when(j == 0)
    def _():
        x = x_ref[...]
        ms = jnp.mean(x * x, axis=-1, keepdims=True)
        hn_ref[...] = (x * lax.rsqrt(ms + NORM_EPS) * nw_ref[...]).astype(BF16)

    y = jnp.dot(hn_ref[...], w_ref[...], preferred_element_type=F32) + b_ref[...]

    cpt = IN_TN // LANES
    classes = _chunk_classes()
    n_tiles = IN_WIDTH // IN_TN
    patterns = {}
    for t in range(n_tiles):
        patterns.setdefault(tuple(classes[t * cpt:(t + 1) * cpt]), []).append(t)

    def emit(pattern):
        for c, kind in enumerate(pattern):
            yc = y[:, c * LANES:(c + 1) * LANES]
            if kind == "a":
                yc = (yc * aa_ref[...] + pltpu.roll(yc, LANES - ROPE_DIM // 2, 1) * ab_ref[...]
                      + pltpu.roll(yc, ROPE_DIM // 2, 1) * ac_ref[...])
            elif kind == "r":
                yc = yc * ra_ref[...] + pltpu.roll(yc, RET_KEY_DIM // 2, 1) * rb_ref[...]
            o_ref[:, c * LANES:(c + 1) * LANES] = yc.astype(o_ref.dtype)

    for pattern, tiles in patterns.items():
        cond = functools.reduce(jnp.logical_or, [j == t for t in tiles])
        pl.when(cond)(functools.partial(emit, pattern))


def _inproj(x, norm_w, w_bf, b, tabs, tm):
    m = x.shape[0]
    aa, ab, ac, ra, rb = tabs
    row = lambda i, j: (i, 0)
    tab_spec = pl.BlockSpec((tm, LANES), row)
    return pl.pallas_call(
        _inproj_kernel,
        grid=(m // tm, IN_WIDTH // IN_TN),
        in_specs=[
            pl.BlockSpec((tm, D_MODEL), row),
            pl.BlockSpec((1, D_MODEL), lambda i, j: (0, 0)),
            pl.BlockSpec((D_MODEL, IN_TN), lambda i, j: (0, j)),
            pl.BlockSpec((1, IN_TN), lambda i, j: (0, j)),
            tab_spec, tab_spec, tab_spec, tab_spec, tab_spec,
        ],
        out_specs=pl.BlockSpec((tm, IN_TN), lambda i, j: (i, j)),
        out_shape=jax.ShapeDtypeStruct((m, IN_WIDTH), BF16),
        scratch_shapes=[pltpu.VMEM((tm, D_MODEL), BF16)],
        compiler_params=_cparams(("parallel", "arbitrary")),
        name="inproj",
    )(x, norm_w, w_bf, b, aa, ab, ac, ra, rb)


def _rope_tables(n_pos):
    pos = jnp.arange(n_pos, dtype=jnp.int32).astype(F32)[:, None]
    lane = np.arange(LANES)
    half = ROPE_DIM // 2
    inv = ROPE_THETA ** (-jnp.arange(half, dtype=F32) / half)
    ang = pos * inv[None, :]
    cos, sin = jnp.cos(ang), jnp.sin(ang)
    c = lane % ATT_HEAD_DIM
    f = c % half
    is_lo = jnp.asarray(c < half)[None, :]
    is_hi = jnp.asarray((c >= half) & (c < ROPE_DIM))[None, :]
    cos_l, sin_l = cos[:, f], sin[:, f]
    aa = jnp.where(is_lo | is_hi, cos_l, 1.0)
    ab = jnp.where(is_lo, -sin_l, 0.0)
    ac = jnp.where(is_hi, sin_l, 0.0)
    rhalf = RET_KEY_DIM // 2
    rinv = RET_ROPE_THETA ** (-jnp.arange(rhalf, dtype=F32) / rhalf)
    rang = pos * rinv[None, :]
    rcos, rsin = jnp.cos(rang), jnp.sin(rang)
    rf = lane % rhalf
    ra = rcos[:, rf]
    rb = jnp.where(jnp.asarray(lane < rhalf)[None, :], -rsin[:, rf], rsin[:, rf])
    return aa, ab, ac, ra, rb


def _attn_kernel(sink_ref, q_ref, ko_ref, vo_ref, kp_ref, vp_ref, km_ref, vm_ref, o_ref):
    n = pl.program_id(0)
    w = WINDOW
    qi = lax.broadcasted_iota(jnp.int32, (w, KEYS_PAD), 0)
    ji = lax.broadcasted_iota(jnp.int32, (w, KEYS_PAD), 1)
    vis_prev = (ji < w) & (ji > qi) & (n > 0)
    vis_own = (ji >= w) & (ji < 2 * w) & (ji - w <= qi)
    vis_meta = (ji >= 2 * w) & (ji < 2 * w + N_META)
    mask = vis_prev | vis_own | vis_meta
    lane = lax.broadcasted_iota(jnp.int32, (KEYS_PAD, LANES), 1)
    lo_half = lane < ATT_HEAD_DIM
    olane = lax.broadcasted_iota(jnp.int32, (w, LANES), 1) < ATT_HEAD_DIM
    scale = ATT_HEAD_DIM ** -0.5

    def spread(prev_ref, own_ref, meta_ref, g):
        c0 = (g // 2) * LANES
        chunk = jnp.concatenate([prev_ref[:, c0:c0 + LANES], own_ref[:, c0:c0 + LANES],
                                 meta_ref[:, c0:c0 + LANES]], axis=0).astype(F32)
        swapped = pltpu.roll(chunk, ATT_HEAD_DIM, 1)
        in_lo, in_hi = (chunk, swapped) if g % 2 == 0 else (swapped, chunk)
        even = jnp.where(lo_half, in_lo, 0.0)
        odd = jnp.where(lo_half, 0.0, in_hi)
        return jnp.concatenate([even, odd], axis=0).astype(BF16)

    for g in range(ATT_KV_HEADS):
        kz = spread(kp_ref, ko_ref, km_ref, g)
        vz = spread(vp_ref, vo_ref, vm_ref, g)
        for r in range(ATT_GROUP // 2):
            c = g * (ATT_GROUP // 2) + r
            qp = q_ref[:, c * LANES:(c + 1) * LANES]
            s2 = lax.dot_general(qp, kz, (((1,), (1,)), ((), ())), preferred_element_type=F32) * scale
            ps, dens = [], []
            for half in range(2):
                sink = sink_ref[2 * c + half]
                s = jnp.where(mask, s2[:, half * KEYS_PAD:(half + 1) * KEYS_PAD], NEG_BIG)
                m = jnp.maximum(jnp.max(s, axis=-1, keepdims=True), sink)
                p = jnp.exp(s - m)
                dens.append(jnp.sum(p, axis=-1, keepdims=True) + jnp.exp(sink - m))
                ps.append(p.astype(BF16))
            o = jnp.dot(jnp.concatenate(ps, axis=1), vz, preferred_element_type=F32)
            o = o / jnp.where(olane, dens[0], dens[1])
            o_ref[:, c * LANES:(c + 1) * LANES] = o.astype(o_ref.dtype)


def _attention(z, zm, sinks):
    s = z.shape[0]
    nb = s // WINDOW
    kcol, vcol = OFF_AK // KV_WIDTH, OFF_AV // KV_WIDTH
    prev = lambda n: jnp.maximum(n - 1, 0)
    return pl.pallas_call(
        _attn_kernel,
        grid=(nb,),
        in_specs=[
            pl.BlockSpec(memory_space=pltpu.SMEM),
            pl.BlockSpec((WINDOW, ATT_WIDTH), lambda n: (n, 0)),
            pl.BlockSpec((WINDOW, KV_WIDTH), lambda n: (n, kcol)),
            pl.BlockSpec((WINDOW, KV_WIDTH), lambda n: (n, vcol)),
            pl.BlockSpec((WINDOW, KV_WIDTH), lambda n: (prev(n), kcol)),
            pl.BlockSpec((WINDOW, KV_WIDTH), lambda n: (prev(n), vcol)),
            pl.BlockSpec((WINDOW, KV_WIDTH), lambda n: (0, kcol)),
            pl.BlockSpec((WINDOW, KV_WIDTH), lambda n: (0, vcol)),
        ],
        out_specs=pl.BlockSpec((WINDOW, ATT_WIDTH), lambda n: (n, 0)),
        out_shape=jax.ShapeDtypeStruct((s, ATT_WIDTH), BF16),
        compiler_params=_cparams(("parallel",)),
        name="swa_attention",
    )(sinks, z, z, z, z, z, zm, zm)


def _ret_kernel(gc_ref, q_ref, k_ref, v_ref, g_ref, km_ref, vm_ref, dm_ref, kd_ref, qd_ref, wm_ref, nw_ref,
                o_ref, u_ref):
    c = pl.program_id(0)
    dk, dv = RET_KEY_DIM, RET_VALUE_DIM

    @pl.when(c == 0)
    def _():
        for hh in range(RET_HEADS):
            kmw = (km_ref[:, hh * dk:(hh + 1) * dk].astype(F32) * wm_ref[hh]).T.astype(BF16)
            u_ref[hh] = jnp.dot(kmw, vm_ref[:, hh * dv:(hh + 1) * dv], preferred_element_type=F32)

    for hh in range(RET_HEADS):
        q = q_ref[:, hh * dk:(hh + 1) * dk]
        k = k_ref[:, hh * dk:(hh + 1) * dk]
        v = v_ref[:, hh * dv:(hh + 1) * dv]
        u = u_ref[hh]
        inner = lax.dot_general(q, k, (((1,), (1,)), ((), ())), preferred_element_type=F32) * dm_ref[hh]
        intra = jnp.dot(inner.astype(BF16), v, preferred_element_type=F32)
        cross = jnp.dot(q, u.astype(BF16), preferred_element_type=F32) * qd_ref[hh]
        y = intra + cross
        mu = jnp.mean(y, axis=-1, keepdims=True)
        yc = y - mu
        var = jnp.mean(yc * yc, axis=-1, keepdims=True)
        yn = yc * lax.rsqrt(var + GN_EPS)
        g = g_ref[:, hh * dv:(hh + 1) * dv].astype(F32)
        silu = g * (1.0 / (1.0 + jnp.exp(-g)))
        o_ref[:, hh * dv:(hh + 1) * dv] = (silu * yn * nw_ref[:, hh * dv:(hh + 1) * dv]).astype(o_ref.dtype)
        kdt = (k.astype(F32) * kd_ref[hh]).T.astype(BF16)
        u_ref[hh] = u * gc_ref[hh] + jnp.dot(kdt, v, preferred_element_type=F32)


def _ret_tables():
    scale = RET_KEY_DIM ** -0.5
    log_g = jnp.log1p(-(2.0 ** (-5.0 - jnp.arange(RET_HEADS, dtype=F32))))
    i = jnp.arange(RET_CHUNK, dtype=F32)
    diff = i[:, None] - i[None, :]
    dm = jnp.where(diff >= 0, jnp.exp(jnp.maximum(diff, 0.0)[None] * log_g[:, None, None]), 0.0) * scale
    qd = (jnp.exp((i + 1.0)[None, :] * log_g[:, None]) * scale)[:, :, None]
    kd = jnp.exp((RET_CHUNK - 1.0 - i)[None, :] * log_g[:, None])[:, :, None]
    jm = jnp.arange(RET_CHUNK, dtype=F32)
    wm = jnp.where(jm[None, :] < N_META, jnp.exp((N_META - 1 - jm)[None, :] * log_g[:, None]), 0.0)[:, :, None]
    gc = jnp.exp(RET_CHUNK * log_g)
    return gc, dm, kd, qd, wm


def _retention(z, zm, ret_norm_w):
    s = z.shape[0]
    nc = s // RET_CHUNK
    gc, dm, kd, qd, wm = _ret_tables()
    qc, kc = OFF_RQ // RET_QK_WIDTH, OFF_RK // RET_QK_WIDTH
    vc, gcol = OFF_RV // RET_WIDTH, OFF_RG // RET_WIDTH
    full3 = lambda c: (0, 0, 0)
    vec_spec = pl.BlockSpec((RET_HEADS, RET_CHUNK, 1), full3)
    return pl.pallas_call(
        _ret_kernel,
        grid=(nc,),
        in_specs=[
            pl.BlockSpec(memory_space=pltpu.SMEM),
            pl.BlockSpec((RET_CHUNK, RET_QK_WIDTH), lambda c: (c, qc)),
            pl.BlockSpec((RET_CHUNK, RET_QK_WIDTH), lambda c: (c, kc)),
            pl.BlockSpec((RET_CHUNK, RET_WIDTH), lambda c: (c, vc)),
            pl.BlockSpec((RET_CHUNK, RET_WIDTH), lambda c: (c, gcol)),
            pl.BlockSpec((RET_CHUNK, RET_QK_WIDTH), lambda c: (0, kc)),
            pl.BlockSpec((RET_CHUNK, RET_WIDTH), lambda c: (0, vc)),
            pl.BlockSpec((RET_HEADS, RET_CHUNK, RET_CHUNK), full3),
            vec_spec, vec_spec, vec_spec,
            pl.BlockSpec((1, RET_WIDTH), lambda c: (0, 0)),
        ],
        out_specs=pl.BlockSpec((RET_CHUNK, RET_WIDTH), lambda c: (c, 0)),
        out_shape=jax.ShapeDtypeStruct((s, RET_WIDTH), BF16),
        scratch_shapes=[pltpu.VMEM((RET_HEADS, RET_KEY_DIM, RET_VALUE_DIM), F32)],
        compiler_params=_cparams(("arbitrary",)),
        name="retention",
    )(gc, z, z, z, z, zm, zm, dm, kd, qd, wm, ret_norm_w)


def _outproj_kernel(a_ref, r_ref, wt_ref, wb_ref, b_ref, x_ref, o_ref):
    acc = jnp.dot(a_ref[...], wt_ref[...], preferred_element_type=F32)
    acc += jnp.dot(r_ref[...], wb_ref[...], preferred_element_type=F32)
    o_ref[...] = x_ref[...] + acc + b_ref[...]


def _outproj(att, ret, w_bf, b, x, tm, tn):
    s = x.shape[0]
    return pl.pallas_call(
        _outproj_kernel,
        grid=(s // tm, D_MODEL // tn),
        in_specs=[
            pl.BlockSpec((tm, ATT_WIDTH), lambda i, j: (i, 0)),
            pl.BlockSpec((tm, RET_WIDTH), lambda i, j: (i, 0)),
            pl.BlockSpec((ATT_WIDTH, tn), lambda i, j: (0, j)),
            pl.BlockSpec((RET_WIDTH, tn), lambda i, j: (1, j)),
            pl.BlockSpec((1, tn), lambda i, j: (0, j)),
            pl.BlockSpec((tm, tn), lambda i, j: (i, j)),
        ],
        out_specs=pl.BlockSpec((tm, tn), lambda i, j: (i, j)),
        out_shape=jax.ShapeDtypeStruct((s, D_MODEL), F32),
        compiler_params=_cparams(("parallel", "arbitrary")),
        name="outproj",
    )(att, ret, w_bf, w_bf, b, x)


def _router_kernel(h_ref, nw_ref, rw_ref, rb_ref, idx_ref, gate_ref):
    x = h_ref[...]
    ms = jnp.mean(x * x, axis=-1, keepdims=True)
    hn = x * lax.rsqrt(ms + NORM_EPS) * nw_ref[...]
    logits = lax.dot_general(rw_ref[...], hn, (((1,), (1,)), ((), ())), preferred_element_type=F32,
                             precision=lax.Precision.HIGHEST) + rb_ref[...]
    eid = lax.broadcasted_iota(jnp.int32, logits.shape, 0)
    vals, idxs = [], []
    for _ in range(TOP_K):
        m = jnp.max(logits, axis=0, keepdims=True)
        sel = jnp.min(jnp.where(logits == m, eid, N_EXPERTS), axis=0, keepdims=True)
        vals.append(m)
        idxs.append(sel)
        logits = jnp.where(eid == sel, -jnp.inf, logits)
    es = [jnp.exp(v - vals[0]) for v in vals]
    tot = functools.reduce(lambda a, b: a + b, es)
    idx_ref[...] = jnp.concatenate(idxs, axis=0)
    gate_ref[...] = jnp.concatenate([e / tot for e in es], axis=0)


def _router(h1, norm_w, rw_t, rb, tm):
    s = h1.shape[0]
    return pl.pallas_call(
        _router_kernel,
        grid=(s // tm,),
        in_specs=[
            pl.BlockSpec((tm, D_MODEL), lambda i: (i, 0)),
            pl.BlockSpec((1, D_MODEL), lambda i: (0, 0)),
            pl.BlockSpec((N_EXPERTS, D_MODEL), lambda i: (0, 0)),
            pl.BlockSpec((N_EXPERTS, 1), lambda i: (0, 0)),
        ],
        out_specs=[pl.BlockSpec((TOP_K, tm), lambda i: (0, i)), pl.BlockSpec((TOP_K, tm), lambda i: (0, i))],
        out_shape=[jax.ShapeDtypeStruct((TOP_K, s), jnp.int32), jax.ShapeDtypeStruct((TOP_K, s), F32)],
        compiler_params=_cparams(("parallel",)),
        name="router",
    )(h1, norm_w, rw_t, rb)


def _row_copy(src_hbm, idx_ref, buf_ref, sems, slot, r):
    return pltpu.make_async_copy(src_hbm.at[pl.ds(idx_ref[0, r], 1)], buf_ref.at[slot, pl.ds(r, 1)], sems.at[slot])


def _slot_wait(src_hbm, buf_ref, sems, slot):
    n = buf_ref.shape[1]
    pltpu.make_async_copy(src_hbm.at[pl.ds(0, n)], buf_ref.at[slot], sems.at[slot]).wait()


def _gather_norm_kernel(nused_ref, tok_ref, tokn_ref, h_hbm, nw_ref, o_ref, buf_ref, sems):
    b = pl.program_id(0)
    nu = nused_ref[0]
    tm = o_ref.shape[0]
    slot = b % 2
    n_steps = tm // DMA_ROWS

    def issue(idx_ref, s, r0):
        for r in range(DMA_ROWS):
            _row_copy(h_hbm, idx_ref, buf_ref, sems, s, r0 + r).start()

    @pl.when(b == 0)
    def _():
        def first(i, carry):
            issue(tok_ref, 0, i * DMA_ROWS)
            return carry
        lax.fori_loop(0, n_steps, first, 0)

    @pl.when(b < nu)
    def _():
        _slot_wait(h_hbm, buf_ref, sems, slot)
        has_next = b + 1 < nu

        def body(i, carry):
            r0 = pl.multiple_of(i * DMA_ROWS, DMA_ROWS)

            @pl.when(has_next)
            def _():
                issue(tokn_ref, 1 - slot, r0)

            x = buf_ref[slot, pl.ds(r0, DMA_ROWS), :]
            ms = jnp.mean(x * x, axis=-1, keepdims=True)
            o_ref[pl.ds(r0, DMA_ROWS), :] = (x * lax.rsqrt(ms + NORM_EPS) * nw_ref[...]).astype(o_ref.dtype)
            return carry
        lax.fori_loop(0, n_steps, body, 0)

    @pl.when(b >= nu)
    def _():
        o_ref[...] = jnp.zeros_like(o_ref)


def _gather_norm(n_used, row_tok3, h1, norm_w, tm):
    nblk = row_tok3.shape[0]
    cur = lambda b, nu: (jnp.minimum(b, nu[0] - 1), 0, 0)
    nxt = lambda b, nu: (jnp.minimum(b + 1, nu[0] - 1), 0, 0)
    return pl.pallas_call(
        _gather_norm_kernel,
        grid_spec=pltpu.PrefetchScalarGridSpec(
            num_scalar_prefetch=1,
            grid=(nblk,),
            in_specs=[
                pl.BlockSpec((None, 1, tm), cur, memory_space=pltpu.SMEM),
                pl.BlockSpec((None, 1, tm), nxt, memory_space=pltpu.SMEM),
                pl.BlockSpec(memory_space=pl.ANY),
                pl.BlockSpec((1, D_MODEL), lambda b, nu: (0, 0)),
            ],
            out_specs=pl.BlockSpec((tm, D_MODEL), lambda b, nu: (b, 0)),
            scratch_shapes=[pltpu.VMEM((2, tm, D_MODEL), F32), pltpu.SemaphoreType.DMA((2,))],
        ),
        out_shape=jax.ShapeDtypeStruct((nblk * tm, D_MODEL), BF16),
        compiler_params=_cparams(("arbitrary",)),
        name="gather_norm",
    )(n_used, row_tok3, row_tok3, h1, norm_w)


def _swiglu(x, wg, wl, bg, bl):
    glu = jnp.dot(x, wg, preferred_element_type=F32) + bg
    lin = jnp.dot(x, wl, preferred_element_type=F32) + bl
    glu = jnp.minimum(glu, SWIGLU_LIMIT)
    lin = jnp.clip(lin, -SWIGLU_LIMIT, SWIGLU_LIMIT)
    return glu * (1.0 / (1.0 + jnp.exp(-SWIGLU_ALPHA * glu))) * (lin + 1.0)


def _gateup_kernel(bv_ref, be_ref, nu_ref, x_ref, wg_ref, wl_ref, bg_ref, bl_ref, o_ref, wgb_ref, wlb_ref):
    b = pl.program_id(1)
    valid = bv_ref[b]
    new_expert = (b == 0) | (be_ref[b] != be_ref[jnp.maximum(b - 1, 0)])
    hm = EXPERT_HALF

    @pl.when((valid > 0) & new_expert)
    def _():
        wgb_ref[...] = wg_ref[...].astype(BF16)
        wlb_ref[...] = wl_ref[...].astype(BF16)

    @pl.when(valid > hm)
    def _():
        act = _swiglu(x_ref[...], wgb_ref[...], wlb_ref[...], bg_ref[...], bl_ref[...])
        o_ref[...] = act.astype(o_ref.dtype)

    @pl.when((valid > 0) & (valid <= hm))
    def _():
        act = _swiglu(x_ref[:hm, :], wgb_ref[...], wlb_ref[...], bg_ref[...], bl_ref[...])
        o_ref[:hm, :] = act.astype(o_ref.dtype)
        o_ref[hm:, :] = jnp.zeros((o_ref.shape[0] - hm, o_ref.shape[1]), o_ref.dtype)

    @pl.when(valid == 0)
    def _():
        o_ref[...] = jnp.zeros_like(o_ref)


def _gateup(n_used, block_valid, block_exp, xb, w_gu, b_gu3, tm):
    nblk = xb.shape[0] // tm
    nj = D_EXPERT // GU_TN
    last = lambda b, nu: jnp.minimum(b, nu[0] - 1)
    return pl.pallas_call(
        _gateup_kernel,
        grid_spec=pltpu.PrefetchScalarGridSpec(
            num_scalar_prefetch=3,
            grid=(nj, nblk),
            in_specs=[
                pl.BlockSpec((tm, D_MODEL), lambda j, b, bv, be, nu: (last(b, nu), 0)),
                pl.BlockSpec((None, D_MODEL, GU_TN), lambda j, b, bv, be, nu: (be[b], 0, j)),
                pl.BlockSpec((None, D_MODEL, GU_TN), lambda j, b, bv, be, nu: (be[b], 0, j + nj)),
                pl.BlockSpec((None, 1, GU_TN), lambda j, b, bv, be, nu: (be[b], 0, j)),
                pl.BlockSpec((None, 1, GU_TN), lambda j, b, bv, be, nu: (be[b], 0, j + nj)),
            ],
            out_specs=pl.BlockSpec((tm, GU_TN), lambda j, b, bv, be, nu: (b, j)),
            scratch_shapes=[pltpu.VMEM((D_MODEL, GU_TN), BF16), pltpu.VMEM((D_MODEL, GU_TN), BF16)],
        ),
        out_shape=jax.ShapeDtypeStruct((nblk * tm, D_EXPERT), BF16),
        compiler_params=_cparams(("arbitrary", "arbitrary")),
        name="expert_gateup",
    )(block_valid, block_exp, n_used, xb, w_gu, w_gu, b_gu3, b_gu3)


def _down_kernel(bv_ref, be_ref, nu_ref, a_ref, w_ref, bias_ref, o_ref, wb_ref):
    b = pl.program_id(1)
    valid = bv_ref[b]
    new_expert = (b == 0) | (be_ref[b] != be_ref[jnp.maximum(b - 1, 0)])
    hm = EXPERT_HALF

    @pl.when((valid > 0) & new_expert)
    def _():
        wb_ref[...] = w_ref[...].astype(BF16)

    @pl.when(valid > hm)
    def _():
        o_ref[...] = jnp.dot(a_ref[...], wb_ref[...], preferred_element_type=F32) + bias_ref[...]

    @pl.when((valid > 0) & (valid <= hm))
    def _():
        o_ref[:hm, :] = jnp.dot(a_ref[:hm, :], wb_ref[...], preferred_element_type=F32) + bias_ref[...]
        o_ref[hm:, :] = jnp.zeros((o_ref.shape[0] - hm, o_ref.shape[1]), o_ref.dtype)

    @pl.when(valid == 0)
    def _():
        o_ref[...] = jnp.zeros_like(o_ref)


def _down(n_used, block_valid, block_exp, act, w_down, b_down3, tm):
    nblk = act.shape[0] // tm
    nj = D_MODEL // DOWN_TN
    last = lambda b, nu: jnp.minimum(b, nu[0] - 1)
    return pl.pallas_call(
        _down_kernel,
        grid_spec=pltpu.PrefetchScalarGridSpec(
            num_scalar_prefetch=3,
            grid=(nj, nblk),
            in_specs=[
                pl.BlockSpec((tm, D_EXPERT), lambda j, b, bv, be, nu: (last(b, nu), 0)),
                pl.BlockSpec((None, D_EXPERT, DOWN_TN), lambda j, b, bv, be, nu: (be[b], 0, j)),
                pl.BlockSpec((None, 1, DOWN_TN), lambda j, b, bv, be, nu: (be[b], 0, j)),
            ],
            out_specs=pl.BlockSpec((tm, DOWN_TN), lambda j, b, bv, be, nu: (b, j)),
            scratch_shapes=[pltpu.VMEM((D_EXPERT, DOWN_TN), BF16)],
        ),
        out_shape=jax.ShapeDtypeStruct((nblk * tm, D_MODEL), F32),
        compiler_params=_cparams(("arbitrary", "arbitrary")),
        name="expert_down",
    )(block_valid, block_exp, n_used, act, w_down, b_down3)


def _combine_kernel(pos_ref, posn_ref, g_ref, h_ref, nw_ref, y_hbm, o_ref, buf_ref, sems):
    i = pl.program_id(0)
    n_tiles = pl.num_programs(0)
    tq = h_ref.shape[0]
    slot = i % 2
    n_steps = tq // DMA_ROWS

    def issue(idx_ref, s, r0):
        for k in range(TOP_K):
            for r in range(DMA_ROWS):
                _row_copy(y_hbm, idx_ref, buf_ref, sems, s, k * tq + r0 + r).start()

    @pl.when(i == 0)
    def _():
        def first(t, carry):
            issue(pos_ref, 0, t * DMA_ROWS)
            return carry
        lax.fori_loop(0, n_steps, first, 0)

    _slot_wait(y_hbm, buf_ref, sems, slot)
    has_next = i + 1 < n_tiles

    def body(t, carry):
        r0 = pl.multiple_of(t * DMA_ROWS, DMA_ROWS)

        @pl.when(has_next)
        def _():
            issue(posn_ref, 1 - slot, r0)

        acc = h_ref[pl.ds(r0, DMA_ROWS), :]
        g = g_ref[pl.ds(r0, DMA_ROWS), :]
        for k in range(TOP_K):
            acc = acc + g[:, k:k + 1] * buf_ref[slot, pl.ds(k * tq + r0, DMA_ROWS), :]
        ms = jnp.mean(acc * acc, axis=-1, keepdims=True)
        o_ref[pl.ds(r0, DMA_ROWS), :] = acc * lax.rsqrt(ms + NORM_EPS) * nw_ref[...]
        return carry
    lax.fori_loop(0, n_steps, body, 0)


def _combine(pos3, gates, h1, norm_w, yb, tq):
    s = h1.shape[0]
    nt = s // tq
    return pl.pallas_call(
        _combine_kernel,
        grid=(nt,),
        in_specs=[
            pl.BlockSpec((None, 1, TOP_K * tq), lambda i: (i, 0, 0), memory_space=pltpu.SMEM),
            pl.BlockSpec((None, 1, TOP_K * tq), lambda i: (jnp.minimum(i + 1, nt - 1), 0, 0),
                         memory_space=pltpu.SMEM),
            pl.BlockSpec((tq, TOP_K), lambda i: (i, 0)),
            pl.BlockSpec((tq, D_MODEL), lambda i: (i, 0)),
            pl.BlockSpec((1, D_MODEL), lambda i: (0, 0)),
            pl.BlockSpec(memory_space=pl.ANY),
        ],
        out_specs=pl.BlockSpec((tq, D_MODEL), lambda i: (i, 0)),
        out_shape=jax.ShapeDtypeStruct((s, D_MODEL), F32),
        scratch_shapes=[pltpu.VMEM((2, TOP_K * tq, D_MODEL), F32), pltpu.SemaphoreType.DMA((2,))],
        compiler_params=_cparams(("arbitrary",)),
        name="combine_norm",
    )(pos3, pos3, gates, h1, norm_w, yb)


def _routing_metadata(top_idx, tm):
    s = top_idx.shape[0]
    n_assign = s * TOP_K
    nblk = -(-(n_assign + N_EXPERTS * (tm - 1)) // tm)
    e_flat = top_idx.reshape(n_assign)
    order = jnp.argsort(e_flat).astype(jnp.int32)
    rank = jnp.argsort(order).astype(jnp.int32)
    experts = jnp.arange(N_EXPERTS, dtype=jnp.int32)
    counts = jnp.sum((e_flat[:, None] == experts[None, :]).astype(jnp.int32), axis=0)
    padded = ((counts + tm - 1) // tm) * tm
    start_sorted = jnp.cumsum(counts) - counts
    start_pad = jnp.cumsum(padded) - padded
    pos = start_pad[e_flat] + rank - start_sorted[e_flat]
    n_used = jnp.sum(padded) // tm
    blocks = jnp.arange(nblk, dtype=jnp.int32)
    block_exp = jnp.sum((start_pad[None, :] <= (blocks * tm)[:, None]).astype(jnp.int32), axis=1) - 1
    block_exp = block_exp[jnp.minimum(blocks, n_used - 1)]
    block_valid = jnp.clip(counts[block_exp] - (blocks * tm - start_pad[block_exp]), 0, tm)
    block_valid = jnp.where(blocks < n_used, block_valid, 0)
    rows = jnp.arange(nblk * tm, dtype=jnp.int32)
    row_exp = jnp.repeat(block_exp, tm)
    off = rows - start_pad[row_exp]
    is_row = (off < counts[row_exp]) & (rows < n_used * tm)
    src = jnp.clip(start_sorted[row_exp] + off, 0, n_assign - 1)
    row_tok = jnp.where(is_row, order[src] // TOP_K, 0)
    as_i32 = lambda t: t.astype(jnp.int32)
    return (as_i32(n_used).reshape(1), as_i32(block_valid), as_i32(block_exp),
            as_i32(row_tok).reshape(nblk, 1, tm), as_i32(pos).reshape(s, TOP_K))


def _pick(n, pref):
    t = pref
    while n % t:
        t //= 2
    return t


def kernel(x, meta_tokens, attn_norm_w, w_in, b_in, attn_sinks, ret_norm_w, w_out, b_out, ffn_norm_w, router_w,
           router_b, w_gu, b_gu, w_down, b_down, final_norm_w):
    assert x.shape[0] == 1 and attn_norm_w.shape[0] == 1
    xs = x[0]
    s = xs.shape[0]
    assert s % WINDOW == 0
    tm = _pick(s, 512)

    tabs = _rope_tables(N_META + s)
    w_in_bf = _permute_in_columns(w_in[0]).astype(BF16)
    b_in2 = _permute_in_columns(b_in[0])[None, :]
    nw = attn_norm_w[0][None, :]
    z = _inproj(xs, nw, w_in_bf, b_in2, [t[N_META:] for t in tabs], tm)
    zm = _inproj(meta_tokens, nw, w_in_bf, b_in2, [t[:N_META] for t in tabs], N_META)
    zm = jnp.pad(zm, ((0, WINDOW - N_META), (0, 0)))

    att = _attention(z, zm, attn_sinks[0])
    ret = _retention(z, zm, ret_norm_w[0][None, :])
    h1 = _outproj(att, ret, w_out[0].astype(BF16), b_out[0][None, :], xs, tm, 1024)

    ffn_w = ffn_norm_w[0][None, :]
    idx_t, gate_t = _router(h1, ffn_w, router_w[0].T, router_b[0][:, None], tm)
    gates = gate_t.T
    n_used, block_valid, block_exp, row_tok3, pos = _routing_metadata(idx_t.T, EXPERT_TM)

    xb = _gather_norm(n_used, row_tok3, h1, ffn_w, EXPERT_TM)
    act = _gateup(n_used, block_valid, block_exp, xb, w_gu[0], b_gu[0][:, None, :], EXPERT_TM)
    yb = _down(n_used, block_valid, block_exp, act, w_down[0], b_down[0][:, None, :], EXPERT_TM)

    tq = WINDOW
    pos3 = pos.reshape(s // tq, tq, TOP_K).transpose(0, 2, 1).reshape(s // tq, 1, TOP_K * tq)
    out = _combine(pos3, gates, h1, final_norm_w[None, :], yb, tq)
    return out[None]
```

```python
import functools

import jax
import jax.numpy as jnp
import numpy as np
from jax import lax
from jax.experimental import pallas as pl
from jax.experimental.pallas import tpu as pltpu

F32 = jnp.float32
BF16 = jnp.bfloat16

D_MODEL = 4096
N_META = 16
ATT_HEADS = 32
ATT_KV_HEADS = 4
ATT_HEAD_DIM = 64
ATT_GROUP = ATT_HEADS // ATT_KV_HEADS
WINDOW = 128
ROPE_THETA = 500000.0
ROPE_DIM = ATT_HEAD_DIM // 4
RET_HEADS = 8
RET_KEY_DIM = 128
RET_VALUE_DIM = 256
RET_CHUNK = 128
RET_ROPE_THETA = 10000.0
ATT_WIDTH = ATT_HEADS * ATT_HEAD_DIM
KV_WIDTH = ATT_KV_HEADS * ATT_HEAD_DIM
RET_QK_WIDTH = RET_HEADS * RET_KEY_DIM
RET_WIDTH = RET_HEADS * RET_VALUE_DIM
IN_WIDTH = ATT_WIDTH + 2 * KV_WIDTH + 2 * RET_QK_WIDTH + 2 * RET_WIDTH
OFF_AQ = 0
OFF_RQ = OFF_AQ + ATT_WIDTH
OFF_RK = OFF_RQ + RET_QK_WIDTH
OFF_RV = OFF_RK + RET_QK_WIDTH
OFF_RG = OFF_RV + RET_WIDTH
OFF_AK = OFF_RG + RET_WIDTH
OFF_AV = OFF_AK + KV_WIDTH
N_EXPERTS = 32
TOP_K = 4
D_EXPERT = D_MODEL // 2
SWIGLU_ALPHA = 1.702
SWIGLU_LIMIT = 7.0
NORM_EPS = 1e-5
GN_EPS = 1e-6

LANES = 128
VMEM_LIMIT = 56 * 1024 * 1024

IN_TN = 512
EXPERT_TM = 512
EXPERT_HALF = EXPERT_TM // 2
GU_TN = 512
DOWN_TN = 1024
KEYS_PAD = 3 * WINDOW
NEG_BIG = -1e30
SLAB_ROWS = D_MODEL // LANES
SLAB_PITCH = 40
GATHER_GROUP = 64
COMBINE_GROUP = 32


def _cparams(sem):
    return pltpu.CompilerParams(dimension_semantics=sem, vmem_limit_bytes=VMEM_LIMIT)


def _permute_in_columns(t):
    a0 = ATT_WIDTH
    a1 = a0 + 2 * KV_WIDTH
    return jnp.concatenate([t[..., :a0], t[..., a1:], t[..., a0:a1]], axis=-1)


def _chunk_classes():
    cls = []
    for c in range(IN_WIDTH // LANES):
        off = c * LANES
        if off < OFF_RQ or OFF_AK <= off < OFF_AV:
            cls.append("a")
        elif OFF_RQ <= off < OFF_RV:
            cls.append("r")
        else:
            cls.append("n")
    return cls


def _inproj_kernel(x_ref, nw_ref, w_ref, b_ref, aa_ref, ab_ref, ac_ref, ra_ref, rb_ref, o_ref, hn_ref):
    j = pl.program_id(1)

    @pl.when(j == 0)
    def _():
        x = x_ref[...]
        ms = jnp.mean(x * x, axis=-1, keepdims=True)
        hn_ref[...] = (x * lax.rsqrt(ms + NORM_EPS) * nw_ref[...]).astype(BF16)

    y = jnp.dot(hn_ref[...], w_ref[...], preferred_element_type=F32) + b_ref[...]

    cpt = IN_TN // LANES
    classes = _chunk_classes()
    n_tiles = IN_WIDTH // IN_TN
    patterns = {}
    for t in range(n_tiles):
        patterns.setdefault(tuple(classes[t * cpt:(t + 1) * cpt]), []).append(t)

    def emit(pattern):
        for c, kind in enumerate(pattern):
            yc = y[:, c * LANES:(c + 1) * LANES]
            if kind == "a":
                yc = (yc * aa_ref[...] + pltpu.roll(yc, LANES - ROPE_DIM // 2, 1) * ab_ref[...]
                      + pltpu.roll(yc, ROPE_DIM // 2, 1) * ac_ref[...])
            elif kind == "r":
                yc = yc * ra_ref[...] + pltpu.roll(yc, RET_KEY_DIM // 2, 1) * rb_ref[...]
            o_ref[:, c * LANES:(c + 1) * LANES] = yc.astype(o_ref.dtype)

    for pattern, tiles in patterns.items():
        cond = functools.reduce(jnp.logical_or, [j == t for t in tiles])
        pl.when(cond)(functools.partial(emit, pattern))


def _inproj(x, norm_w, w_bf, b, tabs, tm):
    m = x.shape[0]
    aa, ab, ac, ra, rb = tabs
    row = lambda i, j: (i, 0)
    tab_spec = pl.BlockSpec((tm, LANES), row)
    return pl.pallas_call(
        _inproj_kernel,
        grid=(m // tm, IN_WIDTH // IN_TN),
        in_specs=[
            pl.BlockSpec((tm, D_MODEL), row),
            pl.BlockSpec((1, D_MODEL), lambda i, j: (0, 0)),
            pl.BlockSpec((D_MODEL, IN_TN), lambda i, j: (0, j)),
            pl.BlockSpec((1, IN_TN), lambda i, j: (0, j)),
            tab_spec, tab_spec, tab_spec, tab_spec, tab_spec,
        ],
        out_specs=pl.BlockSpec((tm, IN_TN), lambda i, j: (i, j)),
        out_shape=jax.ShapeDtypeStruct((m, IN_WIDTH), BF16),
        scratch_shapes=[pltpu.VMEM((tm, D_MODEL), BF16)],
        compiler_params=_cparams(("parallel", "arbitrary")),
        name="inproj",
    )(x, norm_w, w_bf, b, aa, ab, ac, ra, rb)


def _rope_tables(first_pos, n_pos):
    pos = (first_pos + jnp.arange(n_pos, dtype=jnp.int32)).astype(F32)[:, None]
    lane = np.arange(LANES)
    half = ROPE_DIM // 2
    inv = ROPE_THETA ** (-jnp.arange(half, dtype=F32) / half)
    ang = pos * inv[None, :]
    cos, sin = jnp.cos(ang), jnp.sin(ang)
    c = lane % ATT_HEAD_DIM
    f = c % half
    is_lo = jnp.asarray(c < half)[None, :]
    is_hi = jnp.asarray((c >= half) & (c < ROPE_DIM))[None, :]
    cos_l, sin_l = cos[:, f], sin[:, f]
    aa = jnp.where(is_lo | is_hi, cos_l, 1.0)
    ab = jnp.where(is_lo, -sin_l, 0.0)
    ac = jnp.where(is_hi, sin_l, 0.0)
    rhalf = RET_KEY_DIM // 2
    rinv = RET_ROPE_THETA ** (-jnp.arange(rhalf, dtype=F32) / rhalf)
    rang = pos * rinv[None, :]
    rcos, rsin = jnp.cos(rang), jnp.sin(rang)
    rf = lane % rhalf
    ra = rcos[:, rf]
    rb = jnp.where(jnp.asarray(lane < rhalf)[None, :], -rsin[:, rf], rsin[:, rf])
    return aa, ab, ac, ra, rb


def _attn_kernel(sink_ref, q_ref, ko_ref, vo_ref, kp_ref, vp_ref, km_ref, vm_ref, o_ref):
    n = pl.program_id(0)
    w = WINDOW
    qi = lax.broadcasted_iota(jnp.int32, (w, KEYS_PAD), 0)
    ji = lax.broadcasted_iota(jnp.int32, (w, KEYS_PAD), 1)
    vis_prev = (ji < w) & (ji > qi) & (n > 0)
    vis_own = (ji >= w) & (ji < 2 * w) & (ji - w <= qi)
    vis_meta = (ji >= 2 * w) & (ji < 2 * w + N_META)
    mask = vis_prev | vis_own | vis_meta
    lane = lax.broadcasted_iota(jnp.int32, (KEYS_PAD, LANES), 1)
    lo_half = lane < ATT_HEAD_DIM
    olane = lax.broadcasted_iota(jnp.int32, (w, LANES), 1) < ATT_HEAD_DIM
    scale = ATT_HEAD_DIM ** -0.5

    def spread(prev_ref, own_ref, meta_ref, g):
        c0 = (g // 2) * LANES
        chunk = jnp.concatenate([prev_ref[:, c0:c0 + LANES], own_ref[:, c0:c0 + LANES],
                                 meta_ref[:, c0:c0 + LANES]], axis=0).astype(F32)
        swapped = pltpu.roll(chunk, ATT_HEAD_DIM, 1)
        in_lo, in_hi = (chunk, swapped) if g % 2 == 0 else (swapped, chunk)
        even = jnp.where(lo_half, in_lo, 0.0)
        odd = jnp.where(lo_half, 0.0, in_hi)
        return jnp.concatenate([even, odd], axis=0).astype(BF16)

    for g in range(ATT_KV_HEADS):
        kz = spread(kp_ref, ko_ref, km_ref, g)
        vz = spread(vp_ref, vo_ref, vm_ref, g)
        for r in range(ATT_GROUP // 2):
            c = g * (ATT_GROUP // 2) + r
            qp = q_ref[:, c * LANES:(c + 1) * LANES]
            s2 = lax.dot_general(qp, kz, (((1,), (1,)), ((), ())), preferred_element_type=F32) * scale
            ps, dens = [], []
            for half in range(2):
                sink = sink_ref[2 * c + half]
                s = jnp.where(mask, s2[:, half * KEYS_PAD:(half + 1) * KEYS_PAD], NEG_BIG)
                m = jnp.maximum(jnp.max(s, axis=-1, keepdims=True), sink)
                p = jnp.exp(s - m)
                dens.append(jnp.sum(p, axis=-1, keepdims=True) + jnp.exp(sink - m))
                ps.append(p.astype(BF16))
            o = jnp.dot(jnp.concatenate(ps, axis=1), vz, preferred_element_type=F32)
            o = o / jnp.where(olane, dens[0], dens[1])
            o_ref[:, c * LANES:(c + 1) * LANES] = o.astype(o_ref.dtype)


def _attention(z, zm, sinks):
    s = z.shape[0]
    nb = s // WINDOW
    kcol, vcol = OFF_AK // KV_WIDTH, OFF_AV // KV_WIDTH
    prev = lambda n: jnp.maximum(n - 1, 0)
    return pl.pallas_call(
        _attn_kernel,
        grid=(nb,),
        in_specs=[
            pl.BlockSpec(memory_space=pltpu.SMEM),
            pl.BlockSpec((WINDOW, ATT_WIDTH), lambda n: (n, 0)),
            pl.BlockSpec((WINDOW, KV_WIDTH), lambda n: (n, kcol)),
            pl.BlockSpec((WINDOW, KV_WIDTH), lambda n: (n, vcol)),
            pl.BlockSpec((WINDOW, KV_WIDTH), lambda n: (prev(n), kcol)),
            pl.BlockSpec((WINDOW, KV_WIDTH), lambda n: (prev(n), vcol)),
            pl.BlockSpec((WINDOW, KV_WIDTH), lambda n: (0, kcol)),
            pl.BlockSpec((WINDOW, KV_WIDTH), lambda n: (0, vcol)),
        ],
        out_specs=pl.BlockSpec((WINDOW, ATT_WIDTH), lambda n: (n, 0)),
        out_shape=jax.ShapeDtypeStruct((s, ATT_WIDTH), BF16),
        compiler_params=_cparams(("parallel",)),
        name="swa_attention",
    )(sinks, z, z, z, z, z, zm, zm)


def _ret_kernel(gc_ref, q_ref, k_ref, v_ref, g_ref, km_ref, vm_ref, dm_ref, kd_ref, qd_ref, wm_ref, nw_ref,
                o_ref, u_ref):
    c = pl.program_id(0)
    dk, dv = RET_KEY_DIM, RET_VALUE_DIM

    @pl.when(c == 0)
    def _():
        for hh in range(RET_HEADS):
            kmw = (km_ref[:, hh * dk:(hh + 1) * dk].astype(F32) * wm_ref[hh]).T.astype(BF16)
            u_ref[hh] = jnp.dot(kmw, vm_ref[:, hh * dv:(hh + 1) * dv], preferred_element_type=F32)

    for hh in range(RET_HEADS):
        q = q_ref[:, hh * dk:(hh + 1) * dk]
        k = k_ref[:, hh * dk:(hh + 1) * dk]
        v = v_ref[:, hh * dv:(hh + 1) * dv]
        u = u_ref[hh]
        inner = lax.dot_general(q, k, (((1,), (1,)), ((), ())), preferred_element_type=F32) * dm_ref[hh]
        intra = jnp.dot(inner.astype(BF16), v, preferred_element_type=F32)
        cross = jnp.dot(q, u.astype(BF16), preferred_element_type=F32) * qd_ref[hh]
        y = intra + cross
        mu = jnp.mean(y, axis=-1, keepdims=True)
        yc = y - mu
        var = jnp.mean(yc * yc, axis=-1, keepdims=True)
        yn = yc * lax.rsqrt(var + GN_EPS)
        g = g_ref[:, hh * dv:(hh + 1) * dv].astype(F32)
        silu = g * (1.0 / (1.0 + jnp.exp(-g)))
        o_ref[:, hh * dv:(hh + 1) * dv] = (silu * yn * nw_ref[:, hh * dv:(hh + 1) * dv]).astype(o_ref.dtype)
        kdt = (k.astype(F32) * kd_ref[hh]).T.astype(BF16)
        u_ref[hh] = u * gc_ref[hh] + jnp.dot(kdt, v, preferred_element_type=F32)


def _ret_tables():
    scale = RET_KEY_DIM ** -0.5
    log_g = jnp.log1p(-(2.0 ** (-5.0 - jnp.arange(RET_HEADS, dtype=F32))))
    i = jnp.arange(RET_CHUNK, dtype=F32)
    diff = i[:, None] - i[None, :]
    dm = jnp.where(diff >= 0, jnp.exp(jnp.maximum(diff, 0.0)[None] * log_g[:, None, None]), 0.0) * scale
    qd = (jnp.exp((i + 1.0)[None, :] * log_g[:, None]) * scale)[:, :, None]
    kd = jnp.exp((RET_CHUNK - 1.0 - i)[None, :] * log_g[:, None])[:, :, None]
    jm = jnp.arange(RET_CHUNK, dtype=F32)
    wm = jnp.where(jm[None, :] < N_META, jnp.exp((N_META - 1 - jm)[None, :] * log_g[:, None]), 0.0)[:, :, None]
    gc = jnp.exp(RET_CHUNK * log_g)
    return gc, dm, kd, qd, wm


def _retention(z, zm, ret_norm_w):
    s = z.shape[0]
    nc = s // RET_CHUNK
    gc, dm, kd, qd, wm = _ret_tables()
    qc, kc = OFF_RQ // RET_QK_WIDTH, OFF_RK // RET_QK_WIDTH
    vc, gcol = OFF_RV // RET_WIDTH, OFF_RG // RET_WIDTH
    full3 = lambda c: (0, 0, 0)
    vec_spec = pl.BlockSpec((RET_HEADS, RET_CHUNK, 1), full3)
    return pl.pallas_call(
        _ret_kernel,
        grid=(nc,),
        in_specs=[
            pl.BlockSpec(memory_space=pltpu.SMEM),
            pl.BlockSpec((RET_CHUNK, RET_QK_WIDTH), lambda c: (c, qc)),
            pl.BlockSpec((RET_CHUNK, RET_QK_WIDTH), lambda c: (c, kc)),
            pl.BlockSpec((RET_CHUNK, RET_WIDTH), lambda c: (c, vc)),
            pl.BlockSpec((RET_CHUNK, RET_WIDTH), lambda c: (c, gcol)),
            pl.BlockSpec((RET_CHUNK, RET_QK_WIDTH), lambda c: (0, kc)),
            pl.BlockSpec((RET_CHUNK, RET_WIDTH), lambda c: (0, vc)),
            pl.BlockSpec((RET_HEADS, RET_CHUNK, RET_CHUNK), full3),
            vec_spec, vec_spec, vec_spec,
            pl.BlockSpec((1, RET_WIDTH), lambda c: (0, 0)),
        ],
        out_specs=pl.BlockSpec((RET_CHUNK, RET_WIDTH), lambda c: (c, 0)),
        out_shape=jax.ShapeDtypeStruct((s, RET_WIDTH), BF16),
        scratch_shapes=[pltpu.VMEM((RET_HEADS, RET_KEY_DIM, RET_VALUE_DIM), F32)],
        compiler_params=_cparams(("arbitrary",)),
        name="retention",
    )(gc, z, z, z, z, zm, zm, dm, kd, qd, wm, ret_norm_w)


def _store_slabs(slab_ref, rows, value):
    n = slab_ref.shape[1]
    flat = slab_ref.reshape(slab_ref.shape[0] * n, LANES)
    for k in range(n):
        flat[pl.ds(k, rows, stride=n), :] = value[:, k * LANES:(k + 1) * LANES]


def _outproj_kernel(a_ref, r_ref, wt_ref, wb_ref, b_ref, x_ref, o_ref, slab_ref):
    acc = jnp.dot(a_ref[...], wt_ref[...], preferred_element_type=F32)
    acc += jnp.dot(r_ref[...], wb_ref[...], preferred_element_type=F32)
    h = x_ref[...] + acc + b_ref[...]
    o_ref[...] = h
    _store_slabs(slab_ref, h.shape[0], h)


def _outproj(att, ret, w_bf, b, x, tm, tn):
    s = x.shape[0]
    return pl.pallas_call(
        _outproj_kernel,
        grid=(s // tm, D_MODEL // tn),
        in_specs=[
            pl.BlockSpec((tm, ATT_WIDTH), lambda i, j: (i, 0)),
            pl.BlockSpec((tm, RET_WIDTH), lambda i, j: (i, 0)),
            pl.BlockSpec((ATT_WIDTH, tn), lambda i, j: (0, j)),
            pl.BlockSpec((RET_WIDTH, tn), lambda i, j: (1, j)),
            pl.BlockSpec((1, tn), lambda i, j: (0, j)),
            pl.BlockSpec((tm, tn), lambda i, j: (i, j)),
        ],
        out_specs=[pl.BlockSpec((tm, tn), lambda i, j: (i, j)),
                   pl.BlockSpec((tm, tn // LANES, LANES), lambda i, j: (i, j, 0))],
        out_shape=[jax.ShapeDtypeStruct((s, D_MODEL), F32),
                   jax.ShapeDtypeStruct((s, D_MODEL // LANES, LANES), F32)],
        compiler_params=_cparams(("parallel", "arbitrary")),
        name="outproj",
    )(att, ret, w_bf, w_bf, b, x)


def _router_kernel(h_ref, nw_ref, rw_ref, rb_ref, idx_ref, gate_ref):
    x = h_ref[...]
    ms = jnp.mean(x * x, axis=-1, keepdims=True)
    hn = x * lax.rsqrt(ms + NORM_EPS) * nw_ref[...]
    logits = lax.dot_general(rw_ref[...], hn, (((1,), (1,)), ((), ())), preferred_element_type=F32,
                             precision=lax.Precision.HIGHEST) + rb_ref[...]
    eid = lax.broadcasted_iota(jnp.int32, logits.shape, 0)
    vals, idxs = [], []
    for _ in range(TOP_K):
        m = jnp.max(logits, axis=0, keepdims=True)
        sel = jnp.min(jnp.where(logits == m, eid, N_EXPERTS), axis=0, keepdims=True)
        vals.append(m)
        idxs.append(sel)
        logits = jnp.where(eid == sel, -jnp.inf, logits)
    es = [jnp.exp(v - vals[0]) for v in vals]
    tot = functools.reduce(lambda a, b: a + b, es)
    idx_ref[...] = jnp.concatenate(idxs, axis=0)
    gate_ref[...] = jnp.concatenate([e / tot for e in es], axis=0)


def _router(h1, norm_w, rw_t, rb, tm):
    s = h1.shape[0]
    return pl.pallas_call(
        _router_kernel,
        grid=(s // tm,),
        in_specs=[
            pl.BlockSpec((tm, D_MODEL), lambda i: (i, 0)),
            pl.BlockSpec((1, D_MODEL), lambda i: (0, 0)),
            pl.BlockSpec((N_EXPERTS, D_MODEL), lambda i: (0, 0)),
            pl.BlockSpec((N_EXPERTS, 1), lambda i: (0, 0)),
        ],
        out_specs=[pl.BlockSpec((TOP_K, tm), lambda i: (0, i)), pl.BlockSpec((TOP_K, tm), lambda i: (0, i))],
        out_shape=[jax.ShapeDtypeStruct((TOP_K, s), jnp.int32), jax.ShapeDtypeStruct((TOP_K, s), F32)],
        compiler_params=_cparams(("parallel",)),
        name="router",
    )(h1, norm_w, rw_t, rb)


def _slab_copy(src_hbm, idx_ref, buf_ref, sems, slot, r):
    return pltpu.make_async_copy(src_hbm.at[idx_ref[0, r]], buf_ref.at[slot, r, pl.ds(0, SLAB_ROWS), :],
                                 sems.at[slot])


def _slot_wait(src_hbm, buf_ref, sems, slot):
    n = buf_ref.shape[1]
    pltpu.make_async_copy(src_hbm.at[pl.ds(0, n)], buf_ref.at[slot, :, pl.ds(0, SLAB_ROWS), :],
                          sems.at[slot]).wait()


def _slab_chunk(buf_ref, slot, r0, rows, k):
    n_slots, slot_slabs = buf_ref.shape[:2]
    flat = buf_ref.reshape(n_slots * slot_slabs * SLAB_PITCH, LANES)
    base = (slot * slot_slabs + r0) * SLAB_PITCH + k
    return flat[pl.ds(base, rows, stride=SLAB_PITCH), :]


def _gather_norm_kernel(nused_ref, tok_ref, tokn_ref, h_hbm, nw_ref, o_ref, buf_ref, sems):
    b = pl.program_id(0)
    nu = nused_ref[0]
    tm = o_ref.shape[0]
    slot = b % 2
    grp = GATHER_GROUP
    n_groups = tm // grp

    def issue(idx_ref, s, r0):
        for r in range(grp):
            _slab_copy(h_hbm, idx_ref, buf_ref, sems, s, r0 + r).start()

    @pl.when(b == 0)
    def _():
        def first(i, carry):
            issue(tok_ref, 0, i * grp)
            return carry
        lax.fori_loop(0, n_groups, first, 0)

    @pl.when(b < nu)
    def _():
        _slot_wait(h_hbm, buf_ref, sems, slot)
        has_next = b + 1 < nu

        def body(i, carry):
            r0 = pl.multiple_of(i * grp, grp)

            @pl.when(has_next)
            def _():
                issue(tokn_ref, 1 - slot, r0)

            ssq = jnp.zeros((grp, LANES), F32)
            for k in range(SLAB_ROWS):
                x = _slab_chunk(buf_ref, slot, r0, grp, k)
                ssq = ssq + x * x
            sc = lax.rsqrt(jnp.sum(ssq, axis=-1, keepdims=True) * (1.0 / D_MODEL) + NORM_EPS)
            for k in range(SLAB_ROWS):
                x = _slab_chunk(buf_ref, slot, r0, grp, k)
                cols = slice(k * LANES, (k + 1) * LANES)
                o_ref[pl.ds(r0, grp), cols] = (x * sc * nw_ref[:, cols]).astype(o_ref.dtype)
            return carry
        lax.fori_loop(0, n_groups, body, 0)

    @pl.when(b >= nu)
    def _():
        o_ref[...] = jnp.zeros_like(o_ref)


def _gather_norm(n_used, row_tok3, h1_slabs, norm_w, tm):
    nblk = row_tok3.shape[0]
    cur = lambda b, nu: (jnp.minimum(b, nu[0] - 1), 0, 0)
    nxt = lambda b, nu: (jnp.minimum(b + 1, nu[0] - 1), 0, 0)
    return pl.pallas_call(
        _gather_norm_kernel,
        grid_spec=pltpu.PrefetchScalarGridSpec(
            num_scalar_prefetch=1,
            grid=(nblk,),
            in_specs=[
                pl.BlockSpec((None, 1, tm), cur, memory_space=pltpu.SMEM),
                pl.BlockSpec((None, 1, tm), nxt, memory_space=pltpu.SMEM),
                pl.BlockSpec(memory_space=pl.ANY),
                pl.BlockSpec((1, D_MODEL), lambda b, nu: (0, 0)),
            ],
            out_specs=pl.BlockSpec((tm, D_MODEL), lambda b, nu: (b, 0)),
            scratch_shapes=[pltpu.VMEM((2, tm, SLAB_PITCH, LANES), F32), pltpu.SemaphoreType.DMA((2,))],
        ),
        out_shape=jax.ShapeDtypeStruct((nblk * tm, D_MODEL), BF16),
        compiler_params=_cparams(("arbitrary",)),
        name="gather_norm",
    )(n_used, row_tok3, row_tok3, h1_slabs, norm_w)


def _swiglu(x, wg, wl, bg, bl):
    glu = jnp.dot(x, wg, preferred_element_type=F32) + bg
    lin = jnp.dot(x, wl, preferred_element_type=F32) + bl
    glu = jnp.minimum(glu, SWIGLU_LIMIT)
    lin = jnp.clip(lin, -SWIGLU_LIMIT, SWIGLU_LIMIT)
    return glu * (1.0 / (1.0 + jnp.exp(-SWIGLU_ALPHA * glu))) * (lin + 1.0)


def _gateup_kernel(bv_ref, be_ref, nu_ref, x_ref, wg_ref, wl_ref, bg_ref, bl_ref, o_ref, wgb_ref, wlb_ref):
    b = pl.program_id(1)
    valid = bv_ref[b]
    new_expert = (b == 0) | (be_ref[b] != be_ref[jnp.maximum(b - 1, 0)])
    hm = EXPERT_HALF

    @pl.when((valid > 0) & new_expert)
    def _():
        wgb_ref[...] = wg_ref[...].astype(BF16)
        wlb_ref[...] = wl_ref[...].astype(BF16)

    @pl.when(valid > hm)
    def _():
        act = _swiglu(x_ref[...], wgb_ref[...], wlb_ref[...], bg_ref[...], bl_ref[...])
        o_ref[...] = act.astype(o_ref.dtype)

    @pl.when((valid > 0) & (valid <= hm))
    def _():
        act = _swiglu(x_ref[:hm, :], wgb_ref[...], wlb_ref[...], bg_ref[...], bl_ref[...])
        o_ref[:hm, :] = act.astype(o_ref.dtype)
        o_ref[hm:, :] = jnp.zeros((o_ref.shape[0] - hm, o_ref.shape[1]), o_ref.dtype)

    @pl.when(valid == 0)
    def _():
        o_ref[...] = jnp.zeros_like(o_ref)


def _gateup(n_used, block_valid, block_exp, xb, w_gu, b_gu3, tm):
    nblk = xb.shape[0] // tm
    nj = D_EXPERT // GU_TN
    last = lambda b, nu: jnp.minimum(b, nu[0] - 1)
    return pl.pallas_call(
        _gateup_kernel,
        grid_spec=pltpu.PrefetchScalarGridSpec(
            num_scalar_prefetch=3,
            grid=(nj, nblk),
            in_specs=[
                pl.BlockSpec((tm, D_MODEL), lambda j, b, bv, be, nu: (last(b, nu), 0)),
                pl.BlockSpec((None, D_MODEL, GU_TN), lambda j, b, bv, be, nu: (be[b], 0, j)),
                pl.BlockSpec((None, D_MODEL, GU_TN), lambda j, b, bv, be, nu: (be[b], 0, j + nj)),
                pl.BlockSpec((None, 1, GU_TN), lambda j, b, bv, be, nu: (be[b], 0, j)),
                pl.BlockSpec((None, 1, GU_TN), lambda j, b, bv, be, nu: (be[b], 0, j + nj)),
            ],
            out_specs=pl.BlockSpec((tm, GU_TN), lambda j, b, bv, be, nu: (b, j)),
            scratch_shapes=[pltpu.VMEM((D_MODEL, GU_TN), BF16), pltpu.VMEM((D_MODEL, GU_TN), BF16)],
        ),
        out_shape=jax.ShapeDtypeStruct((nblk * tm, D_EXPERT), BF16),
        compiler_params=_cparams(("arbitrary", "arbitrary")),
        name="expert_gateup",
    )(block_valid, block_exp, n_used, xb, w_gu, w_gu, b_gu3, b_gu3)


def _down_kernel(bv_ref, be_ref, nu_ref, a_ref, w_ref, bias_ref, o_ref, wb_ref):
    b = pl.program_id(1)
    valid = bv_ref[b]
    new_expert = (b == 0) | (be_ref[b] != be_ref[jnp.maximum(b - 1, 0)])
    hm = EXPERT_HALF

    @pl.when((valid > 0) & new_expert)
    def _():
        wb_ref[...] = w_ref[...].astype(BF16)

    @pl.when(valid > hm)
    def _():
        res = jnp.dot(a_ref[...], wb_ref[...], preferred_element_type=F32) + bias_ref[...]
        _store_slabs(o_ref, o_ref.shape[0], res)

    @pl.when((valid > 0) & (valid <= hm))
    def _():
        res = jnp.dot(a_ref[:hm, :], wb_ref[...], preferred_element_type=F32) + bias_ref[...]
        _store_slabs(o_ref, hm, res)
        o_ref[hm:] = jnp.zeros((o_ref.shape[0] - hm,) + o_ref.shape[1:], o_ref.dtype)

    @pl.when(valid == 0)
    def _():
        o_ref[...] = jnp.zeros_like(o_ref)


def _down(n_used, block_valid, block_exp, act, w_down, b_down3, tm):
    nblk = act.shape[0] // tm
    nj = D_MODEL // DOWN_TN
    last = lambda b, nu: jnp.minimum(b, nu[0] - 1)
    return pl.pallas_call(
        _down_kernel,
        grid_spec=pltpu.PrefetchScalarGridSpec(
            num_scalar_prefetch=3,
            grid=(nj, nblk),
            in_specs=[
                pl.BlockSpec((tm, D_EXPERT), lambda j, b, bv, be, nu: (last(b, nu), 0)),
                pl.BlockSpec((None, D_EXPERT, DOWN_TN), lambda j, b, bv, be, nu: (be[b], 0, j)),
                pl.BlockSpec((None, 1, DOWN_TN), lambda j, b, bv, be, nu: (be[b], 0, j)),
            ],
            out_specs=pl.BlockSpec((tm, DOWN_TN // LANES, LANES), lambda j, b, bv, be, nu: (b, j, 0)),
            scratch_shapes=[pltpu.VMEM((D_EXPERT, DOWN_TN), BF16)],
        ),
        out_shape=jax.ShapeDtypeStruct((nblk * tm, SLAB_ROWS, LANES), F32),
        compiler_params=_cparams(("arbitrary", "arbitrary")),
        name="expert_down",
    )(block_valid, block_exp, n_used, act, w_down, b_down3)


def _combine_kernel(pos_ref, posn_ref, g_ref, h_ref, nw_ref, y_hbm, o_ref, buf_ref, sems):
    i = pl.program_id(0)
    n_tiles = pl.num_programs(0)
    tq = h_ref.shape[0]
    slot = i % 2
    grp = COMBINE_GROUP
    n_groups = tq // grp

    def issue(idx_ref, s, r0):
        for k in range(TOP_K):
            for r in range(grp):
                _slab_copy(y_hbm, idx_ref, buf_ref, sems, s, k * tq + r0 + r).start()

    @pl.when(i == 0)
    def _():
        def first(t, carry):
            issue(pos_ref, 0, t * grp)
            return carry
        lax.fori_loop(0, n_groups, first, 0)

    _slot_wait(y_hbm, buf_ref, sems, slot)
    has_next = i + 1 < n_tiles

    def body(t, carry):
        r0 = pl.multiple_of(t * grp, grp)

        @pl.when(has_next)
        def _():
            issue(posn_ref, 1 - slot, r0)

        rows = pl.ds(r0, grp)
        g = g_ref[rows, :]
        gk = [jnp.broadcast_to(g[:, k:k + 1], (grp, LANES)) for k in range(TOP_K)]
        ssq = jnp.zeros((grp, LANES), F32)
        for c in range(SLAB_ROWS):
            cols = slice(c * LANES, (c + 1) * LANES)
            acc = h_ref[rows, cols]
            for k in range(TOP_K):
                acc = acc + gk[k] * _slab_chunk(buf_ref, slot, k * tq + r0, grp, c)
            o_ref[rows, cols] = acc
            ssq = ssq + acc * acc
        sc = lax.rsqrt(jnp.sum(ssq, axis=-1, keepdims=True) * (1.0 / D_MODEL) + NORM_EPS)
        o_ref[rows, :] = o_ref[rows, :] * sc * nw_ref[...]
        return carry
    lax.fori_loop(0, n_groups, body, 0)


def _combine(pos3, gates, h1, norm_w, yb, tq):
    s = h1.shape[0]
    nt = s // tq
    return pl.pallas_call(
        _combine_kernel,
        grid=(nt,),
        in_specs=[
            pl.BlockSpec((None, 1, TOP_K * tq), lambda i: (i, 0, 0), memory_space=pltpu.SMEM),
            pl.BlockSpec((None, 1, TOP_K * tq), lambda i: (jnp.minimum(i + 1, nt - 1), 0, 0),
                         memory_space=pltpu.SMEM),
            pl.BlockSpec((tq, TOP_K), lambda i: (i, 0)),
            pl.BlockSpec((tq, D_MODEL), lambda i: (i, 0)),
            pl.BlockSpec((1, D_MODEL), lambda i: (0, 0)),
            pl.BlockSpec(memory_space=pl.ANY),
        ],
        out_specs=pl.BlockSpec((tq, D_MODEL), lambda i: (i, 0)),
        out_shape=jax.ShapeDtypeStruct((s, D_MODEL), F32),
        scratch_shapes=[pltpu.VMEM((2, TOP_K * tq, SLAB_PITCH, LANES), F32), pltpu.SemaphoreType.DMA((2,))],
        compiler_params=_cparams(("arbitrary",)),
        name="combine_norm",
    )(pos3, pos3, gates, h1, norm_w, yb)


def _routing_metadata(top_idx, tm):
    s = top_idx.shape[0]
    n_assign = s * TOP_K
    nblk = -(-(n_assign + N_EXPERTS * (tm - 1)) // tm)
    e_flat = top_idx.reshape(n_assign)
    order = jnp.argsort(e_flat).astype(jnp.int32)
    rank = jnp.argsort(order).astype(jnp.int32)
    experts = jnp.arange(N_EXPERTS, dtype=jnp.int32)
    counts = jnp.sum((e_flat[:, None] == experts[None, :]).astype(jnp.int32), axis=0)
    padded = ((counts + tm - 1) // tm) * tm
    start_sorted = jnp.cumsum(counts) - counts
    start_pad = jnp.cumsum(padded) - padded
    pos = start_pad[e_flat] + rank - start_sorted[e_flat]
    n_used = jnp.sum(padded) // tm
    blocks = jnp.arange(nblk, dtype=jnp.int32)
    block_exp = jnp.sum((start_pad[None, :] <= (blocks * tm)[:, None]).astype(jnp.int32), axis=1) - 1
    block_exp = block_exp[jnp.minimum(blocks, n_used - 1)]
    block_valid = jnp.clip(counts[block_exp] - (blocks * tm - start_pad[block_exp]), 0, tm)
    block_valid = jnp.where(blocks < n_used, block_valid, 0)
    rows = jnp.arange(nblk * tm, dtype=jnp.int32)
    row_exp = jnp.broadcast_to(block_exp[:, None], (nblk, tm)).reshape(nblk * tm)
    off = rows - start_pad[row_exp]
    is_row = (off < counts[row_exp]) & (rows < n_used * tm)
    src = jnp.clip(start_sorted[row_exp] + off, 0, n_assign - 1)
    row_tok = jnp.where(is_row, order[src] // TOP_K, 0)
    as_i32 = lambda t: t.astype(jnp.int32)
    return (as_i32(n_used).reshape(1), as_i32(block_valid), as_i32(block_exp),
            as_i32(row_tok).reshape(nblk, 1, tm), as_i32(pos).reshape(s, TOP_K))


def _pick(n, pref):
    t = pref
    while n % t:
        t //= 2
    return t


def kernel(x, meta_tokens, attn_norm_w, w_in, b_in, attn_sinks, ret_norm_w, w_out, b_out, ffn_norm_w, router_w,
           router_b, w_gu, b_gu, w_down, b_down, final_norm_w):
    assert x.shape[0] == 1 and attn_norm_w.shape[0] == 1
    xs = x[0]
    s = xs.shape[0]
    assert s % WINDOW == 0
    tm = _pick(s, 512)

    w_in_bf = _permute_in_columns(w_in[0]).astype(BF16)
    b_in2 = _permute_in_columns(b_in[0])[None, :]
    nw = attn_norm_w[0][None, :]
    z = _inproj(xs, nw, w_in_bf, b_in2, _rope_tables(N_META, s), tm)
    zm = _inproj(meta_tokens, nw, w_in_bf, b_in2, _rope_tables(0, N_META), N_META)
    zm = jnp.pad(zm, ((0, WINDOW - N_META), (0, 0)))

    att = _attention(z, zm, attn_sinks[0])
    ret = _retention(z, zm, ret_norm_w[0][None, :])
    h1, h1_slabs = _outproj(att, ret, w_out[0].astype(BF16), b_out[0][None, :], xs, tm, 1024)

    ffn_w = ffn_norm_w[0][None, :]
    idx_t, gate_t = _router(h1, ffn_w, router_w[0].T, router_b[0][:, None], tm)
    gates = gate_t.T
    n_used, block_valid, block_exp, row_tok3, pos = _routing_metadata(idx_t.T, EXPERT_TM)

    xb = _gather_norm(n_used, row_tok3, h1_slabs, ffn_w, EXPERT_TM)
    act = _gateup(n_used, block_valid, block_exp, xb, w_gu[0], b_gu[0][:, None, :], EXPERT_TM)
    yb = _down(n_used, block_valid, block_exp, act, w_down[0], b_down[0][:, None, :], EXPERT_TM)

    tq = WINDOW
    pos3 = pos.reshape(s // tq, tq, TOP_K).transpose(0, 2, 1).reshape(s // tq, 1, TOP_K * tq)
    out = _combine(pos3, gates, h1, final_norm_w[None, :], yb, tq)
    return out[None]
```

```python
import functools

import jax
import jax.numpy as jnp
import numpy as np
from jax import lax
from jax.experimental import pallas as pl
from jax.experimental.pallas import tpu as pltpu

F32 = jnp.float32
BF16 = jnp.bfloat16

D_MODEL = 4096
N_META = 16
ATT_HEADS = 32
ATT_KV_HEADS = 4
ATT_HEAD_DIM = 64
ATT_GROUP = ATT_HEADS // ATT_KV_HEADS
WINDOW = 128
ROPE_THETA = 500000.0
ROPE_DIM = ATT_HEAD_DIM // 4
RET_HEADS = 8
RET_KEY_DIM = 128
RET_VALUE_DIM = 256
RET_CHUNK = 128
RET_ROPE_THETA = 10000.0
ATT_WIDTH = ATT_HEADS * ATT_HEAD_DIM
KV_WIDTH = ATT_KV_HEADS * ATT_HEAD_DIM
RET_QK_WIDTH = RET_HEADS * RET_KEY_DIM
RET_WIDTH = RET_HEADS * RET_VALUE_DIM
IN_WIDTH = ATT_WIDTH + 2 * KV_WIDTH + 2 * RET_QK_WIDTH + 2 * RET_WIDTH
OFF_AQ = 0
OFF_RQ = OFF_AQ + ATT_WIDTH
OFF_RK = OFF_RQ + RET_QK_WIDTH
OFF_RV = OFF_RK + RET_QK_WIDTH
OFF_RG = OFF_RV + RET_WIDTH
OFF_AK = OFF_RG + RET_WIDTH
OFF_AV = OFF_AK + KV_WIDTH
N_EXPERTS = 32
TOP_K = 4
D_EXPERT = D_MODEL // 2
SWIGLU_ALPHA = 1.702
SWIGLU_LIMIT = 7.0
NORM_EPS = 1e-5
GN_EPS = 1e-6

LANES = 128
VMEM_LIMIT = 56 * 1024 * 1024

IN_TN = 512
EXPERT_TM = 512
EXPERT_HALF = EXPERT_TM // 2
GU_TN = 512
DOWN_TN = 1024
KEYS_PAD = 3 * WINDOW
NEG_BIG = -1e30
SLAB_ROWS = D_MODEL // LANES
SLAB_PITCH = 40
GATHER_GROUP = 64
COMBINE_GROUP = 32


def _cparams(sem):
    return pltpu.CompilerParams(dimension_semantics=sem, vmem_limit_bytes=VMEM_LIMIT)


def _src_tile(j):
    assert 2 * KV_WIDTH == IN_TN and ATT_WIDTH % IN_TN == 0
    n_aq = ATT_WIDTH // IN_TN
    n_tiles = IN_WIDTH // IN_TN
    return jnp.where(j < n_aq, j, jnp.where(j < n_tiles - 1, j + 1, n_aq))


def _chunk_classes():
    cls = []
    for c in range(IN_WIDTH // LANES):
        off = c * LANES
        if off < OFF_RQ or OFF_AK <= off < OFF_AV:
            cls.append("a")
        elif OFF_RQ <= off < OFF_RV:
            cls.append("r")
        else:
            cls.append("n")
    return cls


def _inproj_kernel(x_ref, nw_ref, w_ref, b_ref, aa_ref, ab_ref, ac_ref, ra_ref, rb_ref, o_ref, hn_ref):
    j = pl.program_id(1)

    @pl.when(j == 0)
    def _():
        x = x_ref[...]
        ms = jnp.mean(x * x, axis=-1, keepdims=True)
        hn_ref[...] = (x * lax.rsqrt(ms + NORM_EPS) * nw_ref[...]).astype(BF16)

    y = jnp.dot(hn_ref[...], w_ref[...], preferred_element_type=F32) + b_ref[...]

    cpt = IN_TN // LANES
    classes = _chunk_classes()
    n_tiles = IN_WIDTH // IN_TN
    patterns = {}
    for t in range(n_tiles):
        patterns.setdefault(tuple(classes[t * cpt:(t + 1) * cpt]), []).append(t)

    def emit(pattern):
        for c, kind in enumerate(pattern):
            yc = y[:, c * LANES:(c + 1) * LANES]
            if kind == "a":
                yc = (yc * aa_ref[...] + pltpu.roll(yc, LANES - ROPE_DIM // 2, 1) * ab_ref[...]
                      + pltpu.roll(yc, ROPE_DIM // 2, 1) * ac_ref[...])
            elif kind == "r":
                yc = yc * ra_ref[...] + pltpu.roll(yc, RET_KEY_DIM // 2, 1) * rb_ref[...]
            o_ref[:, c * LANES:(c + 1) * LANES] = yc.astype(o_ref.dtype)

    for pattern, tiles in patterns.items():
        cond = functools.reduce(jnp.logical_or, [j == t for t in tiles])
        pl.when(cond)(functools.partial(emit, pattern))


def _inproj(x, norm_w, w_bf, b, tabs, tm):
    m = x.shape[0]
    aa, ab, ac, ra, rb = tabs
    row = lambda i, j: (i, 0)
    tab_spec = pl.BlockSpec((tm, LANES), row)
    return pl.pallas_call(
        _inproj_kernel,
        grid=(m // tm, IN_WIDTH // IN_TN),
        in_specs=[
            pl.BlockSpec((tm, D_MODEL), row),
            pl.BlockSpec((1, D_MODEL), lambda i, j: (0, 0)),
            pl.BlockSpec((D_MODEL, IN_TN), lambda i, j: (0, _src_tile(j))),
            pl.BlockSpec((1, IN_TN), lambda i, j: (0, _src_tile(j))),
            tab_spec, tab_spec, tab_spec, tab_spec, tab_spec,
        ],
        out_specs=pl.BlockSpec((tm, IN_TN), lambda i, j: (i, j)),
        out_shape=jax.ShapeDtypeStruct((m, IN_WIDTH), BF16),
        scratch_shapes=[pltpu.VMEM((tm, D_MODEL), BF16)],
        compiler_params=_cparams(("parallel", "arbitrary")),
        name="inproj",
    )(x, norm_w, w_bf, b, aa, ab, ac, ra, rb)


def _rope_tables(first_pos, n_pos):
    pos = (first_pos + jnp.arange(n_pos, dtype=jnp.int32)).astype(F32)[:, None]
    lane = np.arange(LANES)
    half = ROPE_DIM // 2
    inv = ROPE_THETA ** (-jnp.arange(half, dtype=F32) / half)
    ang = pos * inv[None, :]
    cos, sin = jnp.cos(ang), jnp.sin(ang)
    c = lane % ATT_HEAD_DIM
    f = c % half
    is_lo = jnp.asarray(c < half)[None, :]
    is_hi = jnp.asarray((c >= half) & (c < ROPE_DIM))[None, :]
    cos_l, sin_l = cos[:, f], sin[:, f]
    aa = jnp.where(is_lo | is_hi, cos_l, 1.0)
    ab = jnp.where(is_lo, -sin_l, 0.0)
    ac = jnp.where(is_hi, sin_l, 0.0)
    rhalf = RET_KEY_DIM // 2
    rinv = RET_ROPE_THETA ** (-jnp.arange(rhalf, dtype=F32) / rhalf)
    rang = pos * rinv[None, :]
    rcos, rsin = jnp.cos(rang), jnp.sin(rang)
    rf = lane % rhalf
    ra = rcos[:, rf]
    rb = jnp.where(jnp.asarray(lane < rhalf)[None, :], -rsin[:, rf], rsin[:, rf])
    return aa, ab, ac, ra, rb


def _attn_kernel(sink_ref, q_ref, ko_ref, vo_ref, kp_ref, vp_ref, km_ref, vm_ref, o_ref):
    n = pl.program_id(0)
    w = WINDOW
    qi = lax.broadcasted_iota(jnp.int32, (w, KEYS_PAD), 0)
    ji = lax.broadcasted_iota(jnp.int32, (w, KEYS_PAD), 1)
    vis_prev = (ji < w) & (ji > qi) & (n > 0)
    vis_own = (ji >= w) & (ji < 2 * w) & (ji - w <= qi)
    vis_meta = (ji >= 2 * w) & (ji < 2 * w + N_META)
    mask = vis_prev | vis_own | vis_meta
    lane = lax.broadcasted_iota(jnp.int32, (KEYS_PAD, LANES), 1)
    lo_half = lane < ATT_HEAD_DIM
    olane = lax.broadcasted_iota(jnp.int32, (w, LANES), 1) < ATT_HEAD_DIM
    scale = ATT_HEAD_DIM ** -0.5

    def spread(prev_ref, own_ref, meta_ref, g):
        c0 = (g // 2) * LANES
        chunk = jnp.concatenate([prev_ref[:, c0:c0 + LANES], own_ref[:, c0:c0 + LANES],
                                 meta_ref[:, c0:c0 + LANES]], axis=0).astype(F32)
        swapped = pltpu.roll(chunk, ATT_HEAD_DIM, 1)
        in_lo, in_hi = (chunk, swapped) if g % 2 == 0 else (swapped, chunk)
        even = jnp.where(lo_half, in_lo, 0.0)
        odd = jnp.where(lo_half, 0.0, in_hi)
        return jnp.concatenate([even, odd], axis=0).astype(BF16)

    pairs = ATT_GROUP // 2
    for g in range(ATT_KV_HEADS):
        kz = spread(kp_ref, ko_ref, km_ref, g)
        vz = spread(vp_ref, vo_ref, vm_ref, g)
        c0 = g * pairs
        q4 = jnp.concatenate([q_ref[:, (c0 + r) * LANES:(c0 + r + 1) * LANES] for r in range(pairs)], axis=0)
        s4 = lax.dot_general(q4, kz, (((1,), (1,)), ((), ())), preferred_element_type=F32) * scale
        ps, dens = [], []
        for r in range(pairs):
            prow, drow = [], []
            for half in range(2):
                sink = sink_ref[2 * (c0 + r) + half]
                s = jnp.where(mask, s4[r * w:(r + 1) * w, half * KEYS_PAD:(half + 1) * KEYS_PAD], NEG_BIG)
                m = jnp.maximum(jnp.max(s, axis=-1, keepdims=True), sink)
                p = jnp.exp(s - m)
                drow.append(jnp.sum(p, axis=-1, keepdims=True) + jnp.exp(sink - m))
                prow.append(p.astype(BF16))
            ps.append(jnp.concatenate(prow, axis=1))
            dens.append(jnp.where(olane, drow[0], drow[1]))
        o4 = jnp.dot(jnp.concatenate(ps, axis=0), vz, preferred_element_type=F32)
        for r in range(pairs):
            o = o4[r * w:(r + 1) * w] / dens[r]
            o_ref[:, (c0 + r) * LANES:(c0 + r + 1) * LANES] = o.astype(o_ref.dtype)


def _attention(z, zm, sinks):
    s = z.shape[0]
    nb = s // WINDOW
    kcol, vcol = OFF_AK // KV_WIDTH, OFF_AV // KV_WIDTH
    prev = lambda n: jnp.maximum(n - 1, 0)
    return pl.pallas_call(
        _attn_kernel,
        grid=(nb,),
        in_specs=[
            pl.BlockSpec(memory_space=pltpu.SMEM),
            pl.BlockSpec((WINDOW, ATT_WIDTH), lambda n: (n, 0)),
            pl.BlockSpec((WINDOW, KV_WIDTH), lambda n: (n, kcol)),
            pl.BlockSpec((WINDOW, KV_WIDTH), lambda n: (n, vcol)),
            pl.BlockSpec((WINDOW, KV_WIDTH), lambda n: (prev(n), kcol)),
            pl.BlockSpec((WINDOW, KV_WIDTH), lambda n: (prev(n), vcol)),
            pl.BlockSpec((WINDOW, KV_WIDTH), lambda n: (0, kcol)),
            pl.BlockSpec((WINDOW, KV_WIDTH), lambda n: (0, vcol)),
        ],
        out_specs=pl.BlockSpec((WINDOW, ATT_WIDTH), lambda n: (n, 0)),
        out_shape=jax.ShapeDtypeStruct((s, ATT_WIDTH), BF16),
        compiler_params=_cparams(("parallel",)),
        name="swa_attention",
    )(sinks, z, z, z, z, z, zm, zm)


def _ret_kernel(gc_ref, q_ref, k_ref, v_ref, g_ref, km_ref, vm_ref, dm_ref, kd_ref, qd_ref, wm_ref, nw_ref,
                o_ref, u_ref):
    c = pl.program_id(0)
    dk, dv = RET_KEY_DIM, RET_VALUE_DIM

    @pl.when(c == 0)
    def _():
        for hh in range(RET_HEADS):
            kmw = (km_ref[:, hh * dk:(hh + 1) * dk].astype(F32) * wm_ref[hh]).T.astype(BF16)
            u_ref[hh] = jnp.dot(kmw, vm_ref[:, hh * dv:(hh + 1) * dv], preferred_element_type=F32)

    for hh in range(RET_HEADS):
        q = q_ref[:, hh * dk:(hh + 1) * dk]
        k = k_ref[:, hh * dk:(hh + 1) * dk]
        v = v_ref[:, hh * dv:(hh + 1) * dv]
        u = u_ref[hh]
        inner = lax.dot_general(q, k, (((1,), (1,)), ((), ())), preferred_element_type=F32) * dm_ref[hh]
        intra = jnp.dot(inner.astype(BF16), v, preferred_element_type=F32)
        cross = jnp.dot(q, u.astype(BF16), preferred_element_type=F32) * qd_ref[hh]
        y = intra + cross
        mu = jnp.mean(y, axis=-1, keepdims=True)
        yc = y - mu
        var = jnp.mean(yc * yc, axis=-1, keepdims=True)
        yn = yc * lax.rsqrt(var + GN_EPS)
        g = g_ref[:, hh * dv:(hh + 1) * dv].astype(F32)
        silu = g * (1.0 / (1.0 + jnp.exp(-g)))
        o_ref[:, hh * dv:(hh + 1) * dv] = (silu * yn * nw_ref[:, hh * dv:(hh + 1) * dv]).astype(o_ref.dtype)
        kdt = (k.astype(F32) * kd_ref[hh]).T.astype(BF16)
        u_ref[hh] = u * gc_ref[hh] + jnp.dot(kdt, v, preferred_element_type=F32)


def _ret_tables():
    scale = RET_KEY_DIM ** -0.5
    log_g = jnp.log1p(-(2.0 ** (-5.0 - jnp.arange(RET_HEADS, dtype=F32))))
    i = jnp.arange(RET_CHUNK, dtype=F32)
    diff = i[:, None] - i[None, :]
    dm = jnp.where(diff >= 0, jnp.exp(jnp.maximum(diff, 0.0)[None] * log_g[:, None, None]), 0.0) * scale
    qd = (jnp.exp((i + 1.0)[None, :] * log_g[:, None]) * scale)[:, :, None]
    kd = jnp.exp((RET_CHUNK - 1.0 - i)[None, :] * log_g[:, None])[:, :, None]
    jm = jnp.arange(RET_CHUNK, dtype=F32)
    wm = jnp.where(jm[None, :] < N_META, jnp.exp((N_META - 1 - jm)[None, :] * log_g[:, None]), 0.0)[:, :, None]
    gc = jnp.exp(RET_CHUNK * log_g)
    return gc, dm, kd, qd, wm


def _retention(z, zm, ret_norm_w):
    s = z.shape[0]
    nc = s // RET_CHUNK
    gc, dm, kd, qd, wm = _ret_tables()
    qc, kc = OFF_RQ // RET_QK_WIDTH, OFF_RK // RET_QK_WIDTH
    vc, gcol = OFF_RV // RET_WIDTH, OFF_RG // RET_WIDTH
    full3 = lambda c: (0, 0, 0)
    vec_spec = pl.BlockSpec((RET_HEADS, RET_CHUNK, 1), full3)
    return pl.pallas_call(
        _ret_kernel,
        grid=(nc,),
        in_specs=[
            pl.BlockSpec(memory_space=pltpu.SMEM),
            pl.BlockSpec((RET_CHUNK, RET_QK_WIDTH), lambda c: (c, qc)),
            pl.BlockSpec((RET_CHUNK, RET_QK_WIDTH), lambda c: (c, kc)),
            pl.BlockSpec((RET_CHUNK, RET_WIDTH), lambda c: (c, vc)),
            pl.BlockSpec((RET_CHUNK, RET_WIDTH), lambda c: (c, gcol)),
            pl.BlockSpec((RET_CHUNK, RET_QK_WIDTH), lambda c: (0, kc)),
            pl.BlockSpec((RET_CHUNK, RET_WIDTH), lambda c: (0, vc)),
            pl.BlockSpec((RET_HEADS, RET_CHUNK, RET_CHUNK), full3),
            vec_spec, vec_spec, vec_spec,
            pl.BlockSpec((1, RET_WIDTH), lambda c: (0, 0)),
        ],
        out_specs=pl.BlockSpec((RET_CHUNK, RET_WIDTH), lambda c: (c, 0)),
        out_shape=jax.ShapeDtypeStruct((s, RET_WIDTH), BF16),
        scratch_shapes=[pltpu.VMEM((RET_HEADS, RET_KEY_DIM, RET_VALUE_DIM), F32)],
        compiler_params=_cparams(("arbitrary",)),
        name="retention",
    )(gc, z, z, z, z, zm, zm, dm, kd, qd, wm, ret_norm_w)


def _store_slabs(slab_ref, rows, value):
    n = slab_ref.shape[1]
    flat = slab_ref.reshape(slab_ref.shape[0] * n, LANES)
    for k in range(n):
        flat[pl.ds(k, rows, stride=n), :] = value[:, k * LANES:(k + 1) * LANES]


def _outproj_kernel(a_ref, r_ref, wt_ref, wb_ref, b_ref, x_ref, o_ref, slab_ref):
    acc = jnp.dot(a_ref[...], wt_ref[...], preferred_element_type=F32)
    acc += jnp.dot(r_ref[...], wb_ref[...], preferred_element_type=F32)
    h = x_ref[...] + acc + b_ref[...]
    o_ref[...] = h
    _store_slabs(slab_ref, h.shape[0], h)


def _outproj(att, ret, w_bf, b, x, tm, tn):
    s = x.shape[0]
    return pl.pallas_call(
        _outproj_kernel,
        grid=(s // tm, D_MODEL // tn),
        in_specs=[
            pl.BlockSpec((tm, ATT_WIDTH), lambda i, j: (i, 0)),
            pl.BlockSpec((tm, RET_WIDTH), lambda i, j: (i, 0)),
            pl.BlockSpec((ATT_WIDTH, tn), lambda i, j: (0, j)),
            pl.BlockSpec((RET_WIDTH, tn), lambda i, j: (1, j)),
            pl.BlockSpec((1, tn), lambda i, j: (0, j)),
            pl.BlockSpec((tm, tn), lambda i, j: (i, j)),
        ],
        out_specs=[pl.BlockSpec((tm, tn), lambda i, j: (i, j)),
                   pl.BlockSpec((tm, tn // LANES, LANES), lambda i, j: (i, j, 0))],
        out_shape=[jax.ShapeDtypeStruct((s, D_MODEL), F32),
                   jax.ShapeDtypeStruct((s, D_MODEL // LANES, LANES), F32)],
        compiler_params=_cparams(("parallel", "arbitrary")),
        name="outproj",
    )(att, ret, w_bf, w_bf, b, x)


def _router_kernel(h_ref, nw_ref, rw_ref, rb_ref, idx_ref, gate_ref, inv_ref):
    x = h_ref[...]
    ms = jnp.mean(x * x, axis=-1, keepdims=True)
    inv = lax.rsqrt(ms + NORM_EPS)
    inv_ref[...] = inv
    hn = x * inv * nw_ref[...]
    logits = lax.dot_general(rw_ref[...], hn, (((1,), (1,)), ((), ())), preferred_element_type=F32,
                             precision=lax.Precision.HIGHEST) + rb_ref[...]
    eid = lax.broadcasted_iota(jnp.int32, logits.shape, 0)
    vals, idxs = [], []
    for _ in range(TOP_K):
        m = jnp.max(logits, axis=0, keepdims=True)
        sel = jnp.min(jnp.where(logits == m, eid, N_EXPERTS), axis=0, keepdims=True)
        vals.append(m)
        idxs.append(sel)
        logits = jnp.where(eid == sel, -jnp.inf, logits)
    es = [jnp.exp(v - vals[0]) for v in vals]
    tot = functools.reduce(lambda a, b: a + b, es)
    idx_ref[...] = jnp.concatenate(idxs, axis=0)
    gate_ref[...] = jnp.concatenate([e / tot for e in es], axis=0)


def _router(h1, norm_w, rw_t, rb, tm):
    s = h1.shape[0]
    return pl.pallas_call(
        _router_kernel,
        grid=(s // tm,),
        in_specs=[
            pl.BlockSpec((tm, D_MODEL), lambda i: (i, 0)),
            pl.BlockSpec((1, D_MODEL), lambda i: (0, 0)),
            pl.BlockSpec((N_EXPERTS, D_MODEL), lambda i: (0, 0)),
            pl.BlockSpec((N_EXPERTS, 1), lambda i: (0, 0)),
        ],
        out_specs=[pl.BlockSpec((TOP_K, tm), lambda i: (0, i)), pl.BlockSpec((TOP_K, tm), lambda i: (0, i)),
                   pl.BlockSpec((tm, 1), lambda i: (i, 0))],
        out_shape=[jax.ShapeDtypeStruct((TOP_K, s), jnp.int32), jax.ShapeDtypeStruct((TOP_K, s), F32),
                   jax.ShapeDtypeStruct((s, 1), F32)],
        compiler_params=_cparams(("parallel",)),
        name="router",
    )(h1, norm_w, rw_t, rb)


def _slab_copy(src_hbm, idx_ref, buf_ref, sems, slot, r):
    return pltpu.make_async_copy(src_hbm.at[idx_ref[0, r]], buf_ref.at[slot, r, pl.ds(0, SLAB_ROWS), :],
                                 sems.at[slot])


def _slot_wait(src_hbm, buf_ref, sems, slot):
    n = buf_ref.shape[1]
    pltpu.make_async_copy(src_hbm.at[pl.ds(0, n)], buf_ref.at[slot, :, pl.ds(0, SLAB_ROWS), :],
                          sems.at[slot]).wait()


def _slab_chunk(buf_ref, slot, r0, rows, k):
    n_slots, slot_slabs = buf_ref.shape[:2]
    flat = buf_ref.reshape(n_slots * slot_slabs * SLAB_PITCH, LANES)
    base = (slot * slot_slabs + r0) * SLAB_PITCH + k
    return flat[pl.ds(base, rows, stride=SLAB_PITCH), :]


def _gather_norm_kernel(nused_ref, tok_ref, tokn_ref, h_hbm, inv_ref, nw_ref, o_ref, buf_ref, sems):
    b = pl.program_id(0)
    nu = nused_ref[0]
    tm = o_ref.shape[0]
    slot = b % 2
    grp = GATHER_GROUP
    n_groups = tm // grp

    def issue(idx_ref, s, r0):
        for r in range(grp):
            _slab_copy(h_hbm, idx_ref, buf_ref, sems, s, r0 + r).start()

    @pl.when(b == 0)
    def _():
        def first(i, carry):
            issue(tok_ref, 0, i * grp)
            return carry
        lax.fori_loop(0, n_groups, first, 0)

    @pl.when(b < nu)
    def _():
        _slot_wait(h_hbm, buf_ref, sems, slot)
        has_next = b + 1 < nu

        def body(i, carry):
            r0 = pl.multiple_of(i * grp, grp)

            @pl.when(has_next)
            def _():
                issue(tokn_ref, 1 - slot, r0)

            sc = jnp.broadcast_to(inv_ref[pl.ds(r0, grp), :], (grp, LANES))
            for k in range(SLAB_ROWS):
                x = _slab_chunk(buf_ref, slot, r0, grp, k)
                cols = slice(k * LANES, (k + 1) * LANES)
                o_ref[pl.ds(r0, grp), cols] = (x * sc * nw_ref[:, cols]).astype(o_ref.dtype)
            return carry
        lax.fori_loop(0, n_groups, body, 0)

    @pl.when(b >= nu)
    def _():
        o_ref[...] = jnp.zeros_like(o_ref)


def _gather_norm(n_used, row_tok3, h1_slabs, inv_rows, norm_w, tm):
    nblk = row_tok3.shape[0]
    cur = lambda b, nu: (jnp.minimum(b, nu[0] - 1), 0, 0)
    nxt = lambda b, nu: (jnp.minimum(b + 1, nu[0] - 1), 0, 0)
    return pl.pallas_call(
        _gather_norm_kernel,
        grid_spec=pltpu.PrefetchScalarGridSpec(
            num_scalar_prefetch=1,
            grid=(nblk,),
            in_specs=[
                pl.BlockSpec((None, 1, tm), cur, memory_space=pltpu.SMEM),
                pl.BlockSpec((None, 1, tm), nxt, memory_space=pltpu.SMEM),
                pl.BlockSpec(memory_space=pl.ANY),
                pl.BlockSpec((tm, 1), lambda b, nu: (jnp.minimum(b, nu[0] - 1), 0)),
                pl.BlockSpec((1, D_MODEL), lambda b, nu: (0, 0)),
            ],
            out_specs=pl.BlockSpec((tm, D_MODEL), lambda b, nu: (b, 0)),
            scratch_shapes=[pltpu.VMEM((2, tm, SLAB_PITCH, LANES), F32), pltpu.SemaphoreType.DMA((2,))],
        ),
        out_shape=jax.ShapeDtypeStruct((nblk * tm, D_MODEL), BF16),
        compiler_params=_cparams(("arbitrary",)),
        name="gather_norm",
    )(n_used, row_tok3, row_tok3, h1_slabs, inv_rows, norm_w)


def _swiglu(x, wg, wl, bg, bl):
    glu = jnp.dot(x, wg, preferred_element_type=F32) + bg
    lin = jnp.dot(x, wl, preferred_element_type=F32) + bl
    glu = jnp.minimum(glu, SWIGLU_LIMIT)
    lin = jnp.clip(lin, -SWIGLU_LIMIT, SWIGLU_LIMIT)
    return glu * (1.0 / (1.0 + jnp.exp(-SWIGLU_ALPHA * glu))) * (lin + 1.0)


def _gateup_kernel(bv_ref, be_ref, nu_ref, x_ref, wg_ref, wl_ref, bg_ref, bl_ref, o_ref, wgb_ref, wlb_ref):
    b = pl.program_id(1)
    valid = bv_ref[b]
    new_expert = (b == 0) | (be_ref[b] != be_ref[jnp.maximum(b - 1, 0)])
    hm = EXPERT_HALF

    @pl.when((valid > 0) & new_expert)
    def _():
        wgb_ref[...] = wg_ref[...].astype(BF16)
        wlb_ref[...] = wl_ref[...].astype(BF16)

    @pl.when(valid > hm)
    def _():
        act = _swiglu(x_ref[...], wgb_ref[...], wlb_ref[...], bg_ref[...], bl_ref[...])
        o_ref[...] = act.astype(o_ref.dtype)

    @pl.when((valid > 0) & (valid <= hm))
    def _():
        act = _swiglu(x_ref[:hm, :], wgb_ref[...], wlb_ref[...], bg_ref[...], bl_ref[...])
        o_ref[:hm, :] = act.astype(o_ref.dtype)
        o_ref[hm:, :] = jnp.zeros((o_ref.shape[0] - hm, o_ref.shape[1]), o_ref.dtype)

    @pl.when(valid == 0)
    def _():
        o_ref[...] = jnp.zeros_like(o_ref)


def _gateup(n_used, block_valid, block_exp, xb, w_gu, b_gu3, tm):
    nblk = xb.shape[0] // tm
    nj = D_EXPERT // GU_TN
    last = lambda b, nu: jnp.minimum(b, nu[0] - 1)
    return pl.pallas_call(
        _gateup_kernel,
        grid_spec=pltpu.PrefetchScalarGridSpec(
            num_scalar_prefetch=3,
            grid=(nj, nblk),
            in_specs=[
                pl.BlockSpec((tm, D_MODEL), lambda j, b, bv, be, nu: (last(b, nu), 0)),
                pl.BlockSpec((None, D_MODEL, GU_TN), lambda j, b, bv, be, nu: (be[b], 0, j)),
                pl.BlockSpec((None, D_MODEL, GU_TN), lambda j, b, bv, be, nu: (be[b], 0, j + nj)),
                pl.BlockSpec((None, 1, GU_TN), lambda j, b, bv, be, nu: (be[b], 0, j)),
                pl.BlockSpec((None, 1, GU_TN), lambda j, b, bv, be, nu: (be[b], 0, j + nj)),
            ],
            out_specs=pl.BlockSpec((tm, GU_TN), lambda j, b, bv, be, nu: (b, j)),
            scratch_shapes=[pltpu.VMEM((D_MODEL, GU_TN), BF16), pltpu.VMEM((D_MODEL, GU_TN), BF16)],
        ),
        out_shape=jax.ShapeDtypeStruct((nblk * tm, D_EXPERT), BF16),
        compiler_params=_cparams(("arbitrary", "arbitrary")),
        name="expert_gateup",
    )(block_valid, block_exp, n_used, xb, w_gu, w_gu, b_gu3, b_gu3)


def _down_kernel(bv_ref, be_ref, nu_ref, a_ref, w_ref, bias_ref, o_ref, wb_ref):
    b = pl.program_id(1)
    valid = bv_ref[b]
    new_expert = (b == 0) | (be_ref[b] != be_ref[jnp.maximum(b - 1, 0)])
    hm = EXPERT_HALF

    @pl.when((valid > 0) & new_expert)
    def _():
        wb_ref[...] = w_ref[...].astype(BF16)

    @pl.when(valid > hm)
    def _():
        res = jnp.dot(a_ref[...], wb_ref[...], preferred_element_type=F32) + bias_ref[...]
        _store_slabs(o_ref, o_ref.shape[0], res)

    @pl.when((valid > 0) & (valid <= hm))
    def _():
        res = jnp.dot(a_ref[:hm, :], wb_ref[...], preferred_element_type=F32) + bias_ref[...]
        _store_slabs(o_ref, hm, res)
        o_ref[hm:] = jnp.zeros((o_ref.shape[0] - hm,) + o_ref.shape[1:], o_ref.dtype)

    @pl.when(valid == 0)
    def _():
        o_ref[...] = jnp.zeros_like(o_ref)


def _down(n_used, block_valid, block_exp, act, w_down, b_down3, tm):
    nblk = act.shape[0] // tm
    nj = D_MODEL // DOWN_TN
    last = lambda b, nu: jnp.minimum(b, nu[0] - 1)
    return pl.pallas_call(
        _down_kernel,
        grid_spec=pltpu.PrefetchScalarGridSpec(
            num_scalar_prefetch=3,
            grid=(nj, nblk),
            in_specs=[
                pl.BlockSpec((tm, D_EXPERT), lambda j, b, bv, be, nu: (last(b, nu), 0)),
                pl.BlockSpec((None, D_EXPERT, DOWN_TN), lambda j, b, bv, be, nu: (be[b], 0, j)),
                pl.BlockSpec((None, 1, DOWN_TN), lambda j, b, bv, be, nu: (be[b], 0, j)),
            ],
            out_specs=pl.BlockSpec((tm, DOWN_TN // LANES, LANES), lambda j, b, bv, be, nu: (b, j, 0)),
            scratch_shapes=[pltpu.VMEM((D_EXPERT, DOWN_TN), BF16)],
        ),
        out_shape=jax.ShapeDtypeStruct((nblk * tm, SLAB_ROWS, LANES), F32),
        compiler_params=_cparams(("arbitrary", "arbitrary")),
        name="expert_down",
    )(block_valid, block_exp, n_used, act, w_down, b_down3)


def _combine_kernel(pos_ref, posn_ref, g_ref, h_ref, nw_ref, y_hbm, o_ref, buf_ref, sems):
    i = pl.program_id(0)
    n_tiles = pl.num_programs(0)
    tq = h_ref.shape[0]
    slot = i % 2
    grp = COMBINE_GROUP
    n_groups = tq // grp

    def issue(idx_ref, s, r0):
        for k in range(TOP_K):
            for r in range(grp):
                _slab_copy(y_hbm, idx_ref, buf_ref, sems, s, k * tq + r0 + r).start()

    @pl.when(i == 0)
    def _():
        def first(t, carry):
            issue(pos_ref, 0, t * grp)
            return carry
        lax.fori_loop(0, n_groups, first, 0)

    _slot_wait(y_hbm, buf_ref, sems, slot)
    has_next = i + 1 < n_tiles

    def body(t, carry):
        r0 = pl.multiple_of(t * grp, grp)

        @pl.when(has_next)
        def _():
            issue(posn_ref, 1 - slot, r0)

        rows = pl.ds(r0, grp)
        g = g_ref[rows, :]
        gk = [jnp.broadcast_to(g[:, k:k + 1], (grp, LANES)) for k in range(TOP_K)]
        ssq = jnp.zeros((grp, LANES), F32)
        for c in range(SLAB_ROWS):
            cols = slice(c * LANES, (c + 1) * LANES)
            acc = h_ref[rows, cols]
            for k in range(TOP_K):
                acc = acc + gk[k] * _slab_chunk(buf_ref, slot, k * tq + r0, grp, c)
            o_ref[rows, cols] = acc
            ssq = ssq + acc * acc
        sc = lax.rsqrt(jnp.sum(ssq, axis=-1, keepdims=True) * (1.0 / D_MODEL) + NORM_EPS)
        o_ref[rows, :] = o_ref[rows, :] * sc * nw_ref[...]
        return carry
    lax.fori_loop(0, n_groups, body, 0)


def _combine(pos3, gates, h1, norm_w, yb, tq):
    s = h1.shape[0]
    nt = s // tq
    return pl.pallas_call(
        _combine_kernel,
        grid=(nt,),
        in_specs=[
            pl.BlockSpec((None, 1, TOP_K * tq), lambda i: (i, 0, 0), memory_space=pltpu.SMEM),
            pl.BlockSpec((None, 1, TOP_K * tq), lambda i: (jnp.minimum(i + 1, nt - 1), 0, 0),
                         memory_space=pltpu.SMEM),
            pl.BlockSpec((tq, TOP_K), lambda i: (i, 0)),
            pl.BlockSpec((tq, D_MODEL), lambda i: (i, 0)),
            pl.BlockSpec((1, D_MODEL), lambda i: (0, 0)),
            pl.BlockSpec(memory_space=pl.ANY),
        ],
        out_specs=pl.BlockSpec((tq, D_MODEL), lambda i: (i, 0)),
        out_shape=jax.ShapeDtypeStruct((s, D_MODEL), F32),
        scratch_shapes=[pltpu.VMEM((2, TOP_K * tq, SLAB_PITCH, LANES), F32), pltpu.SemaphoreType.DMA((2,))],
        compiler_params=_cparams(("arbitrary",)),
        name="combine_norm",
    )(pos3, pos3, gates, h1, norm_w, yb)


def _routing_metadata(top_idx, tm):
    s = top_idx.shape[0]
    n_assign = s * TOP_K
    nblk = -(-(n_assign + N_EXPERTS * (tm - 1)) // tm)
    e_flat = top_idx.reshape(n_assign)
    order = jnp.argsort(e_flat).astype(jnp.int32)
    rank = jnp.argsort(order).astype(jnp.int32)
    experts = jnp.arange(N_EXPERTS, dtype=jnp.int32)
    counts = jnp.sum((e_flat[:, None] == experts[None, :]).astype(jnp.int32), axis=0)
    padded = ((counts + tm - 1) // tm) * tm
    start_sorted = jnp.cumsum(counts) - counts
    start_pad = jnp.cumsum(padded) - padded
    pos = start_pad[e_flat] + rank - start_sorted[e_flat]
    n_used = jnp.sum(padded) // tm
    blocks = jnp.arange(nblk, dtype=jnp.int32)
    block_exp = jnp.sum((start_pad[None, :] <= (blocks * tm)[:, None]).astype(jnp.int32), axis=1) - 1
    block_exp = block_exp[jnp.minimum(blocks, n_used - 1)]
    block_valid = jnp.clip(counts[block_exp] - (blocks * tm - start_pad[block_exp]), 0, tm)
    block_valid = jnp.where(blocks < n_used, block_valid, 0)
    rows = jnp.arange(nblk * tm, dtype=jnp.int32)
    row_exp = jnp.broadcast_to(block_exp[:, None], (nblk, tm)).reshape(nblk * tm)
    off = rows - start_pad[row_exp]
    is_row = (off < counts[row_exp]) & (rows < n_used * tm)
    src = jnp.clip(start_sorted[row_exp] + off, 0, n_assign - 1)
    row_tok = jnp.where(is_row, order[src] // TOP_K, 0)
    as_i32 = lambda t: t.astype(jnp.int32)
    return (as_i32(n_used).reshape(1), as_i32(block_valid), as_i32(block_exp),
            as_i32(row_tok).reshape(nblk, 1, tm), as_i32(pos).reshape(s, TOP_K))


def _pick(n, pref):
    t = pref
    while n % t:
        t //= 2
    return t


def kernel(x, meta_tokens, attn_norm_w, w_in, b_in, attn_sinks, ret_norm_w, w_out, b_out, ffn_norm_w, router_w,
           router_b, w_gu, b_gu, w_down, b_down, final_norm_w):
    assert x.shape[0] == 1 and attn_norm_w.shape[0] == 1
    xs = x[0]
    s = xs.shape[0]
    assert s % WINDOW == 0
    tm = _pick(s, 512)

    w_in_bf = w_in[0].astype(BF16)
    b_in2 = b_in[0][None, :]
    nw = attn_norm_w[0][None, :]
    z = _inproj(xs, nw, w_in_bf, b_in2, _rope_tables(N_META, s), tm)
    zm = _inproj(meta_tokens, nw, w_in_bf, b_in2, _rope_tables(0, N_META), N_META)
    zm = jnp.pad(zm, ((0, WINDOW - N_META), (0, 0)))

    att = _attention(z, zm, attn_sinks[0])
    ret = _retention(z, zm, ret_norm_w[0][None, :])
    h1, h1_slabs = _outproj(att, ret, w_out[0].astype(BF16), b_out[0][None, :], xs, tm, 1024)

    ffn_w = ffn_norm_w[0][None, :]
    idx_t, gate_t, inv_rms = _router(h1, ffn_w, router_w[0].T, router_b[0][:, None], tm)
    gates = gate_t.T
    n_used, block_valid, block_exp, row_tok3, pos = _routing_metadata(idx_t.T, EXPERT_TM)

    inv_rows = inv_rms[row_tok3.reshape(-1)]
    xb = _gather_norm(n_used, row_tok3, h1_slabs, inv_rows, ffn_w, EXPERT_TM)
    act = _gateup(n_used, block_valid, block_exp, xb, w_gu[0], b_gu[0][:, None, :], EXPERT_TM)
    yb = _down(n_used, block_valid, block_exp, act, w_down[0], b_down[0][:, None, :], EXPERT_TM)

    tq = WINDOW
    pos3 = pos.reshape(s // tq, tq, TOP_K).transpose(0, 2, 1).reshape(s // tq, 1, TOP_K * tq)
    out = _combine(pos3, gates, h1, final_norm_w[None, :], yb, tq)
    return out[None]
```

```python
import functools

import jax
import jax.numpy as jnp
import numpy as np
from jax import lax
from jax.experimental import pallas as pl
from jax.experimental.pallas import tpu as pltpu

F32 = jnp.float32
BF16 = jnp.bfloat16

D_MODEL = 4096
N_META = 16
ATT_HEADS = 32
ATT_KV_HEADS = 4
ATT_HEAD_DIM = 64
ATT_GROUP = ATT_HEADS // ATT_KV_HEADS
WINDOW = 128
ROPE_THETA = 500000.0
ROPE_DIM = ATT_HEAD_DIM // 4
RET_HEADS = 8
RET_KEY_DIM = 128
RET_VALUE_DIM = 256
RET_CHUNK = 128
RET_ROPE_THETA = 10000.0
ATT_WIDTH = ATT_HEADS * ATT_HEAD_DIM
KV_WIDTH = ATT_KV_HEADS * ATT_HEAD_DIM
RET_QK_WIDTH = RET_HEADS * RET_KEY_DIM
RET_WIDTH = RET_HEADS * RET_VALUE_DIM
IN_WIDTH = ATT_WIDTH + 2 * KV_WIDTH + 2 * RET_QK_WIDTH + 2 * RET_WIDTH
OFF_AQ = 0
OFF_RQ = OFF_AQ + ATT_WIDTH
OFF_RK = OFF_RQ + RET_QK_WIDTH
OFF_RV = OFF_RK + RET_QK_WIDTH
OFF_RG = OFF_RV + RET_WIDTH
OFF_AK = OFF_RG + RET_WIDTH
OFF_AV = OFF_AK + KV_WIDTH
N_EXPERTS = 32
TOP_K = 4
D_EXPERT = D_MODEL // 2
SWIGLU_ALPHA = 1.702
SWIGLU_LIMIT = 7.0
NORM_EPS = 1e-5
GN_EPS = 1e-6

LANES = 128
VMEM_LIMIT = 56 * 1024 * 1024

IN_TN = 512
EXPERT_TM = 512
EXPERT_HALF = EXPERT_TM // 2
GU_TN = 512
DOWN_TN = 1024
KEYS_PAD = 3 * WINDOW
NEG_BIG = -1e30
SLAB_ROWS = D_MODEL // LANES
SLAB_PITCH = 40
GATHER_GROUP = 64
COMBINE_GROUP = 32


def _cparams(sem):
    return pltpu.CompilerParams(dimension_semantics=sem, vmem_limit_bytes=VMEM_LIMIT)


def _src_tile(j):
    assert 2 * KV_WIDTH == IN_TN and ATT_WIDTH % IN_TN == 0
    n_aq = ATT_WIDTH // IN_TN
    n_tiles = IN_WIDTH // IN_TN
    return jnp.where(j < n_aq, j, jnp.where(j < n_tiles - 1, j + 1, n_aq))


def _chunk_classes():
    cls = []
    for c in range(IN_WIDTH // LANES):
        off = c * LANES
        if off < OFF_RQ or OFF_AK <= off < OFF_AV:
            cls.append("a")
        elif OFF_RQ <= off < OFF_RV:
            cls.append("r")
        else:
            cls.append("n")
    return cls


def _inproj_kernel(x_ref, nw_ref, w_ref, b_ref, aa_ref, ab_ref, ac_ref, ra_ref, rb_ref, o_ref, hn_ref):
    j = pl.program_id(1)

    @pl.when(j == 0)
    def _():
        x = x_ref[...]
        ms = jnp.mean(x * x, axis=-1, keepdims=True)
        hn_ref[...] = (x * lax.rsqrt(ms + NORM_EPS) * nw_ref[...]).astype(BF16)

    y = jnp.dot(hn_ref[...], w_ref[...], preferred_element_type=F32) + b_ref[...]

    cpt = IN_TN // LANES
    classes = _chunk_classes()
    n_tiles = IN_WIDTH // IN_TN
    patterns = {}
    for t in range(n_tiles):
        patterns.setdefault(tuple(classes[t * cpt:(t + 1) * cpt]), []).append(t)

    def emit(pattern):
        for c, kind in enumerate(pattern):
            yc = y[:, c * LANES:(c + 1) * LANES]
            if kind == "a":
                yc = (yc * aa_ref[...] + pltpu.roll(yc, LANES - ROPE_DIM // 2, 1) * ab_ref[...]
                      + pltpu.roll(yc, ROPE_DIM // 2, 1) * ac_ref[...])
            elif kind == "r":
                yc = yc * ra_ref[...] + pltpu.roll(yc, RET_KEY_DIM // 2, 1) * rb_ref[...]
            o_ref[:, c * LANES:(c + 1) * LANES] = yc.astype(o_ref.dtype)

    for pattern, tiles in patterns.items():
        cond = functools.reduce(jnp.logical_or, [j == t for t in tiles])
        pl.when(cond)(functools.partial(emit, pattern))


def _inproj(x, norm_w, w_bf, b, tabs, tm):
    m = x.shape[0]
    aa, ab, ac, ra, rb = tabs
    row = lambda i, j: (i, 0)
    tab_spec = pl.BlockSpec((tm, LANES), row)
    return pl.pallas_call(
        _inproj_kernel,
        grid=(m // tm, IN_WIDTH // IN_TN),
        in_specs=[
            pl.BlockSpec((tm, D_MODEL), row),
            pl.BlockSpec((1, D_MODEL), lambda i, j: (0, 0)),
            pl.BlockSpec((D_MODEL, IN_TN), lambda i, j: (0, _src_tile(j))),
            pl.BlockSpec((1, IN_TN), lambda i, j: (0, _src_tile(j))),
            tab_spec, tab_spec, tab_spec, tab_spec, tab_spec,
        ],
        out_specs=pl.BlockSpec((tm, IN_TN), lambda i, j: (i, j)),
        out_shape=jax.ShapeDtypeStruct((m, IN_WIDTH), BF16),
        scratch_shapes=[pltpu.VMEM((tm, D_MODEL), BF16)],
        compiler_params=_cparams(("parallel", "arbitrary")),
        name="inproj",
    )(x, norm_w, w_bf, b, aa, ab, ac, ra, rb)


def _rope_tables(first_pos, n_pos):
    pos = (first_pos + jnp.arange(n_pos, dtype=jnp.int32)).astype(F32)[:, None]
    lane = np.arange(LANES)
    half = ROPE_DIM // 2
    inv = ROPE_THETA ** (-jnp.arange(half, dtype=F32) / half)
    ang = pos * inv[None, :]
    cos, sin = jnp.cos(ang), jnp.sin(ang)
    c = lane % ATT_HEAD_DIM
    f = c % half
    is_lo = jnp.asarray(c < half)[None, :]
    is_hi = jnp.asarray((c >= half) & (c < ROPE_DIM))[None, :]
    cos_l, sin_l = cos[:, f], sin[:, f]
    aa = jnp.where(is_lo | is_hi, cos_l, 1.0)
    ab = jnp.where(is_lo, -sin_l, 0.0)
    ac = jnp.where(is_hi, sin_l, 0.0)
    rhalf = RET_KEY_DIM // 2
    rinv = RET_ROPE_THETA ** (-jnp.arange(rhalf, dtype=F32) / rhalf)
    rang = pos * rinv[None, :]
    rcos, rsin = jnp.cos(rang), jnp.sin(rang)
    rf = lane % rhalf
    ra = rcos[:, rf]
    rb = jnp.where(jnp.asarray(lane < rhalf)[None, :], -rsin[:, rf], rsin[:, rf])
    return aa, ab, ac, ra, rb


def _attn_kernel(sink_ref, q_ref, ko_ref, vo_ref, kp_ref, vp_ref, km_ref, vm_ref, o_ref):
    n = pl.program_id(0)
    w = WINDOW
    qi = lax.broadcasted_iota(jnp.int32, (w, KEYS_PAD), 0)
    ji = lax.broadcasted_iota(jnp.int32, (w, KEYS_PAD), 1)
    vis_prev = (ji < w) & (ji > qi) & (n > 0)
    vis_own = (ji >= w) & (ji < 2 * w) & (ji - w <= qi)
    vis_meta = (ji >= 2 * w) & (ji < 2 * w + N_META)
    mask = vis_prev | vis_own | vis_meta
    lane = lax.broadcasted_iota(jnp.int32, (KEYS_PAD, LANES), 1)
    lo_half = lane < ATT_HEAD_DIM
    olane = lax.broadcasted_iota(jnp.int32, (w, LANES), 1) < ATT_HEAD_DIM
    scale = ATT_HEAD_DIM ** -0.5

    def spread(prev_ref, own_ref, meta_ref, g):
        c0 = (g // 2) * LANES
        chunk = jnp.concatenate([prev_ref[:, c0:c0 + LANES], own_ref[:, c0:c0 + LANES],
                                 meta_ref[:, c0:c0 + LANES]], axis=0).astype(F32)
        swapped = pltpu.roll(chunk, ATT_HEAD_DIM, 1)
        in_lo, in_hi = (chunk, swapped) if g % 2 == 0 else (swapped, chunk)
        even = jnp.where(lo_half, in_lo, 0.0)
        odd = jnp.where(lo_half, 0.0, in_hi)
        return jnp.concatenate([even, odd], axis=0).astype(BF16)

    pairs = ATT_GROUP // 2
    for g in range(ATT_KV_HEADS):
        kz = spread(kp_ref, ko_ref, km_ref, g)
        vz = spread(vp_ref, vo_ref, vm_ref, g)
        c0 = g * pairs
        q4 = jnp.concatenate([q_ref[:, (c0 + r) * LANES:(c0 + r + 1) * LANES] for r in range(pairs)], axis=0)
        s4 = lax.dot_general(q4, kz, (((1,), (1,)), ((), ())), preferred_element_type=F32) * scale
        ps, dens = [], []
        for r in range(pairs):
            prow, drow = [], []
            for half in range(2):
                sink = sink_ref[2 * (c0 + r) + half]
                s = jnp.where(mask, s4[r * w:(r + 1) * w, half * KEYS_PAD:(half + 1) * KEYS_PAD], NEG_BIG)
                m = jnp.maximum(jnp.max(s, axis=-1, keepdims=True), sink)
                p = jnp.exp(s - m)
                drow.append(jnp.sum(p, axis=-1, keepdims=True) + jnp.exp(sink - m))
                prow.append(p.astype(BF16))
            ps.append(jnp.concatenate(prow, axis=1))
            dens.append(jnp.where(olane, drow[0], drow[1]))
        o4 = jnp.dot(jnp.concatenate(ps, axis=0), vz, preferred_element_type=F32)
        for r in range(pairs):
            o = o4[r * w:(r + 1) * w] / dens[r]
            o_ref[:, (c0 + r) * LANES:(c0 + r + 1) * LANES] = o.astype(o_ref.dtype)


def _attention(z, zm, sinks):
    s = z.shape[0]
    nb = s // WINDOW
    kcol, vcol = OFF_AK // KV_WIDTH, OFF_AV // KV_WIDTH
    prev = lambda n: jnp.maximum(n - 1, 0)
    return pl.pallas_call(
        _attn_kernel,
        grid=(nb,),
        in_specs=[
            pl.BlockSpec(memory_space=pltpu.SMEM),
            pl.BlockSpec((WINDOW, ATT_WIDTH), lambda n: (n, 0)),
            pl.BlockSpec((WINDOW, KV_WIDTH), lambda n: (n, kcol)),
            pl.BlockSpec((WINDOW, KV_WIDTH), lambda n: (n, vcol)),
            pl.BlockSpec((WINDOW, KV_WIDTH), lambda n: (prev(n), kcol)),
            pl.BlockSpec((WINDOW, KV_WIDTH), lambda n: (prev(n), vcol)),
            pl.BlockSpec((WINDOW, KV_WIDTH), lambda n: (0, kcol)),
            pl.BlockSpec((WINDOW, KV_WIDTH), lambda n: (0, vcol)),
        ],
        out_specs=pl.BlockSpec((WINDOW, ATT_WIDTH), lambda n: (n, 0)),
        out_shape=jax.ShapeDtypeStruct((s, ATT_WIDTH), BF16),
        compiler_params=_cparams(("parallel",)),
        name="swa_attention",
    )(sinks, z, z, z, z, z, zm, zm)


def _ret_kernel(gc_ref, q_ref, k_ref, v_ref, g_ref, km_ref, vm_ref, dm_ref, kd_ref, qd_ref, wm_ref, nw_ref,
                o_ref, u_ref):
    c = pl.program_id(0)
    dk, dv = RET_KEY_DIM, RET_VALUE_DIM

    @pl.when(c == 0)
    def _():
        for hh in range(RET_HEADS):
            kmw = (km_ref[:, hh * dk:(hh + 1) * dk].astype(F32) * wm_ref[hh]).T.astype(BF16)
            u_ref[hh] = jnp.dot(kmw, vm_ref[:, hh * dv:(hh + 1) * dv], preferred_element_type=F32)

    for hh in range(RET_HEADS):
        q = q_ref[:, hh * dk:(hh + 1) * dk]
        k = k_ref[:, hh * dk:(hh + 1) * dk]
        v = v_ref[:, hh * dv:(hh + 1) * dv]
        u = u_ref[hh]
        inner = lax.dot_general(q, k, (((1,), (1,)), ((), ())), preferred_element_type=F32) * dm_ref[hh]
        intra = jnp.dot(inner.astype(BF16), v, preferred_element_type=F32)
        cross = jnp.dot(q, u.astype(BF16), preferred_element_type=F32) * qd_ref[hh]
        y = intra + cross
        mu = jnp.mean(y, axis=-1, keepdims=True)
        yc = y - mu
        var = jnp.mean(yc * yc, axis=-1, keepdims=True)
        yn = yc * lax.rsqrt(var + GN_EPS)
        g = g_ref[:, hh * dv:(hh + 1) * dv].astype(F32)
        silu = g * (1.0 / (1.0 + jnp.exp(-g)))
        o_ref[:, hh * dv:(hh + 1) * dv] = (silu * yn * nw_ref[:, hh * dv:(hh + 1) * dv]).astype(o_ref.dtype)
        kdt = (k.astype(F32) * kd_ref[hh]).T.astype(BF16)
        u_ref[hh] = u * gc_ref[hh] + jnp.dot(kdt, v, preferred_element_type=F32)


def _ret_tables():
    scale = RET_KEY_DIM ** -0.5
    log_g = jnp.log1p(-(2.0 ** (-5.0 - jnp.arange(RET_HEADS, dtype=F32))))
    i = jnp.arange(RET_CHUNK, dtype=F32)
    diff = i[:, None] - i[None, :]
    dm = jnp.where(diff >= 0, jnp.exp(jnp.maximum(diff, 0.0)[None] * log_g[:, None, None]), 0.0) * scale
    qd = (jnp.exp((i + 1.0)[None, :] * log_g[:, None]) * scale)[:, :, None]
    kd = jnp.exp((RET_CHUNK - 1.0 - i)[None, :] * log_g[:, None])[:, :, None]
    jm = jnp.arange(RET_CHUNK, dtype=F32)
    wm = jnp.where(jm[None, :] < N_META, jnp.exp((N_META - 1 - jm)[None, :] * log_g[:, None]), 0.0)[:, :, None]
    gc = jnp.exp(RET_CHUNK * log_g)
    return gc, dm, kd, qd, wm


def _retention(z, zm, ret_norm_w):
    s = z.shape[0]
    nc = s // RET_CHUNK
    gc, dm, kd, qd, wm = _ret_tables()
    qc, kc = OFF_RQ // RET_QK_WIDTH, OFF_RK // RET_QK_WIDTH
    vc, gcol = OFF_RV // RET_WIDTH, OFF_RG // RET_WIDTH
    full3 = lambda c: (0, 0, 0)
    vec_spec = pl.BlockSpec((RET_HEADS, RET_CHUNK, 1), full3)
    return pl.pallas_call(
        _ret_kernel,
        grid=(nc,),
        in_specs=[
            pl.BlockSpec(memory_space=pltpu.SMEM),
            pl.BlockSpec((RET_CHUNK, RET_QK_WIDTH), lambda c: (c, qc)),
            pl.BlockSpec((RET_CHUNK, RET_QK_WIDTH), lambda c: (c, kc)),
            pl.BlockSpec((RET_CHUNK, RET_WIDTH), lambda c: (c, vc)),
            pl.BlockSpec((RET_CHUNK, RET_WIDTH), lambda c: (c, gcol)),
            pl.BlockSpec((RET_CHUNK, RET_QK_WIDTH), lambda c: (0, kc)),
            pl.BlockSpec((RET_CHUNK, RET_WIDTH), lambda c: (0, vc)),
            pl.BlockSpec((RET_HEADS, RET_CHUNK, RET_CHUNK), full3),
            vec_spec, vec_spec, vec_spec,
            pl.BlockSpec((1, RET_WIDTH), lambda c: (0, 0)),
        ],
        out_specs=pl.BlockSpec((RET_CHUNK, RET_WIDTH), lambda c: (c, 0)),
        out_shape=jax.ShapeDtypeStruct((s, RET_WIDTH), BF16),
        scratch_shapes=[pltpu.VMEM((RET_HEADS, RET_KEY_DIM, RET_VALUE_DIM), F32)],
        compiler_params=_cparams(("arbitrary",)),
        name="retention",
    )(gc, z, z, z, z, zm, zm, dm, kd, qd, wm, ret_norm_w)


def _store_slabs(slab_ref, rows, value):
    n = slab_ref.shape[1]
    flat = slab_ref.reshape(slab_ref.shape[0] * n, LANES)
    for k in range(n):
        flat[pl.ds(k, rows, stride=n), :] = value[:, k * LANES:(k + 1) * LANES]


def _outproj_kernel(a_ref, r_ref, wt_ref, wb_ref, b_ref, x_ref, o_ref, slab_ref):
    acc = jnp.dot(a_ref[...], wt_ref[...], preferred_element_type=F32)
    acc += jnp.dot(r_ref[...], wb_ref[...], preferred_element_type=F32)
    h = x_ref[...] + acc + b_ref[...]
    o_ref[...] = h
    _store_slabs(slab_ref, h.shape[0], h)


def _outproj(att, ret, w_bf, b, x, tm, tn):
    s = x.shape[0]
    return pl.pallas_call(
        _outproj_kernel,
        grid=(s // tm, D_MODEL // tn),
        in_specs=[
            pl.BlockSpec((tm, ATT_WIDTH), lambda i, j: (i, 0)),
            pl.BlockSpec((tm, RET_WIDTH), lambda i, j: (i, 0)),
            pl.BlockSpec((ATT_WIDTH, tn), lambda i, j: (0, j)),
            pl.BlockSpec((RET_WIDTH, tn), lambda i, j: (1, j)),
            pl.BlockSpec((1, tn), lambda i, j: (0, j)),
            pl.BlockSpec((tm, tn), lambda i, j: (i, j)),
        ],
        out_specs=[pl.BlockSpec((tm, tn), lambda i, j: (i, j)),
                   pl.BlockSpec((tm, tn // LANES, LANES), lambda i, j: (i, j, 0))],
        out_shape=[jax.ShapeDtypeStruct((s, D_MODEL), F32),
                   jax.ShapeDtypeStruct((s, D_MODEL // LANES, LANES), F32)],
        compiler_params=_cparams(("parallel", "arbitrary")),
        name="outproj",
    )(att, ret, w_bf, w_bf, b, x)


def _router_kernel(h_ref, nw_ref, rw_ref, rb_ref, idx_ref, gate_ref):
    x = h_ref[...]
    ms = jnp.mean(x * x, axis=-1, keepdims=True)
    hn = x * lax.rsqrt(ms + NORM_EPS) * nw_ref[...]
    logits = lax.dot_general(rw_ref[...], hn, (((1,), (1,)), ((), ())), preferred_element_type=F32,
                             precision=lax.Precision.HIGHEST) + rb_ref[...]
    eid = lax.broadcasted_iota(jnp.int32, logits.shape, 0)
    vals, idxs = [], []
    for _ in range(TOP_K):
        m = jnp.max(logits, axis=0, keepdims=True)
        sel = jnp.min(jnp.where(logits == m, eid, N_EXPERTS), axis=0, keepdims=True)
        vals.append(m)
        idxs.append(sel)
        logits = jnp.where(eid == sel, -jnp.inf, logits)
    es = [jnp.exp(v - vals[0]) for v in vals]
    tot = functools.reduce(lambda a, b: a + b, es)
    idx_ref[...] = jnp.concatenate(idxs, axis=0)
    gate_ref[...] = jnp.concatenate([e / tot for e in es], axis=0)


def _router(h1, norm_w, rw_t, rb, tm):
    s = h1.shape[0]
    return pl.pallas_call(
        _router_kernel,
        grid=(s // tm,),
        in_specs=[
            pl.BlockSpec((tm, D_MODEL), lambda i: (i, 0)),
            pl.BlockSpec((1, D_MODEL), lambda i: (0, 0)),
            pl.BlockSpec((N_EXPERTS, D_MODEL), lambda i: (0, 0)),
            pl.BlockSpec((N_EXPERTS, 1), lambda i: (0, 0)),
        ],
        out_specs=[pl.BlockSpec((TOP_K, tm), lambda i: (0, i)), pl.BlockSpec((TOP_K, tm), lambda i: (0, i))],
        out_shape=[jax.ShapeDtypeStruct((TOP_K, s), jnp.int32), jax.ShapeDtypeStruct((TOP_K, s), F32)],
        compiler_params=_cparams(("parallel",)),
        name="router",
    )(h1, norm_w, rw_t, rb)


def _slab_copy(src_hbm, idx_ref, buf_ref, sems, slot, r):
    return pltpu.make_async_copy(src_hbm.at[idx_ref[0, r]], buf_ref.at[slot, r, pl.ds(0, SLAB_ROWS), :],
                                 sems.at[slot])


def _slot_wait(src_hbm, buf_ref, sems, slot):
    n = buf_ref.shape[1]
    pltpu.make_async_copy(src_hbm.at[pl.ds(0, n)], buf_ref.at[slot, :, pl.ds(0, SLAB_ROWS), :],
                          sems.at[slot]).wait()


def _slab_chunk(buf_ref, slot, r0, rows, k):
    n_slots, slot_slabs = buf_ref.shape[:2]
    flat = buf_ref.reshape(n_slots * slot_slabs * SLAB_PITCH, LANES)
    base = (slot * slot_slabs + r0) * SLAB_PITCH + k
    return flat[pl.ds(base, rows, stride=SLAB_PITCH), :]


def _gather_norm_kernel(nused_ref, tok_ref, tokn_ref, h_hbm, nw_ref, o_ref, inv_ref, buf_ref, sems):
    b = pl.program_id(0)
    nu = nused_ref[0]
    tm = o_ref.shape[0]
    slot = b % 2
    grp = GATHER_GROUP
    n_groups = tm // grp

    def issue_block(idx_ref, s):
        def step(i, carry):
            r0 = i * grp
            for r in range(grp):
                _slab_copy(h_hbm, idx_ref, buf_ref, sems, s, r0 + r).start()
            return carry
        lax.fori_loop(0, n_groups, step, 0)

    @pl.when(b == 0)
    def _():
        issue_block(tok_ref, 0)

    @pl.when(b + 1 < nu)
    def _():
        issue_block(tokn_ref, 1 - slot)

    @pl.when(b < nu)
    def _():
        _slot_wait(h_hbm, buf_ref, sems, slot)

        def body(i, carry):
            r0 = pl.multiple_of(i * grp, grp)
            ssq = jnp.zeros((grp, LANES), F32)
            for k in range(SLAB_ROWS):
                x = _slab_chunk(buf_ref, slot, r0, grp, k)
                cols = slice(k * LANES, (k + 1) * LANES)
                o_ref[pl.ds(r0, grp), cols] = (x * nw_ref[:, cols]).astype(o_ref.dtype)
                ssq = ssq + x * x
            ms = jnp.sum(ssq, axis=-1, keepdims=True) * (1.0 / D_MODEL)
            inv_ref[pl.ds(r0, grp), :] = lax.rsqrt(ms + NORM_EPS)
            return carry
        lax.fori_loop(0, n_groups, body, 0)

    @pl.when(b >= nu)
    def _():
        o_ref[...] = jnp.zeros_like(o_ref)
        inv_ref[...] = jnp.zeros_like(inv_ref)


def _gather_norm(n_used, row_tok3, h1_slabs, norm_w, tm):
    nblk = row_tok3.shape[0]
    cur = lambda b, nu: (jnp.minimum(b, nu[0] - 1), 0, 0)
    nxt = lambda b, nu: (jnp.minimum(b + 1, nu[0] - 1), 0, 0)
    return pl.pallas_call(
        _gather_norm_kernel,
        grid_spec=pltpu.PrefetchScalarGridSpec(
            num_scalar_prefetch=1,
            grid=(nblk,),
            in_specs=[
                pl.BlockSpec((None, 1, tm), cur, memory_space=pltpu.SMEM),
                pl.BlockSpec((None, 1, tm), nxt, memory_space=pltpu.SMEM),
                pl.BlockSpec(memory_space=pl.ANY),
                pl.BlockSpec((1, D_MODEL), lambda b, nu: (0, 0)),
            ],
            out_specs=[pl.BlockSpec((tm, D_MODEL), lambda b, nu: (b, 0)),
                       pl.BlockSpec((tm, 1), lambda b, nu: (b, 0))],
            scratch_shapes=[pltpu.VMEM((2, tm, SLAB_PITCH, LANES), F32), pltpu.SemaphoreType.DMA((2,))],
        ),
        out_shape=[jax.ShapeDtypeStruct((nblk * tm, D_MODEL), BF16),
                   jax.ShapeDtypeStruct((nblk * tm, 1), F32)],
        compiler_params=_cparams(("arbitrary",)),
        name="gather_norm",
    )(n_used, row_tok3, row_tok3, h1_slabs, norm_w)


def _swiglu(x, inv, wg, wl, bg, bl):
    glu = jnp.dot(x, wg, preferred_element_type=F32) * inv + bg
    lin = jnp.dot(x, wl, preferred_element_type=F32) * inv + bl
    glu = jnp.minimum(glu, SWIGLU_LIMIT)
    lin = jnp.clip(lin, -SWIGLU_LIMIT, SWIGLU_LIMIT)
    return glu * (1.0 / (1.0 + jnp.exp(-SWIGLU_ALPHA * glu))) * (lin + 1.0)


def _gateup_kernel(bv_ref, be_ref, nu_ref, x_ref, inv_ref, wg_ref, wl_ref, bg_ref, bl_ref, o_ref, wgb_ref,
                   wlb_ref):
    b = pl.program_id(1)
    valid = bv_ref[b]
    new_expert = (b == 0) | (be_ref[b] != be_ref[jnp.maximum(b - 1, 0)])
    hm = EXPERT_HALF

    @pl.when((valid > 0) & new_expert)
    def _():
        wgb_ref[...] = wg_ref[...].astype(BF16)
        wlb_ref[...] = wl_ref[...].astype(BF16)

    @pl.when(valid > hm)
    def _():
        act = _swiglu(x_ref[...], inv_ref[...], wgb_ref[...], wlb_ref[...], bg_ref[...], bl_ref[...])
        o_ref[...] = act.astype(o_ref.dtype)

    @pl.when((valid > 0) & (valid <= hm))
    def _():
        act = _swiglu(x_ref[:hm, :], inv_ref[:hm, :], wgb_ref[...], wlb_ref[...], bg_ref[...], bl_ref[...])
        o_ref[:hm, :] = act.astype(o_ref.dtype)
        o_ref[hm:, :] = jnp.zeros((o_ref.shape[0] - hm, o_ref.shape[1]), o_ref.dtype)

    @pl.when(valid == 0)
    def _():
        o_ref[...] = jnp.zeros_like(o_ref)


def _gateup(n_used, block_valid, block_exp, xb, inv_rows, w_gu, b_gu3, tm):
    nblk = xb.shape[0] // tm
    nj = D_EXPERT // GU_TN
    last = lambda b, nu: jnp.minimum(b, nu[0] - 1)
    return pl.pallas_call(
        _gateup_kernel,
        grid_spec=pltpu.PrefetchScalarGridSpec(
            num_scalar_prefetch=3,
            grid=(nj, nblk),
            in_specs=[
                pl.BlockSpec((tm, D_MODEL), lambda j, b, bv, be, nu: (last(b, nu), 0)),
                pl.BlockSpec((tm, 1), lambda j, b, bv, be, nu: (last(b, nu), 0)),
                pl.BlockSpec((None, D_MODEL, GU_TN), lambda j, b, bv, be, nu: (be[b], 0, j)),
                pl.BlockSpec((None, D_MODEL, GU_TN), lambda j, b, bv, be, nu: (be[b], 0, j + nj)),
                pl.BlockSpec((None, 1, GU_TN), lambda j, b, bv, be, nu: (be[b], 0, j)),
                pl.BlockSpec((None, 1, GU_TN), lambda j, b, bv, be, nu: (be[b], 0, j + nj)),
            ],
            out_specs=pl.BlockSpec((tm, GU_TN), lambda j, b, bv, be, nu: (b, j)),
            scratch_shapes=[pltpu.VMEM((D_MODEL, GU_TN), BF16), pltpu.VMEM((D_MODEL, GU_TN), BF16)],
        ),
        out_shape=jax.ShapeDtypeStruct((nblk * tm, D_EXPERT), BF16),
        compiler_params=_cparams(("arbitrary", "arbitrary")),
        name="expert_gateup",
    )(block_valid, block_exp, n_used, xb, inv_rows, w_gu, w_gu, b_gu3, b_gu3)


def _down_kernel(bv_ref, be_ref, nu_ref, a_ref, w_ref, bias_ref, o_ref, wb_ref):
    b = pl.program_id(1)
    valid = bv_ref[b]
    new_expert = (b == 0) | (be_ref[b] != be_ref[jnp.maximum(b - 1, 0)])
    hm = EXPERT_HALF

    @pl.when((valid > 0) & new_expert)
    def _():
        wb_ref[...] = w_ref[...].astype(BF16)

    @pl.when(valid > hm)
    def _():
        res = jnp.dot(a_ref[...], wb_ref[...], preferred_element_type=F32) + bias_ref[...]
        _store_slabs(o_ref, o_ref.shape[0], res)

    @pl.when((valid > 0) & (valid <= hm))
    def _():
        res = jnp.dot(a_ref[:hm, :], wb_ref[...], preferred_element_type=F32) + bias_ref[...]
        _store_slabs(o_ref, hm, res)
        o_ref[hm:] = jnp.zeros((o_ref.shape[0] - hm,) + o_ref.shape[1:], o_ref.dtype)

    @pl.when(valid == 0)
    def _():
        o_ref[...] = jnp.zeros_like(o_ref)


def _down(n_used, block_valid, block_exp, act, w_down, b_down3, tm):
    nblk = act.shape[0] // tm
    nj = D_MODEL // DOWN_TN
    last = lambda b, nu: jnp.minimum(b, nu[0] - 1)
    return pl.pallas_call(
        _down_kernel,
        grid_spec=pltpu.PrefetchScalarGridSpec(
            num_scalar_prefetch=3,
            grid=(nj, nblk),
            in_specs=[
                pl.BlockSpec((tm, D_EXPERT), lambda j, b, bv, be, nu: (last(b, nu), 0)),
                pl.BlockSpec((None, D_EXPERT, DOWN_TN), lambda j, b, bv, be, nu: (be[b], 0, j)),
                pl.BlockSpec((None, 1, DOWN_TN), lambda j, b, bv, be, nu: (be[b], 0, j)),
            ],
            out_specs=pl.BlockSpec((tm, DOWN_TN // LANES, LANES), lambda j, b, bv, be, nu: (b, j, 0)),
            scratch_shapes=[pltpu.VMEM((D_EXPERT, DOWN_TN), BF16)],
        ),
        out_shape=jax.ShapeDtypeStruct((nblk * tm, SLAB_ROWS, LANES), F32),
        compiler_params=_cparams(("arbitrary", "arbitrary")),
        name="expert_down",
    )(block_valid, block_exp, n_used, act, w_down, b_down3)


def _combine_kernel(pos_ref, posn_ref, g_ref, h_ref, nw_ref, y_hbm, o_ref, buf_ref, sems):
    i = pl.program_id(0)
    n_tiles = pl.num_programs(0)
    tq = h_ref.shape[0]
    slot = i % 2
    grp = COMBINE_GROUP
    n_groups = tq // grp

    def issue(idx_ref, s, r0):
        for k in range(TOP_K):
            for r in range(grp):
                _slab_copy(y_hbm, idx_ref, buf_ref, sems, s, k * tq + r0 + r).start()

    @pl.when(i == 0)
    def _():
        def first(t, carry):
            issue(pos_ref, 0, t * grp)
            return carry
        lax.fori_loop(0, n_groups, first, 0)

    _slot_wait(y_hbm, buf_ref, sems, slot)
    has_next = i + 1 < n_tiles

    def body(t, carry):
        r0 = pl.multiple_of(t * grp, grp)

        @pl.when(has_next)
        def _():
            issue(posn_ref, 1 - slot, r0)

        rows = pl.ds(r0, grp)
        g = g_ref[rows, :]
        gk = [jnp.broadcast_to(g[:, k:k + 1], (grp, LANES)) for k in range(TOP_K)]
        ssq = jnp.zeros((grp, LANES), F32)
        for c in range(SLAB_ROWS):
            cols = slice(c * LANES, (c + 1) * LANES)
            acc = h_ref[rows, cols]
            for k in range(TOP_K):
                acc = acc + gk[k] * _slab_chunk(buf_ref, slot, k * tq + r0, grp, c)
            o_ref[rows, cols] = acc
            ssq = ssq + acc * acc
        sc = lax.rsqrt(jnp.sum(ssq, axis=-1, keepdims=True) * (1.0 / D_MODEL) + NORM_EPS)
        o_ref[rows, :] = o_ref[rows, :] * sc * nw_ref[...]
        return carry
    lax.fori_loop(0, n_groups, body, 0)


def _combine(pos3, gates, h1, norm_w, yb, tq):
    s = h1.shape[0]
    nt = s // tq
    return pl.pallas_call(
        _combine_kernel,
        grid=(nt,),
        in_specs=[
            pl.BlockSpec((None, 1, TOP_K * tq), lambda i: (i, 0, 0), memory_space=pltpu.SMEM),
            pl.BlockSpec((None, 1, TOP_K * tq), lambda i: (jnp.minimum(i + 1, nt - 1), 0, 0),
                         memory_space=pltpu.SMEM),
            pl.BlockSpec((tq, TOP_K), lambda i: (i, 0)),
            pl.BlockSpec((tq, D_MODEL), lambda i: (i, 0)),
            pl.BlockSpec((1, D_MODEL), lambda i: (0, 0)),
            pl.BlockSpec(memory_space=pl.ANY),
        ],
        out_specs=pl.BlockSpec((tq, D_MODEL), lambda i: (i, 0)),
        out_shape=jax.ShapeDtypeStruct((s, D_MODEL), F32),
        scratch_shapes=[pltpu.VMEM((2, TOP_K * tq, SLAB_PITCH, LANES), F32), pltpu.SemaphoreType.DMA((2,))],
        compiler_params=_cparams(("arbitrary",)),
        name="combine_norm",
    )(pos3, pos3, gates, h1, norm_w, yb)


def _routing_metadata(top_idx, tm):
    s = top_idx.shape[0]
    n_assign = s * TOP_K
    nblk = -(-(n_assign + N_EXPERTS * (tm - 1)) // tm)
    e_flat = top_idx.reshape(n_assign)
    order = jnp.argsort(e_flat).astype(jnp.int32)
    rank = jnp.argsort(order).astype(jnp.int32)
    experts = jnp.arange(N_EXPERTS, dtype=jnp.int32)
    counts = jnp.sum((e_flat[:, None] == experts[None, :]).astype(jnp.int32), axis=0)
    padded = ((counts + tm - 1) // tm) * tm
    start_sorted = jnp.cumsum(counts) - counts
    start_pad = jnp.cumsum(padded) - padded
    pos = start_pad[e_flat] + rank - start_sorted[e_flat]
    n_used = jnp.sum(padded) // tm
    blocks = jnp.arange(nblk, dtype=jnp.int32)
    block_exp = jnp.sum((start_pad[None, :] <= (blocks * tm)[:, None]).astype(jnp.int32), axis=1) - 1
    block_exp = block_exp[jnp.minimum(blocks, n_used - 1)]
    block_valid = jnp.clip(counts[block_exp] - (blocks * tm - start_pad[block_exp]), 0, tm)
    block_valid = jnp.where(blocks < n_used, block_valid, 0)
    rows = jnp.arange(nblk * tm, dtype=jnp.int32)
    row_exp = jnp.broadcast_to(block_exp[:, None], (nblk, tm)).reshape(nblk * tm)
    off = rows - start_pad[row_exp]
    is_row = (off < counts[row_exp]) & (rows < n_used * tm)
    src = jnp.clip(start_sorted[row_exp] + off, 0, n_assign - 1)
    row_tok = jnp.where(is_row, order[src] // TOP_K, 0)
    as_i32 = lambda t: t.astype(jnp.int32)
    return (as_i32(n_used).reshape(1), as_i32(block_valid), as_i32(block_exp),
            as_i32(row_tok).reshape(nblk, 1, tm), as_i32(pos).reshape(s, TOP_K))


def _pick(n, pref):
    t = pref
    while n % t:
        t //= 2
    return t


def kernel(x, meta_tokens, attn_norm_w, w_in, b_in, attn_sinks, ret_norm_w, w_out, b_out, ffn_norm_w, router_w,
           router_b, w_gu, b_gu, w_down, b_down, final_norm_w):
    assert x.shape[0] == 1 and attn_norm_w.shape[0] == 1
    xs = x[0]
    s = xs.shape[0]
    assert s % WINDOW == 0
    tm = _pick(s, 512)

    w_in_bf = w_in[0].astype(BF16)
    b_in2 = b_in[0][None, :]
    nw = attn_norm_w[0][None, :]
    z = _inproj(xs, nw, w_in_bf, b_in2, _rope_tables(N_META, s), tm)
    zm = _inproj(meta_tokens, nw, w_in_bf, b_in2, _rope_tables(0, N_META), N_META)
    zm = jnp.pad(zm, ((0, WINDOW - N_META), (0, 0)))

    att = _attention(z, zm, attn_sinks[0])
    ret = _retention(z, zm, ret_norm_w[0][None, :])
    h1, h1_slabs = _outproj(att, ret, w_out[0].astype(BF16), b_out[0][None, :], xs, tm, 1024)

    ffn_w = ffn_norm_w[0][None, :]
    idx_t, gate_t = _router(h1, ffn_w, router_w[0].T, router_b[0][:, None], tm)
    gates = gate_t.T
    n_used, block_valid, block_exp, row_tok3, pos = _routing_metadata(idx_t.T, EXPERT_TM)

    xb, inv_rows = _gather_norm(n_used, row_tok3, h1_slabs, ffn_w, EXPERT_TM)
    act = _gateup(n_used, block_valid, block_exp, xb, inv_rows, w_gu[0], b_gu[0][:, None, :], EXPERT_TM)
    yb = _down(n_used, block_valid, block_exp, act, w_down[0], b_down[0][:, None, :], EXPERT_TM)

    tq = WINDOW
    pos3 = pos.reshape(s // tq, tq, TOP_K).transpose(0, 2, 1).reshape(s // tq, 1, TOP_K * tq)
    out = _combine(pos3, gates, h1, final_norm_w[None, :], yb, tq)
    return out[None]
```

```python
import functools

import jax
import jax.numpy as jnp
import numpy as np
from jax import lax
from jax.experimental import pallas as pl
from jax.experimental.pallas import tpu as pltpu

F32 = jnp.float32
BF16 = jnp.bfloat16

D_MODEL = 4096
N_META = 16
ATT_HEADS = 32
ATT_KV_HEADS = 4
ATT_HEAD_DIM = 64
ATT_GROUP = ATT_HEADS // ATT_KV_HEADS
WINDOW = 128
ROPE_THETA = 500000.0
ROPE_DIM = ATT_HEAD_DIM // 4
RET_HEADS = 8
RET_KEY_DIM = 128
RET_VALUE_DIM = 256
RET_CHUNK = 128
RET_ROPE_THETA = 10000.0
ATT_WIDTH = ATT_HEADS * ATT_HEAD_DIM
KV_WIDTH = ATT_KV_HEADS * ATT_HEAD_DIM
RET_QK_WIDTH = RET_HEADS * RET_KEY_DIM
RET_WIDTH = RET_HEADS * RET_VALUE_DIM
IN_WIDTH = ATT_WIDTH + 2 * KV_WIDTH + 2 * RET_QK_WIDTH + 2 * RET_WIDTH
OFF_AQ = 0
OFF_RQ = OFF_AQ + ATT_WIDTH
OFF_RK = OFF_RQ + RET_QK_WIDTH
OFF_RV = OFF_RK + RET_QK_WIDTH
OFF_RG = OFF_RV + RET_WIDTH
OFF_AK = OFF_RG + RET_WIDTH
OFF_AV = OFF_AK + KV_WIDTH
N_EXPERTS = 32
TOP_K = 4
D_EXPERT = D_MODEL // 2
SWIGLU_ALPHA = 1.702
SWIGLU_LIMIT = 7.0
NORM_EPS = 1e-5
GN_EPS = 1e-6

LANES = 128
VMEM_LIMIT = 56 * 1024 * 1024

IN_TN = 512
EXPERT_TM = 512
EXPERT_HALF = EXPERT_TM // 2
GU_TN = 512
DOWN_TN = 2048
KEYS_PAD = 3 * WINDOW
NEG_BIG = -1e30
SLAB_ROWS = D_MODEL // LANES
SLAB_PITCH = 40
GATHER_GROUP = 64
COMBINE_GROUP = 32


def _cparams(sem):
    return pltpu.CompilerParams(dimension_semantics=sem, vmem_limit_bytes=VMEM_LIMIT)


def _src_tile(j):
    assert 2 * KV_WIDTH == IN_TN and ATT_WIDTH % IN_TN == 0
    n_aq = ATT_WIDTH // IN_TN
    n_tiles = IN_WIDTH // IN_TN
    return jnp.where(j < n_aq, j, jnp.where(j < n_tiles - 1, j + 1, n_aq))


def _chunk_classes():
    cls = []
    for c in range(IN_WIDTH // LANES):
        off = c * LANES
        if off < OFF_RQ or OFF_AK <= off < OFF_AV:
            cls.append("a")
        elif OFF_RQ <= off < OFF_RV:
            cls.append("r")
        else:
            cls.append("n")
    return cls


def _inproj_kernel(x_ref, nw_ref, w_ref, b_ref, aa_ref, ab_ref, ac_ref, ra_ref, rb_ref, o_ref, hn_ref):
    j = pl.program_id(1)

    @pl.when(j == 0)
    def _():
        x = x_ref[...]
        ms = jnp.mean(x * x, axis=-1, keepdims=True)
        hn_ref[...] = (x * lax.rsqrt(ms + NORM_EPS) * nw_ref[...]).astype(BF16)

    y = jnp.dot(hn_ref[...], w_ref[...], preferred_element_type=F32) + b_ref[...]

    cpt = IN_TN // LANES
    classes = _chunk_classes()
    n_tiles = IN_WIDTH // IN_TN
    patterns = {}
    for t in range(n_tiles):
        patterns.setdefault(tuple(classes[t * cpt:(t + 1) * cpt]), []).append(t)

    def emit(pattern):
        for c, kind in enumerate(pattern):
            yc = y[:, c * LANES:(c + 1) * LANES]
            if kind == "a":
                yc = (yc * aa_ref[...] + pltpu.roll(yc, LANES - ROPE_DIM // 2, 1) * ab_ref[...]
                      + pltpu.roll(yc, ROPE_DIM // 2, 1) * ac_ref[...])
            elif kind == "r":
                yc = yc * ra_ref[...] + pltpu.roll(yc, RET_KEY_DIM // 2, 1) * rb_ref[...]
            o_ref[:, c * LANES:(c + 1) * LANES] = yc.astype(o_ref.dtype)

    for pattern, tiles in patterns.items():
        cond = functools.reduce(jnp.logical_or, [j == t for t in tiles])
        pl.when(cond)(functools.partial(emit, pattern))


def _inproj(x, norm_w, w_bf, b, tabs, tm):
    m = x.shape[0]
    aa, ab, ac, ra, rb = tabs
    row = lambda i, j: (i, 0)
    tab_spec = pl.BlockSpec((tm, LANES), row)
    return pl.pallas_call(
        _inproj_kernel,
        grid=(m // tm, IN_WIDTH // IN_TN),
        in_specs=[
            pl.BlockSpec((tm, D_MODEL), row),
            pl.BlockSpec((1, D_MODEL), lambda i, j: (0, 0)),
            pl.BlockSpec((D_MODEL, IN_TN), lambda i, j: (0, _src_tile(j))),
            pl.BlockSpec((1, IN_TN), lambda i, j: (0, _src_tile(j))),
            tab_spec, tab_spec, tab_spec, tab_spec, tab_spec,
        ],
        out_specs=pl.BlockSpec((tm, IN_TN), lambda i, j: (i, j)),
        out_shape=jax.ShapeDtypeStruct((m, IN_WIDTH), BF16),
        scratch_shapes=[pltpu.VMEM((tm, D_MODEL), BF16)],
        compiler_params=_cparams(("parallel", "arbitrary")),
        name="inproj",
    )(x, norm_w, w_bf, b, aa, ab, ac, ra, rb)


def _rope_tables(first_pos, n_pos):
    pos = (first_pos + jnp.arange(n_pos, dtype=jnp.int32)).astype(F32)[:, None]
    lane = np.arange(LANES)
    half = ROPE_DIM // 2
    inv = ROPE_THETA ** (-jnp.arange(half, dtype=F32) / half)
    ang = pos * inv[None, :]
    cos, sin = jnp.cos(ang), jnp.sin(ang)
    c = lane % ATT_HEAD_DIM
    f = c % half
    is_lo = jnp.asarray(c < half)[None, :]
    is_hi = jnp.asarray((c >= half) & (c < ROPE_DIM))[None, :]
    cos_l, sin_l = cos[:, f], sin[:, f]
    aa = jnp.where(is_lo | is_hi, cos_l, 1.0)
    ab = jnp.where(is_lo, -sin_l, 0.0)
    ac = jnp.where(is_hi, sin_l, 0.0)
    rhalf = RET_KEY_DIM // 2
    rinv = RET_ROPE_THETA ** (-jnp.arange(rhalf, dtype=F32) / rhalf)
    rang = pos * rinv[None, :]
    rcos, rsin = jnp.cos(rang), jnp.sin(rang)
    rf = lane % rhalf
    ra = rcos[:, rf]
    rb = jnp.where(jnp.asarray(lane < rhalf)[None, :], -rsin[:, rf], rsin[:, rf])
    return aa, ab, ac, ra, rb


def _attn_kernel(sink_ref, q_ref, ko_ref, vo_ref, kp_ref, vp_ref, km_ref, vm_ref, o_ref):
    n = pl.program_id(0)
    w = WINDOW
    qi = lax.broadcasted_iota(jnp.int32, (w, KEYS_PAD), 0)
    ji = lax.broadcasted_iota(jnp.int32, (w, KEYS_PAD), 1)
    vis_prev = (ji < w) & (ji > qi) & (n > 0)
    vis_own = (ji >= w) & (ji < 2 * w) & (ji - w <= qi)
    vis_meta = (ji >= 2 * w) & (ji < 2 * w + N_META)
    mask = vis_prev | vis_own | vis_meta
    lane = lax.broadcasted_iota(jnp.int32, (KEYS_PAD, LANES), 1)
    lo_half = lane < ATT_HEAD_DIM
    olane = lax.broadcasted_iota(jnp.int32, (w, LANES), 1) < ATT_HEAD_DIM
    scale = ATT_HEAD_DIM ** -0.5

    def spread(prev_ref, own_ref, meta_ref, g):
        c0 = (g // 2) * LANES
        chunk = jnp.concatenate([prev_ref[:, c0:c0 + LANES], own_ref[:, c0:c0 + LANES],
                                 meta_ref[:, c0:c0 + LANES]], axis=0).astype(F32)
        swapped = pltpu.roll(chunk, ATT_HEAD_DIM, 1)
        in_lo, in_hi = (chunk, swapped) if g % 2 == 0 else (swapped, chunk)
        even = jnp.where(lo_half, in_lo, 0.0)
        odd = jnp.where(lo_half, 0.0, in_hi)
        return jnp.concatenate([even, odd], axis=0).astype(BF16)

    pairs = ATT_GROUP // 2
    for g in range(ATT_KV_HEADS):
        kz = spread(kp_ref, ko_ref, km_ref, g)
        vz = spread(vp_ref, vo_ref, vm_ref, g)
        c0 = g * pairs
        q4 = jnp.concatenate([q_ref[:, (c0 + r) * LANES:(c0 + r + 1) * LANES] for r in range(pairs)], axis=0)
        s4 = lax.dot_general(q4, kz, (((1,), (1,)), ((), ())), preferred_element_type=F32) * scale
        ps, dens = [], []
        for r in range(pairs):
            prow, drow = [], []
            for half in range(2):
                sink = sink_ref[2 * (c0 + r) + half]
                s = jnp.where(mask, s4[r * w:(r + 1) * w, half * KEYS_PAD:(half + 1) * KEYS_PAD], NEG_BIG)
                m = jnp.maximum(jnp.max(s, axis=-1, keepdims=True), sink)
                p = jnp.exp(s - m)
                drow.append(jnp.sum(p, axis=-1, keepdims=True) + jnp.exp(sink - m))
                prow.append(p.astype(BF16))
            ps.append(jnp.concatenate(prow, axis=1))
            dens.append(jnp.where(olane, drow[0], drow[1]))
        o4 = jnp.dot(jnp.concatenate(ps, axis=0), vz, preferred_element_type=F32)
        for r in range(pairs):
            o = o4[r * w:(r + 1) * w] / dens[r]
            o_ref[:, (c0 + r) * LANES:(c0 + r + 1) * LANES] = o.astype(o_ref.dtype)


def _attention(z, zm, sinks):
    s = z.shape[0]
    nb = s // WINDOW
    kcol, vcol = OFF_AK // KV_WIDTH, OFF_AV // KV_WIDTH
    prev = lambda n: jnp.maximum(n - 1, 0)
    return pl.pallas_call(
        _attn_kernel,
        grid=(nb,),
        in_specs=[
            pl.BlockSpec(memory_space=pltpu.SMEM),
            pl.BlockSpec((WINDOW, ATT_WIDTH), lambda n: (n, 0)),
            pl.BlockSpec((WINDOW, KV_WIDTH), lambda n: (n, kcol)),
            pl.BlockSpec((WINDOW, KV_WIDTH), lambda n: (n, vcol)),
            pl.BlockSpec((WINDOW, KV_WIDTH), lambda n: (prev(n), kcol)),
            pl.BlockSpec((WINDOW, KV_WIDTH), lambda n: (prev(n), vcol)),
            pl.BlockSpec((WINDOW, KV_WIDTH), lambda n: (0, kcol)),
            pl.BlockSpec((WINDOW, KV_WIDTH), lambda n: (0, vcol)),
        ],
        out_specs=pl.BlockSpec((WINDOW, ATT_WIDTH), lambda n: (n, 0)),
        out_shape=jax.ShapeDtypeStruct((s, ATT_WIDTH), BF16),
        compiler_params=_cparams(("parallel",)),
        name="swa_attention",
    )(sinks, z, z, z, z, z, zm, zm)


def _ret_kernel(gc_ref, q_ref, k_ref, v_ref, g_ref, km_ref, vm_ref, dm_ref, kd_ref, qd_ref, wm_ref, nw_ref,
                o_ref, u_ref):
    c = pl.program_id(0)
    dk, dv = RET_KEY_DIM, RET_VALUE_DIM

    @pl.when(c == 0)
    def _():
        for hh in range(RET_HEADS):
            kmw = (km_ref[:, hh * dk:(hh + 1) * dk].astype(F32) * wm_ref[hh]).T.astype(BF16)
            u_ref[hh] = jnp.dot(kmw, vm_ref[:, hh * dv:(hh + 1) * dv], preferred_element_type=F32)

    for hh in range(RET_HEADS):
        q = q_ref[:, hh * dk:(hh + 1) * dk]
        k = k_ref[:, hh * dk:(hh + 1) * dk]
        v = v_ref[:, hh * dv:(hh + 1) * dv]
        u = u_ref[hh]
        inner = lax.dot_general(q, k, (((1,), (1,)), ((), ())), preferred_element_type=F32) * dm_ref[hh]
        intra = jnp.dot(inner.astype(BF16), v, preferred_element_type=F32)
        cross = jnp.dot(q, u.astype(BF16), preferred_element_type=F32) * qd_ref[hh]
        y = intra + cross
        mu = jnp.mean(y, axis=-1, keepdims=True)
        yc = y - mu
        var = jnp.mean(yc * yc, axis=-1, keepdims=True)
        yn = yc * lax.rsqrt(var + GN_EPS)
        g = g_ref[:, hh * dv:(hh + 1) * dv].astype(F32)
        silu = g * (1.0 / (1.0 + jnp.exp(-g)))
        o_ref[:, hh * dv:(hh + 1) * dv] = (silu * yn * nw_ref[:, hh * dv:(hh + 1) * dv]).astype(o_ref.dtype)
        kdt = (k.astype(F32) * kd_ref[hh]).T.astype(BF16)
        u_ref[hh] = u * gc_ref[hh] + jnp.dot(kdt, v, preferred_element_type=F32)


def _ret_tables():
    scale = RET_KEY_DIM ** -0.5
    log_g = jnp.log1p(-(2.0 ** (-5.0 - jnp.arange(RET_HEADS, dtype=F32))))
    i = jnp.arange(RET_CHUNK, dtype=F32)
    diff = i[:, None] - i[None, :]
    dm = jnp.where(diff >= 0, jnp.exp(jnp.maximum(diff, 0.0)[None] * log_g[:, None, None]), 0.0) * scale
    qd = (jnp.exp((i + 1.0)[None, :] * log_g[:, None]) * scale)[:, :, None]
    kd = jnp.exp((RET_CHUNK - 1.0 - i)[None, :] * log_g[:, None])[:, :, None]
    jm = jnp.arange(RET_CHUNK, dtype=F32)
    wm = jnp.where(jm[None, :] < N_META, jnp.exp((N_META - 1 - jm)[None, :] * log_g[:, None]), 0.0)[:, :, None]
    gc = jnp.exp(RET_CHUNK * log_g)
    return gc, dm, kd, qd, wm


def _retention(z, zm, ret_norm_w):
    s = z.shape[0]
    nc = s // RET_CHUNK
    gc, dm, kd, qd, wm = _ret_tables()
    qc, kc = OFF_RQ // RET_QK_WIDTH, OFF_RK // RET_QK_WIDTH
    vc, gcol = OFF_RV // RET_WIDTH, OFF_RG // RET_WIDTH
    full3 = lambda c: (0, 0, 0)
    vec_spec = pl.BlockSpec((RET_HEADS, RET_CHUNK, 1), full3)
    return pl.pallas_call(
        _ret_kernel,
        grid=(nc,),
        in_specs=[
            pl.BlockSpec(memory_space=pltpu.SMEM),
            pl.BlockSpec((RET_CHUNK, RET_QK_WIDTH), lambda c: (c, qc)),
            pl.BlockSpec((RET_CHUNK, RET_QK_WIDTH), lambda c: (c, kc)),
            pl.BlockSpec((RET_CHUNK, RET_WIDTH), lambda c: (c, vc)),
            pl.BlockSpec((RET_CHUNK, RET_WIDTH), lambda c: (c, gcol)),
            pl.BlockSpec((RET_CHUNK, RET_QK_WIDTH), lambda c: (0, kc)),
            pl.BlockSpec((RET_CHUNK, RET_WIDTH), lambda c: (0, vc)),
            pl.BlockSpec((RET_HEADS, RET_CHUNK, RET_CHUNK), full3),
            vec_spec, vec_spec, vec_spec,
            pl.BlockSpec((1, RET_WIDTH), lambda c: (0, 0)),
        ],
        out_specs=pl.BlockSpec((RET_CHUNK, RET_WIDTH), lambda c: (c, 0)),
        out_shape=jax.ShapeDtypeStruct((s, RET_WIDTH), BF16),
        scratch_shapes=[pltpu.VMEM((RET_HEADS, RET_KEY_DIM, RET_VALUE_DIM), F32)],
        compiler_params=_cparams(("arbitrary",)),
        name="retention",
    )(gc, z, z, z, z, zm, zm, dm, kd, qd, wm, ret_norm_w)


def _store_slabs(slab_ref, rows, value):
    n = slab_ref.shape[1]
    flat = slab_ref.reshape(slab_ref.shape[0] * n, LANES)
    for k in range(n):
        flat[pl.ds(k, rows, stride=n), :] = value[:, k * LANES:(k + 1) * LANES]


def _outproj_kernel(a_ref, r_ref, wt_ref, wb_ref, b_ref, x_ref, o_ref, slab_ref):
    acc = jnp.dot(a_ref[...], wt_ref[...], preferred_element_type=F32)
    acc += jnp.dot(r_ref[...], wb_ref[...], preferred_element_type=F32)
    h = x_ref[...] + acc + b_ref[...]
    o_ref[...] = h
    _store_slabs(slab_ref, h.shape[0], h)


def _outproj(att, ret, w_bf, b, x, tm, tn):
    s = x.shape[0]
    return pl.pallas_call(
        _outproj_kernel,
        grid=(s // tm, D_MODEL // tn),
        in_specs=[
            pl.BlockSpec((tm, ATT_WIDTH), lambda i, j: (i, 0)),
            pl.BlockSpec((tm, RET_WIDTH), lambda i, j: (i, 0)),
            pl.BlockSpec((ATT_WIDTH, tn), lambda i, j: (0, j)),
            pl.BlockSpec((RET_WIDTH, tn), lambda i, j: (1, j)),
            pl.BlockSpec((1, tn), lambda i, j: (0, j)),
            pl.BlockSpec((tm, tn), lambda i, j: (i, j)),
        ],
        out_specs=[pl.BlockSpec((tm, tn), lambda i, j: (i, j)),
                   pl.BlockSpec((tm, tn // LANES, LANES), lambda i, j: (i, j, 0))],
        out_shape=[jax.ShapeDtypeStruct((s, D_MODEL), F32),
                   jax.ShapeDtypeStruct((s, D_MODEL // LANES, LANES), F32)],
        compiler_params=_cparams(("parallel", "arbitrary")),
        name="outproj",
    )(att, ret, w_bf, w_bf, b, x)


def _router_kernel(h_ref, nw_ref, rw_ref, rb_ref, idx_ref, gate_ref):
    x = h_ref[...]
    ms = jnp.mean(x * x, axis=-1, keepdims=True)
    hn = x * lax.rsqrt(ms + NORM_EPS) * nw_ref[...]
    logits = lax.dot_general(rw_ref[...], hn, (((1,), (1,)), ((), ())), preferred_element_type=F32,
                             precision=lax.Precision.HIGHEST) + rb_ref[...]
    eid = lax.broadcasted_iota(jnp.int32, logits.shape, 0)
    vals, idxs = [], []
    for _ in range(TOP_K):
        m = jnp.max(logits, axis=0, keepdims=True)
        sel = jnp.min(jnp.where(logits == m, eid, N_EXPERTS), axis=0, keepdims=True)
        vals.append(m)
        idxs.append(sel)
        logits = jnp.where(eid == sel, -jnp.inf, logits)
    es = [jnp.exp(v - vals[0]) for v in vals]
    tot = functools.reduce(lambda a, b: a + b, es)
    idx_ref[...] = jnp.concatenate(idxs, axis=0)
    gate_ref[...] = jnp.concatenate([e / tot for e in es], axis=0)


def _router(h1, norm_w, rw_t, rb, tm):
    s = h1.shape[0]
    return pl.pallas_call(
        _router_kernel,
        grid=(s // tm,),
        in_specs=[
            pl.BlockSpec((tm, D_MODEL), lambda i: (i, 0)),
            pl.BlockSpec((1, D_MODEL), lambda i: (0, 0)),
            pl.BlockSpec((N_EXPERTS, D_MODEL), lambda i: (0, 0)),
            pl.BlockSpec((N_EXPERTS, 1), lambda i: (0, 0)),
        ],
        out_specs=[pl.BlockSpec((TOP_K, tm), lambda i: (0, i)), pl.BlockSpec((TOP_K, tm), lambda i: (0, i))],
        out_shape=[jax.ShapeDtypeStruct((TOP_K, s), jnp.int32), jax.ShapeDtypeStruct((TOP_K, s), F32)],
        compiler_params=_cparams(("parallel",)),
        name="router",
    )(h1, norm_w, rw_t, rb)


def _slab_copy(src_hbm, idx_ref, buf_ref, sems, slot, r):
    return pltpu.make_async_copy(src_hbm.at[idx_ref[0, r]], buf_ref.at[slot, r, pl.ds(0, SLAB_ROWS), :],
                                 sems.at[slot])


def _slot_wait(src_hbm, buf_ref, sems, slot):
    n = buf_ref.shape[1]
    pltpu.make_async_copy(src_hbm.at[pl.ds(0, n)], buf_ref.at[slot, :, pl.ds(0, SLAB_ROWS), :],
                          sems.at[slot]).wait()


def _slab_chunk(buf_ref, slot, r0, rows, k):
    n_slots, slot_slabs = buf_ref.shape[:2]
    flat = buf_ref.reshape(n_slots * slot_slabs * SLAB_PITCH, LANES)
    base = (slot * slot_slabs + r0) * SLAB_PITCH + k
    return flat[pl.ds(base, rows, stride=SLAB_PITCH), :]


def _gather_norm_kernel(nused_ref, tok_ref, tokn_ref, h_hbm, nw_ref, o_ref, inv_ref, buf_ref, sems):
    b = pl.program_id(0)
    nu = nused_ref[0]
    tm = o_ref.shape[0]
    slot = b % 2
    grp = GATHER_GROUP
    n_groups = tm // grp

    def issue_block(idx_ref, s):
        def step(i, carry):
            r0 = i * grp
            for r in range(grp):
                _slab_copy(h_hbm, idx_ref, buf_ref, sems, s, r0 + r).start()
            return carry
        lax.fori_loop(0, n_groups, step, 0)

    @pl.when(b == 0)
    def _():
        issue_block(tok_ref, 0)

    @pl.when(b + 1 < nu)
    def _():
        issue_block(tokn_ref, 1 - slot)

    @pl.when(b < nu)
    def _():
        _slot_wait(h_hbm, buf_ref, sems, slot)

        def body(i, carry):
            r0 = pl.multiple_of(i * grp, grp)
            ssq = jnp.zeros((grp, LANES), F32)
            for k in range(SLAB_ROWS):
                x = _slab_chunk(buf_ref, slot, r0, grp, k)
                cols = slice(k * LANES, (k + 1) * LANES)
                o_ref[pl.ds(r0, grp), cols] = (x * nw_ref[:, cols]).astype(o_ref.dtype)
                ssq = ssq + x * x
            ms = jnp.sum(ssq, axis=-1, keepdims=True) * (1.0 / D_MODEL)
            inv_ref[pl.ds(r0, grp), :] = lax.rsqrt(ms + NORM_EPS)
            return carry
        lax.fori_loop(0, n_groups, body, 0)

    @pl.when(b >= nu)
    def _():
        o_ref[...] = jnp.zeros_like(o_ref)
        inv_ref[...] = jnp.zeros_like(inv_ref)


def _gather_norm(n_used, row_tok3, h1_slabs, norm_w, tm):
    nblk = row_tok3.shape[0]
    cur = lambda b, nu: (jnp.minimum(b, nu[0] - 1), 0, 0)
    nxt = lambda b, nu: (jnp.minimum(b + 1, nu[0] - 1), 0, 0)
    return pl.pallas_call(
        _gather_norm_kernel,
        grid_spec=pltpu.PrefetchScalarGridSpec(
            num_scalar_prefetch=1,
            grid=(nblk,),
            in_specs=[
                pl.BlockSpec((None, 1, tm), cur, memory_space=pltpu.SMEM),
                pl.BlockSpec((None, 1, tm), nxt, memory_space=pltpu.SMEM),
                pl.BlockSpec(memory_space=pl.ANY),
                pl.BlockSpec((1, D_MODEL), lambda b, nu: (0, 0)),
            ],
            out_specs=[pl.BlockSpec((tm, D_MODEL), lambda b, nu: (b, 0)),
                       pl.BlockSpec((tm, 1), lambda b, nu: (b, 0))],
            scratch_shapes=[pltpu.VMEM((2, tm, SLAB_PITCH, LANES), F32), pltpu.SemaphoreType.DMA((2,))],
        ),
        out_shape=[jax.ShapeDtypeStruct((nblk * tm, D_MODEL), BF16),
                   jax.ShapeDtypeStruct((nblk * tm, 1), F32)],
        compiler_params=_cparams(("arbitrary",)),
        name="gather_norm",
    )(n_used, row_tok3, row_tok3, h1_slabs, norm_w)


def _swiglu(x, inv, wg, wl, bg, bl):
    glu = jnp.dot(x, wg, preferred_element_type=F32) * inv + bg
    lin = jnp.dot(x, wl, preferred_element_type=F32) * inv + bl
    glu = jnp.minimum(glu, SWIGLU_LIMIT)
    lin = jnp.clip(lin, -SWIGLU_LIMIT, SWIGLU_LIMIT)
    return glu * (1.0 / (1.0 + jnp.exp(-SWIGLU_ALPHA * glu))) * (lin + 1.0)


def _expert_weight_stream(n_cols, bv_ref, be_ref, nx_ref, meta_ref, copies, convert):
    j = pl.program_id(0)
    b = pl.program_id(1)
    e = be_ref[b]
    first = (bv_ref[b] > 0) & ((b == 0) | (e != be_ref[jnp.maximum(b - 1, 0)]))

    @pl.when((j == 0) & (b == 0))
    def _():
        for c in copies(e, 0):
            c.start()

    @pl.when(first)
    def _():
        for c in copies(e, j):
            c.wait()
        convert()
        jn = jnp.where(e == meta_ref[1], j + 1, j)

        @pl.when(jn < n_cols)
        def _():
            for c in copies(nx_ref[b], jn):
                c.start()


def _gateup_kernel(bv_ref, be_ref, nx_ref, meta_ref, x_ref, inv_ref, w_hbm, bg_ref, bl_ref, o_ref, stg_ref,
                   wgb_ref, wlb_ref, sems):
    b = pl.program_id(1)
    valid = bv_ref[b]
    hm = EXPERT_HALF
    nj = D_EXPERT // GU_TN

    def copies(expert, jj):
        out = []
        for half in range(2):
            col = pl.multiple_of((jj + half * nj) * GU_TN, GU_TN)
            out.append(pltpu.make_async_copy(w_hbm.at[expert, :, pl.ds(col, GU_TN)], stg_ref.at[half],
                                             sems.at[half]))
        return out

    def convert():
        wgb_ref[...] = stg_ref[0].astype(BF16)
        wlb_ref[...] = stg_ref[1].astype(BF16)

    _expert_weight_stream(nj, bv_ref, be_ref, nx_ref, meta_ref, copies, convert)

    @pl.when(valid > hm)
    def _():
        act = _swiglu(x_ref[...], inv_ref[...], wgb_ref[...], wlb_ref[...], bg_ref[...], bl_ref[...])
        o_ref[...] = act.astype(o_ref.dtype)

    @pl.when((valid > 0) & (valid <= hm))
    def _():
        act = _swiglu(x_ref[:hm, :], inv_ref[:hm, :], wgb_ref[...], wlb_ref[...], bg_ref[...], bl_ref[...])
        o_ref[:hm, :] = act.astype(o_ref.dtype)
        o_ref[hm:, :] = jnp.zeros((o_ref.shape[0] - hm, o_ref.shape[1]), o_ref.dtype)

    @pl.when(valid == 0)
    def _():
        o_ref[...] = jnp.zeros_like(o_ref)


def _gateup(sched, xb, inv_rows, w_gu, b_gu3, tm):
    nblk = xb.shape[0] // tm
    nj = D_EXPERT // GU_TN
    last = lambda b, meta: jnp.minimum(b, meta[0] - 1)
    return pl.pallas_call(
        _gateup_kernel,
        grid_spec=pltpu.PrefetchScalarGridSpec(
            num_scalar_prefetch=4,
            grid=(nj, nblk),
            in_specs=[
                pl.BlockSpec((tm, D_MODEL), lambda j, b, bv, be, nx, meta: (last(b, meta), 0)),
                pl.BlockSpec((tm, 1), lambda j, b, bv, be, nx, meta: (last(b, meta), 0)),
                pl.BlockSpec(memory_space=pl.ANY),
                pl.BlockSpec((None, 1, GU_TN), lambda j, b, bv, be, nx, meta: (be[b], 0, j)),
                pl.BlockSpec((None, 1, GU_TN), lambda j, b, bv, be, nx, meta: (be[b], 0, j + nj)),
            ],
            out_specs=pl.BlockSpec((tm, GU_TN), lambda j, b, bv, be, nx, meta: (b, j)),
            scratch_shapes=[pltpu.VMEM((2, D_MODEL, GU_TN), F32), pltpu.VMEM((D_MODEL, GU_TN), BF16),
                            pltpu.VMEM((D_MODEL, GU_TN), BF16), pltpu.SemaphoreType.DMA((2,))],
        ),
        out_shape=jax.ShapeDtypeStruct((nblk * tm, D_EXPERT), BF16),
        compiler_params=_cparams(("arbitrary", "arbitrary")),
        name="expert_gateup",
    )(*sched, xb, inv_rows, w_gu, b_gu3, b_gu3)


def _down_kernel(bv_ref, be_ref, nx_ref, meta_ref, a_ref, w_hbm, bias_ref, o_ref, stg_ref, wb_ref, sems):
    b = pl.program_id(1)
    valid = bv_ref[b]
    hm = EXPERT_HALF

    def copies(expert, jj):
        col = pl.multiple_of(jj * DOWN_TN, DOWN_TN)
        return [pltpu.make_async_copy(w_hbm.at[expert, :, pl.ds(col, DOWN_TN)], stg_ref, sems.at[0])]

    def convert():
        wb_ref[...] = stg_ref[...].astype(BF16)

    _expert_weight_stream(D_MODEL // DOWN_TN, bv_ref, be_ref, nx_ref, meta_ref, copies, convert)

    @pl.when(valid > hm)
    def _():
        res = jnp.dot(a_ref[...], wb_ref[...], preferred_element_type=F32) + bias_ref[...]
        _store_slabs(o_ref, o_ref.shape[0], res)

    @pl.when((valid > 0) & (valid <= hm))
    def _():
        res = jnp.dot(a_ref[:hm, :], wb_ref[...], preferred_element_type=F32) + bias_ref[...]
        _store_slabs(o_ref, hm, res)
        o_ref[hm:] = jnp.zeros((o_ref.shape[0] - hm,) + o_ref.shape[1:], o_ref.dtype)

    @pl.when(valid == 0)
    def _():
        o_ref[...] = jnp.zeros_like(o_ref)


def _down(sched, act, w_down, b_down3, tm):
    nblk = act.shape[0] // tm
    nj = D_MODEL // DOWN_TN
    last = lambda b, meta: jnp.minimum(b, meta[0] - 1)
    return pl.pallas_call(
        _down_kernel,
        grid_spec=pltpu.PrefetchScalarGridSpec(
            num_scalar_prefetch=4,
            grid=(nj, nblk),
            in_specs=[
                pl.BlockSpec((tm, D_EXPERT), lambda j, b, bv, be, nx, meta: (last(b, meta), 0)),
                pl.BlockSpec(memory_space=pl.ANY),
                pl.BlockSpec((None, 1, DOWN_TN), lambda j, b, bv, be, nx, meta: (be[b], 0, j)),
            ],
            out_specs=pl.BlockSpec((tm, DOWN_TN // LANES, LANES), lambda j, b, bv, be, nx, meta: (b, j, 0)),
            scratch_shapes=[pltpu.VMEM((D_EXPERT, DOWN_TN), F32), pltpu.VMEM((D_EXPERT, DOWN_TN), BF16),
                            pltpu.SemaphoreType.DMA((1,))],
        ),
        out_shape=jax.ShapeDtypeStruct((nblk * tm, SLAB_ROWS, LANES), F32),
        compiler_params=_cparams(("arbitrary", "arbitrary")),
        name="expert_down",
    )(*sched, act, w_down, b_down3)


def _combine_kernel(pos_ref, posn_ref, g_ref, h_ref, nw_ref, y_hbm, o_ref, buf_ref, sems):
    i = pl.program_id(0)
    n_tiles = pl.num_programs(0)
    tq = h_ref.shape[0]
    slot = i % 2
    grp = COMBINE_GROUP
    n_groups = tq // grp

    def issue(idx_ref, s, r0):
        for k in range(TOP_K):
            for r in range(grp):
                _slab_copy(y_hbm, idx_ref, buf_ref, sems, s, k * tq + r0 + r).start()

    @pl.when(i == 0)
    def _():
        def first(t, carry):
            issue(pos_ref, 0, t * grp)
            return carry
        lax.fori_loop(0, n_groups, first, 0)

    _slot_wait(y_hbm, buf_ref, sems, slot)
    has_next = i + 1 < n_tiles

    def body(t, carry):
        r0 = pl.multiple_of(t * grp, grp)

        @pl.when(has_next)
        def _():
            issue(posn_ref, 1 - slot, r0)

        rows = pl.ds(r0, grp)
        g = g_ref[rows, :]
        gk = [jnp.broadcast_to(g[:, k:k + 1], (grp, LANES)) for k in range(TOP_K)]
        ssq = jnp.zeros((grp, LANES), F32)
        for c in range(SLAB_ROWS):
            cols = slice(c * LANES, (c + 1) * LANES)
            acc = h_ref[rows, cols]
            for k in range(TOP_K):
                acc = acc + gk[k] * _slab_chunk(buf_ref, slot, k * tq + r0, grp, c)
            o_ref[rows, cols] = acc
            ssq = ssq + acc * acc
        sc = lax.rsqrt(jnp.sum(ssq, axis=-1, keepdims=True) * (1.0 / D_MODEL) + NORM_EPS)
        o_ref[rows, :] = o_ref[rows, :] * sc * nw_ref[...]
        return carry
    lax.fori_loop(0, n_groups, body, 0)


def _combine(pos3, gates, h1, norm_w, yb, tq):
    s = h1.shape[0]
    nt = s // tq
    return pl.pallas_call(
        _combine_kernel,
        grid=(nt,),
        in_specs=[
            pl.BlockSpec((None, 1, TOP_K * tq), lambda i: (i, 0, 0), memory_space=pltpu.SMEM),
            pl.BlockSpec((None, 1, TOP_K * tq), lambda i: (jnp.minimum(i + 1, nt - 1), 0, 0),
                         memory_space=pltpu.SMEM),
            pl.BlockSpec((tq, TOP_K), lambda i: (i, 0)),
            pl.BlockSpec((tq, D_MODEL), lambda i: (i, 0)),
            pl.BlockSpec((1, D_MODEL), lambda i: (0, 0)),
            pl.BlockSpec(memory_space=pl.ANY),
        ],
        out_specs=pl.BlockSpec((tq, D_MODEL), lambda i: (i, 0)),
        out_shape=jax.ShapeDtypeStruct((s, D_MODEL), F32),
        scratch_shapes=[pltpu.VMEM((2, TOP_K * tq, SLAB_PITCH, LANES), F32), pltpu.SemaphoreType.DMA((2,))],
        compiler_params=_cparams(("arbitrary",)),
        name="combine_norm",
    )(pos3, pos3, gates, h1, norm_w, yb)


def _routing_metadata(top_idx, tm):
    s = top_idx.shape[0]
    n_assign = s * TOP_K
    nblk = -(-(n_assign + N_EXPERTS * (tm - 1)) // tm)
    e_flat = top_idx.reshape(n_assign)
    order = jnp.argsort(e_flat).astype(jnp.int32)
    rank = jnp.argsort(order).astype(jnp.int32)
    experts = jnp.arange(N_EXPERTS, dtype=jnp.int32)
    counts = jnp.sum((e_flat[:, None] == experts[None, :]).astype(jnp.int32), axis=0)
    padded = ((counts + tm - 1) // tm) * tm
    start_sorted = jnp.cumsum(counts) - counts
    start_pad = jnp.cumsum(padded) - padded
    pos = start_pad[e_flat] + rank - start_sorted[e_flat]
    n_used = jnp.sum(padded) // tm
    blocks = jnp.arange(nblk, dtype=jnp.int32)
    block_exp = jnp.sum((start_pad[None, :] <= (blocks * tm)[:, None]).astype(jnp.int32), axis=1) - 1
    block_exp = block_exp[jnp.minimum(blocks, n_used - 1)]
    block_valid = jnp.clip(counts[block_exp] - (blocks * tm - start_pad[block_exp]), 0, tm)
    block_valid = jnp.where(blocks < n_used, block_valid, 0)
    rows = jnp.arange(nblk * tm, dtype=jnp.int32)
    row_exp = jnp.broadcast_to(block_exp[:, None], (nblk, tm)).reshape(nblk * tm)
    off = rows - start_pad[row_exp]
    is_row = (off < counts[row_exp]) & (rows < n_used * tm)
    src = jnp.clip(start_sorted[row_exp] + off, 0, n_assign - 1)
    row_tok = jnp.where(is_row, order[src] // TOP_K, 0)
    present = counts > 0
    nxt = experts[None, :] > experts[:, None]
    next_present = jnp.min(jnp.where(nxt & present[None, :], experts[None, :], N_EXPERTS), axis=1)
    first_present = jnp.min(jnp.where(present, experts, N_EXPERTS))
    last_present = jnp.max(jnp.where(present, experts, -1))
    next_present = jnp.where(next_present == N_EXPERTS, first_present, next_present)
    block_next = next_present[block_exp]
    as_i32 = lambda t: t.astype(jnp.int32)
    meta = jnp.stack([as_i32(n_used), as_i32(last_present)])
    sched = (as_i32(block_valid), as_i32(block_exp), as_i32(block_next), meta)
    return (as_i32(n_used).reshape(1), sched, as_i32(row_tok).reshape(nblk, 1, tm),
            as_i32(pos).reshape(s, TOP_K))


def _pick(n, pref):
    t = pref
    while n % t:
        t //= 2
    return t


def kernel(x, meta_tokens, attn_norm_w, w_in, b_in, attn_sinks, ret_norm_w, w_out, b_out, ffn_norm_w, router_w,
           router_b, w_gu, b_gu, w_down, b_down, final_norm_w):
    assert x.shape[0] == 1 and attn_norm_w.shape[0] == 1
    xs = x[0]
    s = xs.shape[0]
    assert s % WINDOW == 0
    tm = _pick(s, 512)

    w_in_bf = w_in[0].astype(BF16)
    b_in2 = b_in[0][None, :]
    nw = attn_norm_w[0][None, :]
    z = _inproj(xs, nw, w_in_bf, b_in2, _rope_tables(N_META, s), tm)
    zm = _inproj(meta_tokens, nw, w_in_bf, b_in2, _rope_tables(0, N_META), N_META)
    zm = jnp.pad(zm, ((0, WINDOW - N_META), (0, 0)))

    att = _attention(z, zm, attn_sinks[0])
    ret = _retention(z, zm, ret_norm_w[0][None, :])
    h1, h1_slabs = _outproj(att, ret, w_out[0].astype(BF16), b_out[0][None, :], xs, tm, 1024)

    ffn_w = ffn_norm_w[0][None, :]
    idx_t, gate_t = _router(h1, ffn_w, router_w[0].T, router_b[0][:, None], tm)
    gates = gate_t.T
    n_used, sched, row_tok3, pos = _routing_metadata(idx_t.T, EXPERT_TM)

    xb, inv_rows = _gather_norm(n_used, row_tok3, h1_slabs, ffn_w, EXPERT_TM)
    act = _gateup(sched, xb, inv_rows, w_gu[0], b_gu[0][:, None, :], EXPERT_TM)
    yb = _down(sched, act, w_down[0], b_down[0][:, None, :], EXPERT_TM)

    tq = WINDOW
    pos3 = pos.reshape(s // tq, tq, TOP_K).transpose(0, 2, 1).reshape(s // tq, 1, TOP_K * tq)
    out = _combine(pos3, gates, h1, final_norm_w[None, :], yb, tq)
    return out[None]
```

```python
import functools

import jax
import jax.numpy as jnp
import numpy as np
from jax import lax
from jax.experimental import pallas as pl
from jax.experimental.pallas import tpu as pltpu

F32 = jnp.float32
BF16 = jnp.bfloat16
U32 = jnp.uint32

D_MODEL = 4096
N_META = 16
ATT_HEADS = 32
ATT_KV_HEADS = 4
ATT_HEAD_DIM = 64
ATT_GROUP = ATT_HEADS // ATT_KV_HEADS
WINDOW = 128
ROPE_THETA = 500000.0
ROPE_DIM = ATT_HEAD_DIM // 4
RET_HEADS = 8
RET_KEY_DIM = 128
RET_VALUE_DIM = 256
RET_CHUNK = 128
RET_ROPE_THETA = 10000.0
ATT_WIDTH = ATT_HEADS * ATT_HEAD_DIM
KV_WIDTH = ATT_KV_HEADS * ATT_HEAD_DIM
RET_QK_WIDTH = RET_HEADS * RET_KEY_DIM
RET_WIDTH = RET_HEADS * RET_VALUE_DIM
IN_WIDTH = ATT_WIDTH + 2 * KV_WIDTH + 2 * RET_QK_WIDTH + 2 * RET_WIDTH
OFF_AQ = 0
OFF_RQ = OFF_AQ + ATT_WIDTH
OFF_RK = OFF_RQ + RET_QK_WIDTH
OFF_RV = OFF_RK + RET_QK_WIDTH
OFF_RG = OFF_RV + RET_WIDTH
OFF_AK = OFF_RG + RET_WIDTH
OFF_AV = OFF_AK + KV_WIDTH
N_EXPERTS = 32
TOP_K = 4
D_EXPERT = D_MODEL // 2
SWIGLU_ALPHA = 1.702
SWIGLU_LIMIT = 7.0
NORM_EPS = 1e-5
GN_EPS = 1e-6

LANES = 128
VMEM_LIMIT = 56 * 1024 * 1024

IN_TN = 512
EXPERT_TM = 512
EXPERT_HALF = EXPERT_TM // 2
GU_TN = 512
DOWN_TN = 2048
KEYS_PAD = 3 * WINDOW
NEG_BIG = -1e30
SLAB_ROWS = D_MODEL // LANES
SLAB_PITCH = 40
PACKED_SLAB_ROWS = SLAB_ROWS // 2
PACKED_SLAB_PITCH = 24
GATHER_GROUP = 64
COMBINE_GROUP = 32


def _cparams(sem):
    return pltpu.CompilerParams(dimension_semantics=sem, vmem_limit_bytes=VMEM_LIMIT)


def _src_tile(j):
    assert 2 * KV_WIDTH == IN_TN and ATT_WIDTH % IN_TN == 0
    n_aq = ATT_WIDTH // IN_TN
    n_tiles = IN_WIDTH // IN_TN
    return jnp.where(j < n_aq, j, jnp.where(j < n_tiles - 1, j + 1, n_aq))


def _chunk_classes():
    cls = []
    for c in range(IN_WIDTH // LANES):
        off = c * LANES
        if off < OFF_RQ or OFF_AK <= off < OFF_AV:
            cls.append("a")
        elif OFF_RQ <= off < OFF_RV:
            cls.append("r")
        else:
            cls.append("n")
    return cls


def _inproj_kernel(x_ref, nw_ref, w_ref, b_ref, aa_ref, ab_ref, ac_ref, ra_ref, rb_ref, o_ref, hn_ref):
    j = pl.program_id(1)

    @pl.when(j == 0)
    def _():
        x = x_ref[...]
        ms = jnp.mean(x * x, axis=-1, keepdims=True)
        hn_ref[...] = (x * lax.rsqrt(ms + NORM_EPS) * nw_ref[...]).astype(BF16)

    y = jnp.dot(hn_ref[...], w_ref[...], preferred_element_type=F32) + b_ref[...]

    cpt = IN_TN // LANES
    classes = _chunk_classes()
    n_tiles = IN_WIDTH // IN_TN
    patterns = {}
    for t in range(n_tiles):
        patterns.setdefault(tuple(classes[t * cpt:(t + 1) * cpt]), []).append(t)

    def emit(pattern):
        for c, kind in enumerate(pattern):
            yc = y[:, c * LANES:(c + 1) * LANES]
            if kind == "a":
                yc = (yc * aa_ref[...] + pltpu.roll(yc, LANES - ROPE_DIM // 2, 1) * ab_ref[...]
                      + pltpu.roll(yc, ROPE_DIM // 2, 1) * ac_ref[...])
            elif kind == "r":
                yc = yc * ra_ref[...] + pltpu.roll(yc, RET_KEY_DIM // 2, 1) * rb_ref[...]
            o_ref[:, c * LANES:(c + 1) * LANES] = yc.astype(o_ref.dtype)

    for pattern, tiles in patterns.items():
        cond = functools.reduce(jnp.logical_or, [j == t for t in tiles])
        pl.when(cond)(functools.partial(emit, pattern))


def _inproj(x, norm_w, w_bf, b, tabs, tm):
    m = x.shape[0]
    aa, ab, ac, ra, rb = tabs
    row = lambda i, j: (i, 0)
    tab_spec = pl.BlockSpec((tm, LANES), row)
    return pl.pallas_call(
        _inproj_kernel,
        grid=(m // tm, IN_WIDTH // IN_TN),
        in_specs=[
            pl.BlockSpec((tm, D_MODEL), row),
            pl.BlockSpec((1, D_MODEL), lambda i, j: (0, 0)),
            pl.BlockSpec((D_MODEL, IN_TN), lambda i, j: (0, _src_tile(j))),
            pl.BlockSpec((1, IN_TN), lambda i, j: (0, _src_tile(j))),
            tab_spec, tab_spec, tab_spec, tab_spec, tab_spec,
        ],
        out_specs=pl.BlockSpec((tm, IN_TN), lambda i, j: (i, j)),
        out_shape=jax.ShapeDtypeStruct((m, IN_WIDTH), BF16),
        scratch_shapes=[pltpu.VMEM((tm, D_MODEL), BF16)],
        compiler_params=_cparams(("parallel", "arbitrary")),
        name="inproj",
    )(x, norm_w, w_bf, b, aa, ab, ac, ra, rb)


def _rope_tables(first_pos, n_pos):
    pos = (first_pos + jnp.arange(n_pos, dtype=jnp.int32)).astype(F32)[:, None]
    lane = np.arange(LANES)
    half = ROPE_DIM // 2
    inv = ROPE_THETA ** (-jnp.arange(half, dtype=F32) / half)
    ang = pos * inv[None, :]
    cos, sin = jnp.cos(ang), jnp.sin(ang)
    c = lane % ATT_HEAD_DIM
    f = c % half
    is_lo = jnp.asarray(c < half)[None, :]
    is_hi = jnp.asarray((c >= half) & (c < ROPE_DIM))[None, :]
    cos_l, sin_l = cos[:, f], sin[:, f]
    aa = jnp.where(is_lo | is_hi, cos_l, 1.0)
    ab = jnp.where(is_lo, -sin_l, 0.0)
    ac = jnp.where(is_hi, sin_l, 0.0)
    rhalf = RET_KEY_DIM // 2
    rinv = RET_ROPE_THETA ** (-jnp.arange(rhalf, dtype=F32) / rhalf)
    rang = pos * rinv[None, :]
    rcos, rsin = jnp.cos(rang), jnp.sin(rang)
    rf = lane % rhalf
    ra = rcos[:, rf]
    rb = jnp.where(jnp.asarray(lane < rhalf)[None, :], -rsin[:, rf], rsin[:, rf])
    return aa, ab, ac, ra, rb


def _attn_kernel(sink_ref, q_ref, ko_ref, vo_ref, kp_ref, vp_ref, km_ref, vm_ref, o_ref):
    n = pl.program_id(0)
    w = WINDOW
    qi = lax.broadcasted_iota(jnp.int32, (w, KEYS_PAD), 0)
    ji = lax.broadcasted_iota(jnp.int32, (w, KEYS_PAD), 1)
    vis_prev = (ji < w) & (ji > qi) & (n > 0)
    vis_own = (ji >= w) & (ji < 2 * w) & (ji - w <= qi)
    vis_meta = (ji >= 2 * w) & (ji < 2 * w + N_META)
    mask = vis_prev | vis_own | vis_meta
    lane = lax.broadcasted_iota(jnp.int32, (KEYS_PAD, LANES), 1)
    lo_half = lane < ATT_HEAD_DIM
    olane = lax.broadcasted_iota(jnp.int32, (w, LANES), 1) < ATT_HEAD_DIM
    scale = ATT_HEAD_DIM ** -0.5

    def spread(prev_ref, own_ref, meta_ref, g):
        c0 = (g // 2) * LANES
        chunk = jnp.concatenate([prev_ref[:, c0:c0 + LANES], own_ref[:, c0:c0 + LANES],
                                 meta_ref[:, c0:c0 + LANES]], axis=0).astype(F32)
        swapped = pltpu.roll(chunk, ATT_HEAD_DIM, 1)
        in_lo, in_hi = (chunk, swapped) if g % 2 == 0 else (swapped, chunk)
        even = jnp.where(lo_half, in_lo, 0.0)
        odd = jnp.where(lo_half, 0.0, in_hi)
        return jnp.concatenate([even, odd], axis=0).astype(BF16)

    pairs = ATT_GROUP // 2
    for g in range(ATT_KV_HEADS):
        kz = spread(kp_ref, ko_ref, km_ref, g)
        vz = spread(vp_ref, vo_ref, vm_ref, g)
        c0 = g * pairs
        q4 = jnp.concatenate([q_ref[:, (c0 + r) * LANES:(c0 + r + 1) * LANES] for r in range(pairs)], axis=0)
        s4 = lax.dot_general(q4, kz, (((1,), (1,)), ((), ())), preferred_element_type=F32) * scale
        ps, dens = [], []
        for r in range(pairs):
            prow, drow = [], []
            for half in range(2):
                sink = sink_ref[2 * (c0 + r) + half]
                s = jnp.where(mask, s4[r * w:(r + 1) * w, half * KEYS_PAD:(half + 1) * KEYS_PAD], NEG_BIG)
                m = jnp.maximum(jnp.max(s, axis=-1, keepdims=True), sink)
                p = jnp.exp(s - m)
                drow.append(jnp.sum(p, axis=-1, keepdims=True) + jnp.exp(sink - m))
                prow.append(p.astype(BF16))
            ps.append(jnp.concatenate(prow, axis=1))
            dens.append(jnp.where(olane, drow[0], drow[1]))
        o4 = jnp.dot(jnp.concatenate(ps, axis=0), vz, preferred_element_type=F32)
        for r in range(pairs):
            o = o4[r * w:(r + 1) * w] / dens[r]
            o_ref[:, (c0 + r) * LANES:(c0 + r + 1) * LANES] = o.astype(o_ref.dtype)


def _attention(z, zm, sinks):
    s = z.shape[0]
    nb = s // WINDOW
    kcol, vcol = OFF_AK // KV_WIDTH, OFF_AV // KV_WIDTH
    prev = lambda n: jnp.maximum(n - 1, 0)
    return pl.pallas_call(
        _attn_kernel,
        grid=(nb,),
        in_specs=[
            pl.BlockSpec(memory_space=pltpu.SMEM),
            pl.BlockSpec((WINDOW, ATT_WIDTH), lambda n: (n, 0)),
            pl.BlockSpec((WINDOW, KV_WIDTH), lambda n: (n, kcol)),
            pl.BlockSpec((WINDOW, KV_WIDTH), lambda n: (n, vcol)),
            pl.BlockSpec((WINDOW, KV_WIDTH), lambda n: (prev(n), kcol)),
            pl.BlockSpec((WINDOW, KV_WIDTH), lambda n: (prev(n), vcol)),
            pl.BlockSpec((WINDOW, KV_WIDTH), lambda n: (0, kcol)),
            pl.BlockSpec((WINDOW, KV_WIDTH), lambda n: (0, vcol)),
        ],
        out_specs=pl.BlockSpec((WINDOW, ATT_WIDTH), lambda n: (n, 0)),
        out_shape=jax.ShapeDtypeStruct((s, ATT_WIDTH), BF16),
        compiler_params=_cparams(("parallel",)),
        name="swa_attention",
    )(sinks, z, z, z, z, z, zm, zm)


def _ret_kernel(gc_ref, q_ref, k_ref, v_ref, g_ref, km_ref, vm_ref, dm_ref, kd_ref, qd_ref, wm_ref, nw_ref,
                o_ref, u_ref):
    c = pl.program_id(0)
    dk, dv = RET_KEY_DIM, RET_VALUE_DIM

    @pl.when(c == 0)
    def _():
        for hh in range(RET_HEADS):
            kmw = (km_ref[:, hh * dk:(hh + 1) * dk].astype(F32) * wm_ref[hh]).T.astype(BF16)
            u_ref[hh] = jnp.dot(kmw, vm_ref[:, hh * dv:(hh + 1) * dv], preferred_element_type=F32)

    for hh in range(RET_HEADS):
        q = q_ref[:, hh * dk:(hh + 1) * dk]
        k = k_ref[:, hh * dk:(hh + 1) * dk]
        v = v_ref[:, hh * dv:(hh + 1) * dv]
        u = u_ref[hh]
        inner = lax.dot_general(q, k, (((1,), (1,)), ((), ())), preferred_element_type=F32) * dm_ref[hh]
        intra = jnp.dot(inner.astype(BF16), v, preferred_element_type=F32)
        cross = jnp.dot(q, u.astype(BF16), preferred_element_type=F32) * qd_ref[hh]
        y = intra + cross
        mu = jnp.mean(y, axis=-1, keepdims=True)
        yc = y - mu
        var = jnp.mean(yc * yc, axis=-1, keepdims=True)
        yn = yc * lax.rsqrt(var + GN_EPS)
        g = g_ref[:, hh * dv:(hh + 1) * dv].astype(F32)
        silu = g * (1.0 / (1.0 + jnp.exp(-g)))
        o_ref[:, hh * dv:(hh + 1) * dv] = (silu * yn * nw_ref[:, hh * dv:(hh + 1) * dv]).astype(o_ref.dtype)
        kdt = (k.astype(F32) * kd_ref[hh]).T.astype(BF16)
        u_ref[hh] = u * gc_ref[hh] + jnp.dot(kdt, v, preferred_element_type=F32)


def _ret_tables():
    scale = RET_KEY_DIM ** -0.5
    log_g = jnp.log1p(-(2.0 ** (-5.0 - jnp.arange(RET_HEADS, dtype=F32))))
    i = jnp.arange(RET_CHUNK, dtype=F32)
    diff = i[:, None] - i[None, :]
    dm = jnp.where(diff >= 0, jnp.exp(jnp.maximum(diff, 0.0)[None] * log_g[:, None, None]), 0.0) * scale
    qd = (jnp.exp((i + 1.0)[None, :] * log_g[:, None]) * scale)[:, :, None]
    kd = jnp.exp((RET_CHUNK - 1.0 - i)[None, :] * log_g[:, None])[:, :, None]
    jm = jnp.arange(RET_CHUNK, dtype=F32)
    wm = jnp.where(jm[None, :] < N_META, jnp.exp((N_META - 1 - jm)[None, :] * log_g[:, None]), 0.0)[:, :, None]
    gc = jnp.exp(RET_CHUNK * log_g)
    return gc, dm, kd, qd, wm


def _retention(z, zm, ret_norm_w):
    s = z.shape[0]
    nc = s // RET_CHUNK
    gc, dm, kd, qd, wm = _ret_tables()
    qc, kc = OFF_RQ // RET_QK_WIDTH, OFF_RK // RET_QK_WIDTH
    vc, gcol = OFF_RV // RET_WIDTH, OFF_RG // RET_WIDTH
    full3 = lambda c: (0, 0, 0)
    vec_spec = pl.BlockSpec((RET_HEADS, RET_CHUNK, 1), full3)
    return pl.pallas_call(
        _ret_kernel,
        grid=(nc,),
        in_specs=[
            pl.BlockSpec(memory_space=pltpu.SMEM),
            pl.BlockSpec((RET_CHUNK, RET_QK_WIDTH), lambda c: (c, qc)),
            pl.BlockSpec((RET_CHUNK, RET_QK_WIDTH), lambda c: (c, kc)),
            pl.BlockSpec((RET_CHUNK, RET_WIDTH), lambda c: (c, vc)),
            pl.BlockSpec((RET_CHUNK, RET_WIDTH), lambda c: (c, gcol)),
            pl.BlockSpec((RET_CHUNK, RET_QK_WIDTH), lambda c: (0, kc)),
            pl.BlockSpec((RET_CHUNK, RET_WIDTH), lambda c: (0, vc)),
            pl.BlockSpec((RET_HEADS, RET_CHUNK, RET_CHUNK), full3),
            vec_spec, vec_spec, vec_spec,
            pl.BlockSpec((1, RET_WIDTH), lambda c: (0, 0)),
        ],
        out_specs=pl.BlockSpec((RET_CHUNK, RET_WIDTH), lambda c: (c, 0)),
        out_shape=jax.ShapeDtypeStruct((s, RET_WIDTH), BF16),
        scratch_shapes=[pltpu.VMEM((RET_HEADS, RET_KEY_DIM, RET_VALUE_DIM), F32)],
        compiler_params=_cparams(("arbitrary",)),
        name="retention",
    )(gc, z, z, z, z, zm, zm, dm, kd, qd, wm, ret_norm_w)


def _store_slabs(slab_ref, rows, value):
    n = slab_ref.shape[1]
    flat = slab_ref.reshape(slab_ref.shape[0] * n, LANES)
    for k in range(n):
        flat[pl.ds(k, rows, stride=n), :] = value[:, k * LANES:(k + 1) * LANES]


def _outproj_kernel(a_ref, r_ref, wt_ref, wb_ref, b_ref, x_ref, o_ref, slab_ref):
    acc = jnp.dot(a_ref[...], wt_ref[...], preferred_element_type=F32)
    acc += jnp.dot(r_ref[...], wb_ref[...], preferred_element_type=F32)
    h = x_ref[...] + acc + b_ref[...]
    o_ref[...] = h
    _store_slabs(slab_ref, h.shape[0], h)


def _outproj(att, ret, w_bf, b, x, tm, tn):
    s = x.shape[0]
    return pl.pallas_call(
        _outproj_kernel,
        grid=(D_MODEL // tn, s // tm),
        in_specs=[
            pl.BlockSpec((tm, ATT_WIDTH), lambda j, i: (i, 0)),
            pl.BlockSpec((tm, RET_WIDTH), lambda j, i: (i, 0)),
            pl.BlockSpec((ATT_WIDTH, tn), lambda j, i: (0, j)),
            pl.BlockSpec((RET_WIDTH, tn), lambda j, i: (1, j)),
            pl.BlockSpec((1, tn), lambda j, i: (0, j)),
            pl.BlockSpec((tm, tn), lambda j, i: (i, j)),
        ],
        out_specs=[pl.BlockSpec((tm, tn), lambda j, i: (i, j)),
                   pl.BlockSpec((tm, tn // LANES, LANES), lambda j, i: (i, j, 0))],
        out_shape=[jax.ShapeDtypeStruct((s, D_MODEL), F32),
                   jax.ShapeDtypeStruct((s, D_MODEL // LANES, LANES), F32)],
        compiler_params=_cparams(("arbitrary", "arbitrary")),
        name="outproj",
    )(att, ret, w_bf, w_bf, b, x)


def _router_kernel(h_ref, nw_ref, rw_ref, rb_ref, idx_ref, gate_ref):
    x = h_ref[...]
    ms = jnp.mean(x * x, axis=-1, keepdims=True)
    hn = x * lax.rsqrt(ms + NORM_EPS) * nw_ref[...]
    rw = rw_ref[...]
    rw_hi = rw.astype(BF16)
    rw_lo = (rw - rw_hi.astype(F32)).astype(BF16)
    hn_hi = hn.astype(BF16)
    hn_lo = (hn - hn_hi.astype(F32)).astype(BF16)
    nt = (((1,), (1,)), ((), ()))
    logits = (lax.dot_general(rw_hi, hn_hi, nt, preferred_element_type=F32)
              + lax.dot_general(rw_hi, hn_lo, nt, preferred_element_type=F32)
              + lax.dot_general(rw_lo, hn_hi, nt, preferred_element_type=F32)) + rb_ref[...]
    eid = lax.broadcasted_iota(jnp.int32, logits.shape, 0)
    vals, idxs = [], []
    for _ in range(TOP_K):
        m = jnp.max(logits, axis=0, keepdims=True)
        sel = jnp.min(jnp.where(logits == m, eid, N_EXPERTS), axis=0, keepdims=True)
        vals.append(m)
        idxs.append(sel)
        logits = jnp.where(eid == sel, -jnp.inf, logits)
    es = [jnp.exp(v - vals[0]) for v in vals]
    tot = functools.reduce(lambda a, b: a + b, es)
    idx_ref[...] = jnp.concatenate(idxs, axis=0)
    gate_ref[...] = jnp.concatenate([e / tot for e in es], axis=0)


def _router(h1, norm_w, rw_t, rb, tm):
    s = h1.shape[0]
    return pl.pallas_call(
        _router_kernel,
        grid=(s // tm,),
        in_specs=[
            pl.BlockSpec((tm, D_MODEL), lambda i: (i, 0)),
            pl.BlockSpec((1, D_MODEL), lambda i: (0, 0)),
            pl.BlockSpec((N_EXPERTS, D_MODEL), lambda i: (0, 0)),
            pl.BlockSpec((N_EXPERTS, 1), lambda i: (0, 0)),
        ],
        out_specs=[pl.BlockSpec((TOP_K, tm), lambda i: (0, i)), pl.BlockSpec((TOP_K, tm), lambda i: (0, i))],
        out_shape=[jax.ShapeDtypeStruct((TOP_K, s), jnp.int32), jax.ShapeDtypeStruct((TOP_K, s), F32)],
        compiler_params=_cparams(("parallel",)),
        name="router",
    )(h1, norm_w, rw_t, rb)


def _slab_copy(src_hbm, idx_ref, buf_ref, sems, slot, r):
    return pltpu.make_async_copy(src_hbm.at[idx_ref[0, r]], buf_ref.at[slot, r, pl.ds(0, src_hbm.shape[1]), :],
                                 sems.at[slot])


def _slot_wait(src_hbm, buf_ref, sems, slot):
    n = buf_ref.shape[1]
    pltpu.make_async_copy(src_hbm.at[pl.ds(0, n)], buf_ref.at[slot, :, pl.ds(0, src_hbm.shape[1]), :],
                          sems.at[slot]).wait()


def _slab_chunk(buf_ref, slot, r0, rows, k):
    n_slots, slot_slabs, pitch = buf_ref.shape[:3]
    flat = buf_ref.reshape(n_slots * slot_slabs * pitch, LANES)
    base = (slot * slot_slabs + r0) * pitch + k
    return flat[pl.ds(base, rows, stride=pitch), :]


def _gather_norm_kernel(nused_ref, tok_ref, tokn_ref, h_hbm, nw_ref, o_ref, inv_ref, buf_ref, sems):
    b = pl.program_id(0)
    nu = nused_ref[0]
    tm = o_ref.shape[0]
    slot = b % 2
    grp = GATHER_GROUP
    n_groups = tm // grp

    def issue_block(idx_ref, s):
        def step(i, carry):
            r0 = i * grp
            for r in range(grp):
                _slab_copy(h_hbm, idx_ref, buf_ref, sems, s, r0 + r).start()
            return carry
        lax.fori_loop(0, n_groups, step, 0)

    @pl.when(b == 0)
    def _():
        issue_block(tok_ref, 0)

    @pl.when(b + 1 < nu)
    def _():
        issue_block(tokn_ref, 1 - slot)

    @pl.when(b < nu)
    def _():
        _slot_wait(h_hbm, buf_ref, sems, slot)

        def body(i, carry):
            r0 = pl.multiple_of(i * grp, grp)
            ssq = jnp.zeros((grp, LANES), F32)
            for k in range(SLAB_ROWS):
                x = _slab_chunk(buf_ref, slot, r0, grp, k)
                cols = slice(k * LANES, (k + 1) * LANES)
                o_ref[pl.ds(r0, grp), cols] = (x * nw_ref[:, cols]).astype(o_ref.dtype)
                ssq = ssq + x * x
            ms = jnp.sum(ssq, axis=-1, keepdims=True) * (1.0 / D_MODEL)
            inv_ref[pl.ds(r0, grp), :] = lax.rsqrt(ms + NORM_EPS)
            return carry
        lax.fori_loop(0, n_groups, body, 0)

    @pl.when(b >= nu)
    def _():
        o_ref[...] = jnp.zeros_like(o_ref)
        inv_ref[...] = jnp.zeros_like(inv_ref)


def _gather_norm(n_used, row_tok3, h1_slabs, norm_w, tm):
    nblk = row_tok3.shape[0]
    cur = lambda b, nu: (jnp.minimum(b, nu[0] - 1), 0, 0)
    nxt = lambda b, nu: (jnp.minimum(b + 1, nu[0] - 1), 0, 0)
    return pl.pallas_call(
        _gather_norm_kernel,
        grid_spec=pltpu.PrefetchScalarGridSpec(
            num_scalar_prefetch=1,
            grid=(nblk,),
            in_specs=[
                pl.BlockSpec((None, 1, tm), cur, memory_space=pltpu.SMEM),
                pl.BlockSpec((None, 1, tm), nxt, memory_space=pltpu.SMEM),
                pl.BlockSpec(memory_space=pl.ANY),
                pl.BlockSpec((1, D_MODEL), lambda b, nu: (0, 0)),
            ],
            out_specs=[pl.BlockSpec((tm, D_MODEL), lambda b, nu: (b, 0)),
                       pl.BlockSpec((tm, 1), lambda b, nu: (b, 0))],
            scratch_shapes=[pltpu.VMEM((2, tm, SLAB_PITCH, LANES), F32), pltpu.SemaphoreType.DMA((2,))],
        ),
        out_shape=[jax.ShapeDtypeStruct((nblk * tm, D_MODEL), BF16),
                   jax.ShapeDtypeStruct((nblk * tm, 1), F32)],
        compiler_params=_cparams(("arbitrary",)),
        name="gather_norm",
    )(n_used, row_tok3, row_tok3, h1_slabs, norm_w)


def _swiglu(x, inv, wg, wl, bg, bl):
    glu = jnp.dot(x, wg, preferred_element_type=F32) * inv + bg
    lin = jnp.dot(x, wl, preferred_element_type=F32) * inv + bl
    glu = jnp.minimum(glu, SWIGLU_LIMIT)
    lin = jnp.clip(lin, -SWIGLU_LIMIT, SWIGLU_LIMIT)
    return glu * (1.0 / (1.0 + jnp.exp(-SWIGLU_ALPHA * glu))) * (lin + 1.0)


def _expert_weight_stream(n_cols, bv_ref, be_ref, nx_ref, meta_ref, copies, convert):
    j = pl.program_id(0)
    b = pl.program_id(1)
    e = be_ref[b]
    first = (bv_ref[b] > 0) & ((b == 0) | (e != be_ref[jnp.maximum(b - 1, 0)]))

    @pl.when((j == 0) & (b == 0))
    def _():
        for c in copies(e, 0):
            c.start()

    @pl.when(first)
    def _():
        for c in copies(e, j):
            c.wait()
        convert()
        jn = jnp.where(e == meta_ref[1], j + 1, j)

        @pl.when(jn < n_cols)
        def _():
            for c in copies(nx_ref[b], jn):
                c.start()


def _gateup_kernel(bv_ref, be_ref, nx_ref, meta_ref, x_ref, inv_ref, w_hbm, bg_ref, bl_ref, o_ref, stg_ref,
                   wgb_ref, wlb_ref, sems):
    b = pl.program_id(1)
    valid = bv_ref[b]
    hm = EXPERT_HALF
    nj = D_EXPERT // GU_TN

    def copies(expert, jj):
        out = []
        for half in range(2):
            col = pl.multiple_of((jj + half * nj) * GU_TN, GU_TN)
            out.append(pltpu.make_async_copy(w_hbm.at[expert, :, pl.ds(col, GU_TN)], stg_ref.at[half],
                                             sems.at[half]))
        return out

    def convert():
        wgb_ref[...] = stg_ref[0].astype(BF16)
        wlb_ref[...] = stg_ref[1].astype(BF16)

    _expert_weight_stream(nj, bv_ref, be_ref, nx_ref, meta_ref, copies, convert)

    @pl.when(valid > hm)
    def _():
        act = _swiglu(x_ref[...], inv_ref[...], wgb_ref[...], wlb_ref[...], bg_ref[...], bl_ref[...])
        o_ref[...] = act.astype(o_ref.dtype)

    @pl.when((valid > 0) & (valid <= hm))
    def _():
        act = _swiglu(x_ref[:hm, :], inv_ref[:hm, :], wgb_ref[...], wlb_ref[...], bg_ref[...], bl_ref[...])
        o_ref[:hm, :] = act.astype(o_ref.dtype)
        o_ref[hm:, :] = jnp.zeros((o_ref.shape[0] - hm, o_ref.shape[1]), o_ref.dtype)

    @pl.when(valid == 0)
    def _():
        o_ref[...] = jnp.zeros_like(o_ref)


def _gateup(sched, xb, inv_rows, w_gu, b_gu3, tm):
    nblk = xb.shape[0] // tm
    nj = D_EXPERT // GU_TN
    last = lambda b, meta: jnp.minimum(b, meta[0] - 1)
    return pl.pallas_call(
        _gateup_kernel,
        grid_spec=pltpu.PrefetchScalarGridSpec(
            num_scalar_prefetch=4,
            grid=(nj, nblk),
            in_specs=[
                pl.BlockSpec((tm, D_MODEL), lambda j, b, bv, be, nx, meta: (last(b, meta), 0)),
                pl.BlockSpec((tm, 1), lambda j, b, bv, be, nx, meta: (last(b, meta), 0)),
                pl.BlockSpec(memory_space=pl.ANY),
                pl.BlockSpec((None, 1, GU_TN), lambda j, b, bv, be, nx, meta: (be[b], 0, j)),
                pl.BlockSpec((None, 1, GU_TN), lambda j, b, bv, be, nx, meta: (be[b], 0, j + nj)),
            ],
            out_specs=pl.BlockSpec((tm, GU_TN), lambda j, b, bv, be, nx, meta: (b, j)),
            scratch_shapes=[pltpu.VMEM((2, D_MODEL, GU_TN), F32), pltpu.VMEM((D_MODEL, GU_TN), BF16),
                            pltpu.VMEM((D_MODEL, GU_TN), BF16), pltpu.SemaphoreType.DMA((2,))],
        ),
        out_shape=jax.ShapeDtypeStruct((nblk * tm, D_EXPERT), BF16),
        compiler_params=_cparams(("arbitrary", "arbitrary")),
        name="expert_gateup",
    )(*sched, xb, inv_rows, w_gu, b_gu3, b_gu3)


def _down_kernel(bv_ref, be_ref, nx_ref, meta_ref, a_ref, w_hbm, bias_ref, o_ref, stg_ref, wb_ref, sems):
    b = pl.program_id(1)
    valid = bv_ref[b]
    hm = EXPERT_HALF

    def copies(expert, jj):
        col = pl.multiple_of(jj * DOWN_TN, DOWN_TN)
        return [pltpu.make_async_copy(w_hbm.at[expert, :, pl.ds(col, DOWN_TN)], stg_ref, sems.at[0])]

    def convert():
        wb_ref[...] = stg_ref[...].astype(BF16)

    _expert_weight_stream(D_MODEL // DOWN_TN, bv_ref, be_ref, nx_ref, meta_ref, copies, convert)

    @pl.when(valid > hm)
    def _():
        res = jnp.dot(a_ref[...], wb_ref[...], preferred_element_type=F32) + bias_ref[...]
        _store_slabs(o_ref, o_ref.shape[0], _pack_bf16_pairs(res))

    @pl.when((valid > 0) & (valid <= hm))
    def _():
        res = jnp.dot(a_ref[:hm, :], wb_ref[...], preferred_element_type=F32) + bias_ref[...]
        _store_slabs(o_ref, hm, _pack_bf16_pairs(res))
        o_ref[hm:] = jnp.zeros((o_ref.shape[0] - hm,) + o_ref.shape[1:], o_ref.dtype)

    @pl.when(valid == 0)
    def _():
        o_ref[...] = jnp.zeros_like(o_ref)


def _pack_bf16_pairs(t):
    n = t.shape[1] // 2
    hi = lax.bitcast_convert_type(t[:, :n].astype(BF16).astype(F32), U32)
    lo = lax.bitcast_convert_type(t[:, n:].astype(BF16).astype(F32), U32)
    return hi | (lo >> 16)


def _unpack_bf16_pairs(w):
    hi = lax.bitcast_convert_type(w & jnp.uint32(0xFFFF0000), F32)
    lo = lax.bitcast_convert_type(w << 16, F32)
    return hi, lo


def _down(sched, act, w_down, b_down3, tm):
    nblk = act.shape[0] // tm
    nj = D_MODEL // DOWN_TN
    last = lambda b, meta: jnp.minimum(b, meta[0] - 1)
    return pl.pallas_call(
        _down_kernel,
        grid_spec=pltpu.PrefetchScalarGridSpec(
            num_scalar_prefetch=4,
            grid=(nj, nblk),
            in_specs=[
                pl.BlockSpec((tm, D_EXPERT), lambda j, b, bv, be, nx, meta: (last(b, meta), 0)),
                pl.BlockSpec(memory_space=pl.ANY),
                pl.BlockSpec((None, 1, DOWN_TN), lambda j, b, bv, be, nx, meta: (be[b], 0, j)),
            ],
            out_specs=pl.BlockSpec((tm, DOWN_TN // (2 * LANES), LANES),
                                   lambda j, b, bv, be, nx, meta: (b, j, 0)),
            scratch_shapes=[pltpu.VMEM((D_EXPERT, DOWN_TN), F32), pltpu.VMEM((D_EXPERT, DOWN_TN), BF16),
                            pltpu.SemaphoreType.DMA((1,))],
        ),
        out_shape=jax.ShapeDtypeStruct((nblk * tm, PACKED_SLAB_ROWS, LANES), U32),
        compiler_params=_cparams(("arbitrary", "arbitrary")),
        name="expert_down",
    )(*sched, act, w_down, b_down3)


def _combine_kernel(pos_ref, posn_ref, g_ref, h_ref, nw_ref, y_hbm, o_ref, buf_ref, sems):
    i = pl.program_id(0)
    n_tiles = pl.num_programs(0)
    tq = h_ref.shape[0]
    slot = i % 2
    grp = COMBINE_GROUP
    n_groups = tq // grp

    def issue(idx_ref, s, r0):
        for k in range(TOP_K):
            for r in range(grp):
                _slab_copy(y_hbm, idx_ref, buf_ref, sems, s, k * tq + r0 + r).start()

    @pl.when(i == 0)
    def _():
        def first(t, carry):
            issue(pos_ref, 0, t * grp)
            return carry
        lax.fori_loop(0, n_groups, first, 0)

    _slot_wait(y_hbm, buf_ref, sems, slot)
    has_next = i + 1 < n_tiles

    def body(t, carry):
        r0 = pl.multiple_of(t * grp, grp)

        @pl.when(has_next)
        def _():
            issue(posn_ref, 1 - slot, r0)

        rows = pl.ds(r0, grp)
        g = g_ref[rows, :]
        gk = [jnp.broadcast_to(g[:, k:k + 1], (grp, LANES)) for k in range(TOP_K)]
        ssq = jnp.zeros((grp, LANES), F32)
        half = DOWN_TN // 2
        for c in range(PACKED_SLAB_ROWS):
            col_hi = (c * LANES // half) * DOWN_TN + (c * LANES) % half
            pairs = [_unpack_bf16_pairs(_slab_chunk(buf_ref, slot, k * tq + r0, grp, c)) for k in range(TOP_K)]
            for part, col in enumerate((col_hi, col_hi + half)):
                cols = slice(col, col + LANES)
                acc = h_ref[rows, cols]
                for k in range(TOP_K):
                    acc = acc + gk[k] * pairs[k][part]
                o_ref[rows, cols] = acc
                ssq = ssq + acc * acc
        sc = lax.rsqrt(jnp.sum(ssq, axis=-1, keepdims=True) * (1.0 / D_MODEL) + NORM_EPS)
        o_ref[rows, :] = o_ref[rows, :] * sc * nw_ref[...]
        return carry
    lax.fori_loop(0, n_groups, body, 0)


def _combine(pos3, gates, h1, norm_w, yb, tq):
    s = h1.shape[0]
    nt = s // tq
    return pl.pallas_call(
        _combine_kernel,
        grid=(nt,),
        in_specs=[
            pl.BlockSpec((None, 1, TOP_K * tq), lambda i: (i, 0, 0), memory_space=pltpu.SMEM),
            pl.BlockSpec((None, 1, TOP_K * tq), lambda i: (jnp.minimum(i + 1, nt - 1), 0, 0),
                         memory_space=pltpu.SMEM),
            pl.BlockSpec((tq, TOP_K), lambda i: (i, 0)),
            pl.BlockSpec((tq, D_MODEL), lambda i: (i, 0)),
            pl.BlockSpec((1, D_MODEL), lambda i: (0, 0)),
            pl.BlockSpec(memory_space=pl.ANY),
        ],
        out_specs=pl.BlockSpec((tq, D_MODEL), lambda i: (i, 0)),
        out_shape=jax.ShapeDtypeStruct((s, D_MODEL), F32),
        scratch_shapes=[pltpu.VMEM((2, TOP_K * tq, PACKED_SLAB_PITCH, LANES), U32),
                        pltpu.SemaphoreType.DMA((2,))],
        compiler_params=_cparams(("arbitrary",)),
        name="combine_norm",
    )(pos3, pos3, gates, h1, norm_w, yb)


def _routing_metadata(top_idx, tm):
    s = top_idx.shape[0]
    n_assign = s * TOP_K
    nblk = -(-(n_assign + N_EXPERTS * (tm - 1)) // tm)
    e_flat = top_idx.reshape(n_assign)
    order = jnp.argsort(e_flat).astype(jnp.int32)
    rank = jnp.argsort(order).astype(jnp.int32)
    experts = jnp.arange(N_EXPERTS, dtype=jnp.int32)
    counts = jnp.sum((e_flat[:, None] == experts[None, :]).astype(jnp.int32), axis=0)
    padded = ((counts + tm - 1) // tm) * tm
    start_sorted = jnp.cumsum(counts) - counts
    start_pad = jnp.cumsum(padded) - padded
    pos = start_pad[e_flat] + rank - start_sorted[e_flat]
    n_used = jnp.sum(padded) // tm
    blocks = jnp.arange(nblk, dtype=jnp.int32)
    block_exp = jnp.sum((start_pad[None, :] <= (blocks * tm)[:, None]).astype(jnp.int32), axis=1) - 1
    block_exp = block_exp[jnp.minimum(blocks, n_used - 1)]
    block_valid = jnp.clip(counts[block_exp] - (blocks * tm - start_pad[block_exp]), 0, tm)
    block_valid = jnp.where(blocks < n_used, block_valid, 0)
    rows = jnp.arange(nblk * tm, dtype=jnp.int32)
    row_exp = jnp.broadcast_to(block_exp[:, None], (nblk, tm)).reshape(nblk * tm)
    off = rows - start_pad[row_exp]
    is_row = (off < counts[row_exp]) & (rows < n_used * tm)
    src = jnp.clip(start_sorted[row_exp] + off, 0, n_assign - 1)
    row_tok = jnp.where(is_row, order[src] // TOP_K, 0)
    present = counts > 0
    nxt = experts[None, :] > experts[:, None]
    next_present = jnp.min(jnp.where(nxt & present[None, :], experts[None, :], N_EXPERTS), axis=1)
    first_present = jnp.min(jnp.where(present, experts, N_EXPERTS))
    last_present = jnp.max(jnp.where(present, experts, -1))
    next_present = jnp.where(next_present == N_EXPERTS, first_present, next_present)
    block_next = next_present[block_exp]
    as_i32 = lambda t: t.astype(jnp.int32)
    meta = jnp.stack([as_i32(n_used), as_i32(last_present)])
    sched = (as_i32(block_valid), as_i32(block_exp), as_i32(block_next), meta)
    return (as_i32(n_used).reshape(1), sched, as_i32(row_tok).reshape(nblk, 1, tm),
            as_i32(pos).reshape(s, TOP_K))


def _pick(n, pref):
    t = pref
    while n % t:
        t //= 2
    return t


def kernel(x, meta_tokens, attn_norm_w, w_in, b_in, attn_sinks, ret_norm_w, w_out, b_out, ffn_norm_w, router_w,
           router_b, w_gu, b_gu, w_down, b_down, final_norm_w):
    assert x.shape[0] == 1 and attn_norm_w.shape[0] == 1
    xs = x[0]
    s = xs.shape[0]
    assert s % WINDOW == 0
    tm = _pick(s, 512)

    w_in_bf = w_in[0].astype(BF16)
    b_in2 = b_in[0][None, :]
    nw = attn_norm_w[0][None, :]
    z = _inproj(xs, nw, w_in_bf, b_in2, _rope_tables(N_META, s), tm)
    zm = _inproj(meta_tokens, nw, w_in_bf, b_in2, _rope_tables(0, N_META), N_META)
    zm = jnp.pad(zm, ((0, WINDOW - N_META), (0, 0)))

    att = _attention(z, zm, attn_sinks[0])
    ret = _retention(z, zm, ret_norm_w[0][None, :])
    h1, h1_slabs = _outproj(att, ret, w_out[0].astype(BF16), b_out[0][None, :], xs, tm, 1024)

    ffn_w = ffn_norm_w[0][None, :]
    idx_t, gate_t = _router(h1, ffn_w, router_w[0].T, router_b[0][:, None], tm)
    gates = gate_t.T
    n_used, sched, row_tok3, pos = _routing_metadata(idx_t.T, EXPERT_TM)

    xb, inv_rows = _gather_norm(n_used, row_tok3, h1_slabs, ffn_w, EXPERT_TM)
    act = _gateup(sched, xb, inv_rows, w_gu[0], b_gu[0][:, None, :], EXPERT_TM)
    yb = _down(sched, act, w_down[0], b_down[0][:, None, :], EXPERT_TM)

    tq = WINDOW
    pos3 = pos.reshape(s // tq, tq, TOP_K).transpose(0, 2, 1).reshape(s // tq, 1, TOP_K * tq)
    out = _combine(pos3, gates, h1, final_norm_w[None, :], yb, tq)
    return out[None]
```

```python
import functools

import jax
import jax.numpy as jnp
import numpy as np
from jax import lax
from jax.experimental import pallas as pl
from jax.experimental.pallas import tpu as pltpu

F32 = jnp.float32
BF16 = jnp.bfloat16
U32 = jnp.uint32

D_MODEL = 4096
N_META = 16
ATT_HEADS = 32
ATT_KV_HEADS = 4
ATT_HEAD_DIM = 64
ATT_GROUP = ATT_HEADS // ATT_KV_HEADS
WINDOW = 128
ROPE_THETA = 500000.0
ROPE_DIM = ATT_HEAD_DIM // 4
RET_HEADS = 8
RET_KEY_DIM = 128
RET_VALUE_DIM = 256
RET_CHUNK = 128
RET_ROPE_THETA = 10000.0
ATT_WIDTH = ATT_HEADS * ATT_HEAD_DIM
KV_WIDTH = ATT_KV_HEADS * ATT_HEAD_DIM
RET_QK_WIDTH = RET_HEADS * RET_KEY_DIM
RET_WIDTH = RET_HEADS * RET_VALUE_DIM
IN_WIDTH = ATT_WIDTH + 2 * KV_WIDTH + 2 * RET_QK_WIDTH + 2 * RET_WIDTH
OFF_AQ = 0
OFF_RQ = OFF_AQ + ATT_WIDTH
OFF_RK = OFF_RQ + RET_QK_WIDTH
OFF_RV = OFF_RK + RET_QK_WIDTH
OFF_RG = OFF_RV + RET_WIDTH
OFF_AK = OFF_RG + RET_WIDTH
OFF_AV = OFF_AK + KV_WIDTH
N_EXPERTS = 32
TOP_K = 4
D_EXPERT = D_MODEL // 2
SWIGLU_ALPHA = 1.702
SWIGLU_LIMIT = 7.0
NORM_EPS = 1e-5
GN_EPS = 1e-6

LANES = 128
VMEM_LIMIT = 56 * 1024 * 1024

IN_TN = 512
EXPERT_TM = 512
EXPERT_HALF = EXPERT_TM // 2
GU_TN = 512
DOWN_TN = 2048
OUT_TN = 2048
KEYS_PAD = 3 * WINDOW
NEG_BIG = -1e30
SLAB_ROWS = D_MODEL // LANES
SLAB_PITCH = 40
PACKED_SLAB_ROWS = SLAB_ROWS // 2
PACKED_SLAB_PITCH = 24
GATHER_GROUP = 64
COMBINE_GROUP = 32


def _cparams(sem):
    return pltpu.CompilerParams(dimension_semantics=sem, vmem_limit_bytes=VMEM_LIMIT)


def _src_tile(j):
    assert 2 * KV_WIDTH == IN_TN and ATT_WIDTH % IN_TN == 0
    n_aq = ATT_WIDTH // IN_TN
    n_tiles = IN_WIDTH // IN_TN
    return jnp.where(j < n_aq, j, jnp.where(j < n_tiles - 1, j + 1, n_aq))


def _chunk_classes():
    cls = []
    for c in range(IN_WIDTH // LANES):
        off = c * LANES
        if off < OFF_RQ or OFF_AK <= off < OFF_AV:
            cls.append("a")
        elif OFF_RQ <= off < OFF_RV:
            cls.append("r")
        else:
            cls.append("n")
    return cls


def _inproj_kernel(x_ref, nw_ref, w_ref, b_ref, aa_ref, ab_ref, ac_ref, ra_ref, rb_ref, o_ref, hn_ref):
    j = pl.program_id(1)

    @pl.when(j == 0)
    def _():
        x = x_ref[...]
        ms = jnp.mean(x * x, axis=-1, keepdims=True)
        hn_ref[...] = (x * lax.rsqrt(ms + NORM_EPS) * nw_ref[...]).astype(BF16)

    y = jnp.dot(hn_ref[...], w_ref[...], preferred_element_type=F32) + b_ref[...]

    cpt = IN_TN // LANES
    classes = _chunk_classes()
    n_tiles = IN_WIDTH // IN_TN
    patterns = {}
    for t in range(n_tiles):
        patterns.setdefault(tuple(classes[t * cpt:(t + 1) * cpt]), []).append(t)

    def emit(pattern):
        for c, kind in enumerate(pattern):
            yc = y[:, c * LANES:(c + 1) * LANES]
            if kind == "a":
                yc = (yc * aa_ref[...] + pltpu.roll(yc, LANES - ROPE_DIM // 2, 1) * ab_ref[...]
                      + pltpu.roll(yc, ROPE_DIM // 2, 1) * ac_ref[...])
            elif kind == "r":
                yc = yc * ra_ref[...] + pltpu.roll(yc, RET_KEY_DIM // 2, 1) * rb_ref[...]
            o_ref[:, c * LANES:(c + 1) * LANES] = yc.astype(o_ref.dtype)

    for pattern, tiles in patterns.items():
        cond = functools.reduce(jnp.logical_or, [j == t for t in tiles])
        pl.when(cond)(functools.partial(emit, pattern))


def _inproj(x, norm_w, w_bf, b, tabs, tm):
    m = x.shape[0]
    aa, ab, ac, ra, rb = tabs
    row = lambda i, j: (i, 0)
    tab_spec = pl.BlockSpec((tm, LANES), row)
    return pl.pallas_call(
        _inproj_kernel,
        grid=(m // tm, IN_WIDTH // IN_TN),
        in_specs=[
            pl.BlockSpec((tm, D_MODEL), row),
            pl.BlockSpec((1, D_MODEL), lambda i, j: (0, 0)),
            pl.BlockSpec((D_MODEL, IN_TN), lambda i, j: (0, _src_tile(j))),
            pl.BlockSpec((1, IN_TN), lambda i, j: (0, _src_tile(j))),
            tab_spec, tab_spec, tab_spec, tab_spec, tab_spec,
        ],
        out_specs=pl.BlockSpec((tm, IN_TN), lambda i, j: (i, j)),
        out_shape=jax.ShapeDtypeStruct((m, IN_WIDTH), BF16),
        scratch_shapes=[pltpu.VMEM((tm, D_MODEL), BF16)],
        compiler_params=_cparams(("parallel", "arbitrary")),
        name="inproj",
    )(x, norm_w, w_bf, b, aa, ab, ac, ra, rb)


def _rope_tables(first_pos, n_pos):
    pos = (first_pos + jnp.arange(n_pos, dtype=jnp.int32)).astype(F32)[:, None]
    lane = np.arange(LANES)
    half = ROPE_DIM // 2
    inv = ROPE_THETA ** (-jnp.arange(half, dtype=F32) / half)
    ang = pos * inv[None, :]
    cos, sin = jnp.cos(ang), jnp.sin(ang)
    c = lane % ATT_HEAD_DIM
    f = c % half
    is_lo = jnp.asarray(c < half)[None, :]
    is_hi = jnp.asarray((c >= half) & (c < ROPE_DIM))[None, :]
    cos_l, sin_l = cos[:, f], sin[:, f]
    aa = jnp.where(is_lo | is_hi, cos_l, 1.0)
    ab = jnp.where(is_lo, -sin_l, 0.0)
    ac = jnp.where(is_hi, sin_l, 0.0)
    rhalf = RET_KEY_DIM // 2
    rinv = RET_ROPE_THETA ** (-jnp.arange(rhalf, dtype=F32) / rhalf)
    rang = pos * rinv[None, :]
    rcos, rsin = jnp.cos(rang), jnp.sin(rang)
    rf = lane % rhalf
    ra = rcos[:, rf]
    rb = jnp.where(jnp.asarray(lane < rhalf)[None, :], -rsin[:, rf], rsin[:, rf])
    return aa, ab, ac, ra, rb


def _attn_kernel(sink_ref, q_ref, ko_ref, vo_ref, kp_ref, vp_ref, km_ref, vm_ref, o_ref):
    n = pl.program_id(0)
    w = WINDOW
    qi = lax.broadcasted_iota(jnp.int32, (w, KEYS_PAD), 0)
    ji = lax.broadcasted_iota(jnp.int32, (w, KEYS_PAD), 1)
    vis_prev = (ji < w) & (ji > qi) & (n > 0)
    vis_own = (ji >= w) & (ji < 2 * w) & (ji - w <= qi)
    vis_meta = (ji >= 2 * w) & (ji < 2 * w + N_META)
    mask = vis_prev | vis_own | vis_meta
    lane = lax.broadcasted_iota(jnp.int32, (KEYS_PAD, LANES), 1)
    lo_half = lane < ATT_HEAD_DIM
    olane = lax.broadcasted_iota(jnp.int32, (w, LANES), 1) < ATT_HEAD_DIM
    scale = ATT_HEAD_DIM ** -0.5

    def spread(prev_ref, own_ref, meta_ref, g):
        c0 = (g // 2) * LANES
        chunk = jnp.concatenate([prev_ref[:, c0:c0 + LANES], own_ref[:, c0:c0 + LANES],
                                 meta_ref[:, c0:c0 + LANES]], axis=0).astype(F32)
        swapped = pltpu.roll(chunk, ATT_HEAD_DIM, 1)
        in_lo, in_hi = (chunk, swapped) if g % 2 == 0 else (swapped, chunk)
        even = jnp.where(lo_half, in_lo, 0.0)
        odd = jnp.where(lo_half, 0.0, in_hi)
        return jnp.concatenate([even, odd], axis=0).astype(BF16)

    pairs = ATT_GROUP // 2
    for g in range(ATT_KV_HEADS):
        kz = spread(kp_ref, ko_ref, km_ref, g)
        vz = spread(vp_ref, vo_ref, vm_ref, g)
        c0 = g * pairs
        q4 = jnp.concatenate([q_ref[:, (c0 + r) * LANES:(c0 + r + 1) * LANES] for r in range(pairs)], axis=0)
        s4 = lax.dot_general(q4, kz, (((1,), (1,)), ((), ())), preferred_element_type=F32) * scale
        ps, dens = [], []
        for r in range(pairs):
            prow, drow = [], []
            for half in range(2):
                sink = sink_ref[2 * (c0 + r) + half]
                s = jnp.where(mask, s4[r * w:(r + 1) * w, half * KEYS_PAD:(half + 1) * KEYS_PAD], NEG_BIG)
                m = jnp.maximum(jnp.max(s, axis=-1, keepdims=True), sink)
                p = jnp.exp(s - m)
                drow.append(jnp.sum(p, axis=-1, keepdims=True) + jnp.exp(sink - m))
                prow.append(p.astype(BF16))
            ps.append(jnp.concatenate(prow, axis=1))
            dens.append(jnp.where(olane, drow[0], drow[1]))
        o4 = jnp.dot(jnp.concatenate(ps, axis=0), vz, preferred_element_type=F32)
        for r in range(pairs):
            o = o4[r * w:(r + 1) * w] / dens[r]
            o_ref[:, (c0 + r) * LANES:(c0 + r + 1) * LANES] = o.astype(o_ref.dtype)


def _attention(z, zm, sinks):
    s = z.shape[0]
    nb = s // WINDOW
    kcol, vcol = OFF_AK // KV_WIDTH, OFF_AV // KV_WIDTH
    prev = lambda n: jnp.maximum(n - 1, 0)
    return pl.pallas_call(
        _attn_kernel,
        grid=(nb,),
        in_specs=[
            pl.BlockSpec(memory_space=pltpu.SMEM),
            pl.BlockSpec((WINDOW, ATT_WIDTH), lambda n: (n, 0)),
            pl.BlockSpec((WINDOW, KV_WIDTH), lambda n: (n, kcol)),
            pl.BlockSpec((WINDOW, KV_WIDTH), lambda n: (n, vcol)),
            pl.BlockSpec((WINDOW, KV_WIDTH), lambda n: (prev(n), kcol)),
            pl.BlockSpec((WINDOW, KV_WIDTH), lambda n: (prev(n), vcol)),
            pl.BlockSpec((WINDOW, KV_WIDTH), lambda n: (0, kcol)),
            pl.BlockSpec((WINDOW, KV_WIDTH), lambda n: (0, vcol)),
        ],
        out_specs=pl.BlockSpec((WINDOW, ATT_WIDTH), lambda n: (n, 0)),
        out_shape=jax.ShapeDtypeStruct((s, ATT_WIDTH), BF16),
        compiler_params=_cparams(("parallel",)),
        name="swa_attention",
    )(sinks, z, z, z, z, z, zm, zm)


def _ret_kernel(gc_ref, q_ref, k_ref, v_ref, g_ref, km_ref, vm_ref, dm_ref, kd_ref, qd_ref, wm_ref, nw_ref,
                o_ref, u_ref):
    c = pl.program_id(0)
    dk, dv = RET_KEY_DIM, RET_VALUE_DIM

    @pl.when(c == 0)
    def _():
        for hh in range(RET_HEADS):
            kmw = (km_ref[:, hh * dk:(hh + 1) * dk].astype(F32) * wm_ref[hh]).T.astype(BF16)
            u_ref[hh] = jnp.dot(kmw, vm_ref[:, hh * dv:(hh + 1) * dv], preferred_element_type=F32)

    for hh in range(RET_HEADS):
        q = q_ref[:, hh * dk:(hh + 1) * dk]
        k = k_ref[:, hh * dk:(hh + 1) * dk]
        v = v_ref[:, hh * dv:(hh + 1) * dv]
        u = u_ref[hh]
        inner = lax.dot_general(q, k, (((1,), (1,)), ((), ())), preferred_element_type=F32) * dm_ref[hh]
        intra = jnp.dot(inner.astype(BF16), v, preferred_element_type=F32)
        cross = jnp.dot(q, u.astype(BF16), preferred_element_type=F32) * qd_ref[hh]
        y = intra + cross
        mu = jnp.mean(y, axis=-1, keepdims=True)
        yc = y - mu
        var = jnp.mean(yc * yc, axis=-1, keepdims=True)
        yn = yc * lax.rsqrt(var + GN_EPS)
        g = g_ref[:, hh * dv:(hh + 1) * dv].astype(F32)
        silu = g * (1.0 / (1.0 + jnp.exp(-g)))
        o_ref[:, hh * dv:(hh + 1) * dv] = (silu * yn * nw_ref[:, hh * dv:(hh + 1) * dv]).astype(o_ref.dtype)
        kdt = (k.astype(F32) * kd_ref[hh]).T.astype(BF16)
        u_ref[hh] = u * gc_ref[hh] + jnp.dot(kdt, v, preferred_element_type=F32)


def _ret_tables():
    scale = RET_KEY_DIM ** -0.5
    log_g = jnp.log1p(-(2.0 ** (-5.0 - jnp.arange(RET_HEADS, dtype=F32))))
    i = jnp.arange(RET_CHUNK, dtype=F32)
    diff = i[:, None] - i[None, :]
    dm = jnp.where(diff >= 0, jnp.exp(jnp.maximum(diff, 0.0)[None] * log_g[:, None, None]), 0.0) * scale
    qd = (jnp.exp((i + 1.0)[None, :] * log_g[:, None]) * scale)[:, :, None]
    kd = jnp.exp((RET_CHUNK - 1.0 - i)[None, :] * log_g[:, None])[:, :, None]
    jm = jnp.arange(RET_CHUNK, dtype=F32)
    wm = jnp.where(jm[None, :] < N_META, jnp.exp((N_META - 1 - jm)[None, :] * log_g[:, None]), 0.0)[:, :, None]
    gc = jnp.exp(RET_CHUNK * log_g)
    return gc, dm, kd, qd, wm


def _retention(z, zm, ret_norm_w):
    s = z.shape[0]
    nc = s // RET_CHUNK
    gc, dm, kd, qd, wm = _ret_tables()
    qc, kc = OFF_RQ // RET_QK_WIDTH, OFF_RK // RET_QK_WIDTH
    vc, gcol = OFF_RV // RET_WIDTH, OFF_RG // RET_WIDTH
    full3 = lambda c: (0, 0, 0)
    vec_spec = pl.BlockSpec((RET_HEADS, RET_CHUNK, 1), full3)
    return pl.pallas_call(
        _ret_kernel,
        grid=(nc,),
        in_specs=[
            pl.BlockSpec(memory_space=pltpu.SMEM),
            pl.BlockSpec((RET_CHUNK, RET_QK_WIDTH), lambda c: (c, qc)),
            pl.BlockSpec((RET_CHUNK, RET_QK_WIDTH), lambda c: (c, kc)),
            pl.BlockSpec((RET_CHUNK, RET_WIDTH), lambda c: (c, vc)),
            pl.BlockSpec((RET_CHUNK, RET_WIDTH), lambda c: (c, gcol)),
            pl.BlockSpec((RET_CHUNK, RET_QK_WIDTH), lambda c: (0, kc)),
            pl.BlockSpec((RET_CHUNK, RET_WIDTH), lambda c: (0, vc)),
            pl.BlockSpec((RET_HEADS, RET_CHUNK, RET_CHUNK), full3),
            vec_spec, vec_spec, vec_spec,
            pl.BlockSpec((1, RET_WIDTH), lambda c: (0, 0)),
        ],
        out_specs=pl.BlockSpec((RET_CHUNK, RET_WIDTH), lambda c: (c, 0)),
        out_shape=jax.ShapeDtypeStruct((s, RET_WIDTH), BF16),
        scratch_shapes=[pltpu.VMEM((RET_HEADS, RET_KEY_DIM, RET_VALUE_DIM), F32)],
        compiler_params=_cparams(("arbitrary",)),
        name="retention",
    )(gc, z, z, z, z, zm, zm, dm, kd, qd, wm, ret_norm_w)


def _store_slabs(slab_ref, rows, value):
    n = slab_ref.shape[1]
    flat = slab_ref.reshape(slab_ref.shape[0] * n, LANES)
    for k in range(n):
        flat[pl.ds(k, rows, stride=n), :] = value[:, k * LANES:(k + 1) * LANES]


def _outproj_kernel(a_ref, r_ref, wt_ref, wb_ref, b_ref, x_ref, o_ref, slab_ref):
    acc = jnp.dot(a_ref[...], wt_ref[...], preferred_element_type=F32)
    acc += jnp.dot(r_ref[...], wb_ref[...], preferred_element_type=F32)
    h = x_ref[...] + acc + b_ref[...]
    o_ref[...] = h
    _store_slabs(slab_ref, h.shape[0], _pack_bf16_pairs(h))


def _packed_columns(c, tile):
    half = tile // 2
    col_hi = (c * LANES // half) * tile + (c * LANES) % half
    return col_hi, col_hi + half


def _outproj(att, ret, w_bf, b, x, tm, tn):
    s = x.shape[0]
    return pl.pallas_call(
        _outproj_kernel,
        grid=(D_MODEL // tn, s // tm),
        in_specs=[
            pl.BlockSpec((tm, ATT_WIDTH), lambda j, i: (i, 0)),
            pl.BlockSpec((tm, RET_WIDTH), lambda j, i: (i, 0)),
            pl.BlockSpec((ATT_WIDTH, tn), lambda j, i: (0, j), pipeline_mode=pl.Buffered(1)),
            pl.BlockSpec((RET_WIDTH, tn), lambda j, i: (1, j), pipeline_mode=pl.Buffered(1)),
            pl.BlockSpec((1, tn), lambda j, i: (0, j)),
            pl.BlockSpec((tm, tn), lambda j, i: (i, j)),
        ],
        out_specs=[pl.BlockSpec((tm, tn), lambda j, i: (i, j)),
                   pl.BlockSpec((tm, tn // (2 * LANES), LANES), lambda j, i: (i, j, 0))],
        out_shape=[jax.ShapeDtypeStruct((s, D_MODEL), F32),
                   jax.ShapeDtypeStruct((s, PACKED_SLAB_ROWS, LANES), U32)],
        compiler_params=_cparams(("arbitrary", "arbitrary")),
        name="outproj",
    )(att, ret, w_bf, w_bf, b, x)


def _router_kernel(h_ref, nw_ref, rw_ref, rb_ref, idx_ref, gate_ref):
    x = h_ref[...]
    ms = jnp.mean(x * x, axis=-1, keepdims=True)
    hn = x * lax.rsqrt(ms + NORM_EPS) * nw_ref[...]
    rw = rw_ref[...]
    rw_hi = rw.astype(BF16)
    rw_lo = (rw - rw_hi.astype(F32)).astype(BF16)
    hn_hi = hn.astype(BF16)
    hn_lo = (hn - hn_hi.astype(F32)).astype(BF16)
    nt = (((1,), (1,)), ((), ()))
    logits = (lax.dot_general(rw_hi, hn_hi, nt, preferred_element_type=F32)
              + lax.dot_general(rw_hi, hn_lo, nt, preferred_element_type=F32)
              + lax.dot_general(rw_lo, hn_hi, nt, preferred_element_type=F32)) + rb_ref[...]
    eid = lax.broadcasted_iota(jnp.int32, logits.shape, 0)
    vals, idxs = [], []
    for _ in range(TOP_K):
        m = jnp.max(logits, axis=0, keepdims=True)
        sel = jnp.min(jnp.where(logits == m, eid, N_EXPERTS), axis=0, keepdims=True)
        vals.append(m)
        idxs.append(sel)
        logits = jnp.where(eid == sel, -jnp.inf, logits)
    es = [jnp.exp(v - vals[0]) for v in vals]
    tot = functools.reduce(lambda a, b: a + b, es)
    idx_ref[...] = jnp.concatenate(idxs, axis=0)
    gate_ref[...] = jnp.concatenate([e / tot for e in es], axis=0)


def _router(h1, norm_w, rw_t, rb, tm):
    s = h1.shape[0]
    return pl.pallas_call(
        _router_kernel,
        grid=(s // tm,),
        in_specs=[
            pl.BlockSpec((tm, D_MODEL), lambda i: (i, 0)),
            pl.BlockSpec((1, D_MODEL), lambda i: (0, 0)),
            pl.BlockSpec((N_EXPERTS, D_MODEL), lambda i: (0, 0)),
            pl.BlockSpec((N_EXPERTS, 1), lambda i: (0, 0)),
        ],
        out_specs=[pl.BlockSpec((TOP_K, tm), lambda i: (0, i)), pl.BlockSpec((TOP_K, tm), lambda i: (0, i))],
        out_shape=[jax.ShapeDtypeStruct((TOP_K, s), jnp.int32), jax.ShapeDtypeStruct((TOP_K, s), F32)],
        compiler_params=_cparams(("parallel",)),
        name="router",
    )(h1, norm_w, rw_t, rb)


def _slab_copy(src_hbm, idx_ref, buf_ref, sems, slot, r):
    return pltpu.make_async_copy(src_hbm.at[idx_ref[0, r]], buf_ref.at[slot, r, pl.ds(0, src_hbm.shape[1]), :],
                                 sems.at[slot])


def _slot_wait(src_hbm, buf_ref, sems, slot):
    n = buf_ref.shape[1]
    pltpu.make_async_copy(src_hbm.at[pl.ds(0, n)], buf_ref.at[slot, :, pl.ds(0, src_hbm.shape[1]), :],
                          sems.at[slot]).wait()


def _slab_chunk(buf_ref, slot, r0, rows, k):
    n_slots, slot_slabs, pitch = buf_ref.shape[:3]
    flat = buf_ref.reshape(n_slots * slot_slabs * pitch, LANES)
    base = (slot * slot_slabs + r0) * pitch + k
    return flat[pl.ds(base, rows, stride=pitch), :]


def _gather_norm_kernel(nused_ref, tok_ref, tokn_ref, h_hbm, nw_ref, o_ref, inv_ref, buf_ref, sems):
    b = pl.program_id(0)
    nu = nused_ref[0]
    tm = o_ref.shape[0]
    slot = b % 2
    grp = GATHER_GROUP
    n_groups = tm // grp

    def issue_block(idx_ref, s):
        def step(i, carry):
            r0 = i * grp
            for r in range(grp):
                _slab_copy(h_hbm, idx_ref, buf_ref, sems, s, r0 + r).start()
            return carry
        lax.fori_loop(0, n_groups, step, 0)

    @pl.when(b == 0)
    def _():
        issue_block(tok_ref, 0)

    @pl.when(b + 1 < nu)
    def _():
        issue_block(tokn_ref, 1 - slot)

    @pl.when(b < nu)
    def _():
        _slot_wait(h_hbm, buf_ref, sems, slot)

        def body(i, carry):
            r0 = pl.multiple_of(i * grp, grp)
            ssq = jnp.zeros((grp, LANES), F32)
            for k in range(PACKED_SLAB_ROWS):
                parts = _unpack_bf16_pairs(_slab_chunk(buf_ref, slot, r0, grp, k))
                for x, col in zip(parts, _packed_columns(k, OUT_TN)):
                    cols = slice(col, col + LANES)
                    o_ref[pl.ds(r0, grp), cols] = (x * nw_ref[:, cols]).astype(o_ref.dtype)
                    ssq = ssq + x * x
            ms = jnp.sum(ssq, axis=-1, keepdims=True) * (1.0 / D_MODEL)
            inv_ref[pl.ds(r0, grp), :] = lax.rsqrt(ms + NORM_EPS)
            return carry
        lax.fori_loop(0, n_groups, body, 0)

    @pl.when(b >= nu)
    def _():
        o_ref[...] = jnp.zeros_like(o_ref)
        inv_ref[...] = jnp.zeros_like(inv_ref)


def _gather_norm(n_used, row_tok3, h1_slabs, norm_w, tm):
    nblk = row_tok3.shape[0]
    cur = lambda b, nu: (jnp.minimum(b, nu[0] - 1), 0, 0)
    nxt = lambda b, nu: (jnp.minimum(b + 1, nu[0] - 1), 0, 0)
    return pl.pallas_call(
        _gather_norm_kernel,
        grid_spec=pltpu.PrefetchScalarGridSpec(
            num_scalar_prefetch=1,
            grid=(nblk,),
            in_specs=[
                pl.BlockSpec((None, 1, tm), cur, memory_space=pltpu.SMEM),
                pl.BlockSpec((None, 1, tm), nxt, memory_space=pltpu.SMEM),
                pl.BlockSpec(memory_space=pl.ANY),
                pl.BlockSpec((1, D_MODEL), lambda b, nu: (0, 0)),
            ],
            out_specs=[pl.BlockSpec((tm, D_MODEL), lambda b, nu: (b, 0)),
                       pl.BlockSpec((tm, 1), lambda b, nu: (b, 0))],
            scratch_shapes=[pltpu.VMEM((2, tm, PACKED_SLAB_PITCH, LANES), U32), pltpu.SemaphoreType.DMA((2,))],
        ),
        out_shape=[jax.ShapeDtypeStruct((nblk * tm, D_MODEL), BF16),
                   jax.ShapeDtypeStruct((nblk * tm, 1), F32)],
        compiler_params=_cparams(("arbitrary",)),
        name="gather_norm",
    )(n_used, row_tok3, row_tok3, h1_slabs, norm_w)


def _swiglu(x, inv, wg, wl, bg, bl):
    glu = jnp.dot(x, wg, preferred_element_type=F32) * inv + bg
    lin = jnp.dot(x, wl, preferred_element_type=F32) * inv + bl
    glu = jnp.minimum(glu, SWIGLU_LIMIT)
    lin = jnp.clip(lin, -SWIGLU_LIMIT, SWIGLU_LIMIT)
    return glu * (1.0 / (1.0 + jnp.exp(-SWIGLU_ALPHA * glu))) * (lin + 1.0)


def _expert_weight_stream(n_cols, bv_ref, be_ref, nx_ref, meta_ref, copies, convert):
    j = pl.program_id(0)
    b = pl.program_id(1)
    e = be_ref[b]
    first = (bv_ref[b] > 0) & ((b == 0) | (e != be_ref[jnp.maximum(b - 1, 0)]))

    @pl.when((j == 0) & (b == 0))
    def _():
        for c in copies(e, 0):
            c.start()

    @pl.when(first)
    def _():
        for c in copies(e, j):
            c.wait()
        convert()
        jn = jnp.where(e == meta_ref[1], j + 1, j)

        @pl.when(jn < n_cols)
        def _():
            for c in copies(nx_ref[b], jn):
                c.start()


def _gateup_kernel(bv_ref, be_ref, nx_ref, meta_ref, x_ref, inv_ref, w_hbm, bg_ref, bl_ref, o_ref, stg_ref,
                   wgb_ref, wlb_ref, sems):
    b = pl.program_id(1)
    valid = bv_ref[b]
    hm = EXPERT_HALF
    nj = D_EXPERT // GU_TN

    def copies(expert, jj):
        out = []
        for half in range(2):
            col = pl.multiple_of((jj + half * nj) * GU_TN, GU_TN)
            out.append(pltpu.make_async_copy(w_hbm.at[expert, :, pl.ds(col, GU_TN)], stg_ref.at[half],
                                             sems.at[half]))
        return out

    def convert():
        wgb_ref[...] = stg_ref[0].astype(BF16)
        wlb_ref[...] = stg_ref[1].astype(BF16)

    _expert_weight_stream(nj, bv_ref, be_ref, nx_ref, meta_ref, copies, convert)

    @pl.when(valid > hm)
    def _():
        act = _swiglu(x_ref[...], inv_ref[...], wgb_ref[...], wlb_ref[...], bg_ref[...], bl_ref[...])
        o_ref[...] = act.astype(o_ref.dtype)

    @pl.when((valid > 0) & (valid <= hm))
    def _():
        act = _swiglu(x_ref[:hm, :], inv_ref[:hm, :], wgb_ref[...], wlb_ref[...], bg_ref[...], bl_ref[...])
        o_ref[:hm, :] = act.astype(o_ref.dtype)
        o_ref[hm:, :] = jnp.zeros((o_ref.shape[0] - hm, o_ref.shape[1]), o_ref.dtype)

    @pl.when(valid == 0)
    def _():
        o_ref[...] = jnp.zeros_like(o_ref)


def _gateup(sched, xb, inv_rows, w_gu, b_gu3, tm):
    nblk = xb.shape[0] // tm
    nj = D_EXPERT // GU_TN
    last = lambda b, meta: jnp.minimum(b, meta[0] - 1)
    return pl.pallas_call(
        _gateup_kernel,
        grid_spec=pltpu.PrefetchScalarGridSpec(
            num_scalar_prefetch=4,
            grid=(nj, nblk),
            in_specs=[
                pl.BlockSpec((tm, D_MODEL), lambda j, b, bv, be, nx, meta: (last(b, meta), 0)),
                pl.BlockSpec((tm, 1), lambda j, b, bv, be, nx, meta: (last(b, meta), 0)),
                pl.BlockSpec(memory_space=pl.ANY),
                pl.BlockSpec((None, 1, GU_TN), lambda j, b, bv, be, nx, meta: (be[b], 0, j)),
                pl.BlockSpec((None, 1, GU_TN), lambda j, b, bv, be, nx, meta: (be[b], 0, j + nj)),
            ],
            out_specs=pl.BlockSpec((tm, GU_TN), lambda j, b, bv, be, nx, meta: (b, j)),
            scratch_shapes=[pltpu.VMEM((2, D_MODEL, GU_TN), F32), pltpu.VMEM((D_MODEL, GU_TN), BF16),
                            pltpu.VMEM((D_MODEL, GU_TN), BF16), pltpu.SemaphoreType.DMA((2,))],
        ),
        out_shape=jax.ShapeDtypeStruct((nblk * tm, D_EXPERT), BF16),
        compiler_params=_cparams(("arbitrary", "arbitrary")),
        name="expert_gateup",
    )(*sched, xb, inv_rows, w_gu, b_gu3, b_gu3)


def _down_kernel(bv_ref, be_ref, nx_ref, meta_ref, a_ref, w_hbm, bias_ref, o_ref, stg_ref, wb_ref, sems):
    b = pl.program_id(1)
    valid = bv_ref[b]
    hm = EXPERT_HALF

    def copies(expert, jj):
        col = pl.multiple_of(jj * DOWN_TN, DOWN_TN)
        return [pltpu.make_async_copy(w_hbm.at[expert, :, pl.ds(col, DOWN_TN)], stg_ref, sems.at[0])]

    def convert():
        wb_ref[...] = stg_ref[...].astype(BF16)

    _expert_weight_stream(D_MODEL // DOWN_TN, bv_ref, be_ref, nx_ref, meta_ref, copies, convert)

    @pl.when(valid > hm)
    def _():
        res = jnp.dot(a_ref[...], wb_ref[...], preferred_element_type=F32) + bias_ref[...]
        _store_slabs(o_ref, o_ref.shape[0], _pack_bf16_pairs(res))

    @pl.when((valid > 0) & (valid <= hm))
    def _():
        res = jnp.dot(a_ref[:hm, :], wb_ref[...], preferred_element_type=F32) + bias_ref[...]
        _store_slabs(o_ref, hm, _pack_bf16_pairs(res))
        o_ref[hm:] = jnp.zeros((o_ref.shape[0] - hm,) + o_ref.shape[1:], o_ref.dtype)

    @pl.when(valid == 0)
    def _():
        o_ref[...] = jnp.zeros_like(o_ref)


def _pack_bf16_pairs(t):
    n = t.shape[1] // 2
    hi = lax.bitcast_convert_type(t[:, :n].astype(BF16).astype(F32), U32)
    lo = lax.bitcast_convert_type(t[:, n:].astype(BF16).astype(F32), U32)
    return hi | (lo >> 16)


def _unpack_bf16_pairs(w):
    hi = lax.bitcast_convert_type(w & jnp.uint32(0xFFFF0000), F32)
    lo = lax.bitcast_convert_type(w << 16, F32)
    return hi, lo


def _down(sched, act, w_down, b_down3, tm):
    nblk = act.shape[0] // tm
    nj = D_MODEL // DOWN_TN
    last = lambda b, meta: jnp.minimum(b, meta[0] - 1)
    return pl.pallas_call(
        _down_kernel,
        grid_spec=pltpu.PrefetchScalarGridSpec(
            num_scalar_prefetch=4,
            grid=(nj, nblk),
            in_specs=[
                pl.BlockSpec((tm, D_EXPERT), lambda j, b, bv, be, nx, meta: (last(b, meta), 0)),
                pl.BlockSpec(memory_space=pl.ANY),
                pl.BlockSpec((None, 1, DOWN_TN), lambda j, b, bv, be, nx, meta: (be[b], 0, j)),
            ],
            out_specs=pl.BlockSpec((tm, DOWN_TN // (2 * LANES), LANES),
                                   lambda j, b, bv, be, nx, meta: (b, j, 0)),
            scratch_shapes=[pltpu.VMEM((D_EXPERT, DOWN_TN), F32), pltpu.VMEM((D_EXPERT, DOWN_TN), BF16),
                            pltpu.SemaphoreType.DMA((1,))],
        ),
        out_shape=jax.ShapeDtypeStruct((nblk * tm, PACKED_SLAB_ROWS, LANES), U32),
        compiler_params=_cparams(("arbitrary", "arbitrary")),
        name="expert_down",
    )(*sched, act, w_down, b_down3)


def _combine_kernel(pos_ref, posn_ref, g_ref, h_ref, nw_ref, y_hbm, o_ref, buf_ref, sems):
    i = pl.program_id(0)
    n_tiles = pl.num_programs(0)
    tq = h_ref.shape[0]
    slot = i % 2
    grp = COMBINE_GROUP
    n_groups = tq // grp

    def issue(idx_ref, s, r0):
        for k in range(TOP_K):
            for r in range(grp):
                _slab_copy(y_hbm, idx_ref, buf_ref, sems, s, k * tq + r0 + r).start()

    @pl.when(i == 0)
    def _():
        def first(t, carry):
            issue(pos_ref, 0, t * grp)
            return carry
        lax.fori_loop(0, n_groups, first, 0)

    _slot_wait(y_hbm, buf_ref, sems, slot)
    has_next = i + 1 < n_tiles

    def body(t, carry):
        r0 = pl.multiple_of(t * grp, grp)

        @pl.when(has_next)
        def _():
            issue(posn_ref, 1 - slot, r0)

        rows = pl.ds(r0, grp)
        g = g_ref[rows, :]
        gk = [jnp.broadcast_to(g[:, k:k + 1], (grp, LANES)) for k in range(TOP_K)]
        ssq = jnp.zeros((grp, LANES), F32)
        for c in range(PACKED_SLAB_ROWS):
            pairs = [_unpack_bf16_pairs(_slab_chunk(buf_ref, slot, k * tq + r0, grp, c)) for k in range(TOP_K)]
            for part, col in enumerate(_packed_columns(c, DOWN_TN)):
                cols = slice(col, col + LANES)
                acc = h_ref[rows, cols]
                for k in range(TOP_K):
                    acc = acc + gk[k] * pairs[k][part]
                o_ref[rows, cols] = acc
                ssq = ssq + acc * acc
        sc = lax.rsqrt(jnp.sum(ssq, axis=-1, keepdims=True) * (1.0 / D_MODEL) + NORM_EPS)
        o_ref[rows, :] = o_ref[rows, :] * sc * nw_ref[...]
        return carry
    lax.fori_loop(0, n_groups, body, 0)


def _combine(pos3, gates, h1, norm_w, yb, tq):
    s = h1.shape[0]
    nt = s // tq
    return pl.pallas_call(
        _combine_kernel,
        grid=(nt,),
        in_specs=[
            pl.BlockSpec((None, 1, TOP_K * tq), lambda i: (i, 0, 0), memory_space=pltpu.SMEM),
            pl.BlockSpec((None, 1, TOP_K * tq), lambda i: (jnp.minimum(i + 1, nt - 1), 0, 0),
                         memory_space=pltpu.SMEM),
            pl.BlockSpec((tq, TOP_K), lambda i: (i, 0)),
            pl.BlockSpec((tq, D_MODEL), lambda i: (i, 0)),
            pl.BlockSpec((1, D_MODEL), lambda i: (0, 0)),
            pl.BlockSpec(memory_space=pl.ANY),
        ],
        out_specs=pl.BlockSpec((tq, D_MODEL), lambda i: (i, 0)),
        out_shape=jax.ShapeDtypeStruct((s, D_MODEL), F32),
        scratch_shapes=[pltpu.VMEM((2, TOP_K * tq, PACKED_SLAB_PITCH, LANES), U32),
                        pltpu.SemaphoreType.DMA((2,))],
        compiler_params=_cparams(("arbitrary",)),
        name="combine_norm",
    )(pos3, pos3, gates, h1, norm_w, yb)


def _routing_metadata(top_idx, tm):
    s = top_idx.shape[0]
    n_assign = s * TOP_K
    nblk = -(-(n_assign + N_EXPERTS * (tm - 1)) // tm)
    e_flat = top_idx.reshape(n_assign)
    order = jnp.argsort(e_flat).astype(jnp.int32)
    rank = jnp.argsort(order).astype(jnp.int32)
    experts = jnp.arange(N_EXPERTS, dtype=jnp.int32)
    counts = jnp.sum((e_flat[:, None] == experts[None, :]).astype(jnp.int32), axis=0)
    padded = ((counts + tm - 1) // tm) * tm
    start_sorted = jnp.cumsum(counts) - counts
    start_pad = jnp.cumsum(padded) - padded
    pos = start_pad[e_flat] + rank - start_sorted[e_flat]
    n_used = jnp.sum(padded) // tm
    blocks = jnp.arange(nblk, dtype=jnp.int32)
    block_exp = jnp.sum((start_pad[None, :] <= (blocks * tm)[:, None]).astype(jnp.int32), axis=1) - 1
    block_exp = block_exp[jnp.minimum(blocks, n_used - 1)]
    block_valid = jnp.clip(counts[block_exp] - (blocks * tm - start_pad[block_exp]), 0, tm)
    block_valid = jnp.where(blocks < n_used, block_valid, 0)
    off = (blocks * tm - start_pad[block_exp])[:, None] + jnp.arange(tm, dtype=jnp.int32)[None, :]
    is_row = (off < counts[block_exp][:, None]) & (blocks < n_used)[:, None]
    src = jnp.clip(start_sorted[block_exp][:, None] + off, 0, n_assign - 1)
    row_tok = jnp.where(is_row, order[src] // TOP_K, 0)
    present = counts > 0
    nxt = experts[None, :] > experts[:, None]
    next_present = jnp.min(jnp.where(nxt & present[None, :], experts[None, :], N_EXPERTS), axis=1)
    first_present = jnp.min(jnp.where(present, experts, N_EXPERTS))
    last_present = jnp.max(jnp.where(present, experts, -1))
    next_present = jnp.where(next_present == N_EXPERTS, first_present, next_present)
    block_next = next_present[block_exp]
    as_i32 = lambda t: t.astype(jnp.int32)
    meta = jnp.stack([as_i32(n_used), as_i32(last_present)])
    sched = (as_i32(block_valid), as_i32(block_exp), as_i32(block_next), meta)
    return (as_i32(n_used).reshape(1), sched, as_i32(row_tok).reshape(nblk, 1, tm),
            as_i32(pos).reshape(s, TOP_K))


def _pick(n, pref):
    t = pref
    while n % t:
        t //= 2
    return t


def kernel(x, meta_tokens, attn_norm_w, w_in, b_in, attn_sinks, ret_norm_w, w_out, b_out, ffn_norm_w, router_w,
           router_b, w_gu, b_gu, w_down, b_down, final_norm_w):
    assert x.shape[0] == 1 and attn_norm_w.shape[0] == 1
    xs = x[0]
    s = xs.shape[0]
    assert s % WINDOW == 0
    tm = _pick(s, 512)

    w_in_bf = w_in[0].astype(BF16)
    b_in2 = b_in[0][None, :]
    nw = attn_norm_w[0][None, :]
    z = _inproj(xs, nw, w_in_bf, b_in2, _rope_tables(N_META, s), tm)
    zm = _inproj(meta_tokens, nw, w_in_bf, b_in2, _rope_tables(0, N_META), N_META)
    zm = jnp.pad(zm, ((0, WINDOW - N_META), (0, 0)))

    att = _attention(z, zm, attn_sinks[0])
    ret = _retention(z, zm, ret_norm_w[0][None, :])
    h1, h1_slabs = _outproj(att, ret, w_out[0].astype(BF16), b_out[0][None, :], xs, tm, OUT_TN)

    ffn_w = ffn_norm_w[0][None, :]
    idx_t, gate_t = _router(h1, ffn_w, router_w[0].T, router_b[0][:, None], tm)
    gates = gate_t.T
    n_used, sched, row_tok3, pos = _routing_metadata(idx_t.T, EXPERT_TM)

    xb, inv_rows = _gather_norm(n_used, row_tok3, h1_slabs, ffn_w, EXPERT_TM)
    act = _gateup(sched, xb, inv_rows, w_gu[0], b_gu[0][:, None, :], EXPERT_TM)
    yb = _down(sched, act, w_down[0], b_down[0][:, None, :], EXPERT_TM)

    tq = WINDOW
    pos3 = pos.reshape(s // tq, tq, TOP_K).transpose(0, 2, 1).reshape(s // tq, 1, TOP_K * tq)
    out = _combine(pos3, gates, h1, final_norm_w[None, :], yb, tq)
    return out[None]
```

```python
import functools

import jax
import jax.numpy as jnp
import numpy as np
from jax import lax
from jax.experimental import pallas as pl
from jax.experimental.pallas import tpu as pltpu

F32 = jnp.float32
BF16 = jnp.bfloat16
U32 = jnp.uint32

D_MODEL = 4096
N_META = 16
ATT_HEADS = 32
ATT_KV_HEADS = 4
ATT_HEAD_DIM = 64
ATT_GROUP = ATT_HEADS // ATT_KV_HEADS
WINDOW = 128
ROPE_THETA = 500000.0
ROPE_DIM = ATT_HEAD_DIM // 4
RET_HEADS = 8
RET_KEY_DIM = 128
RET_VALUE_DIM = 256
RET_CHUNK = 128
RET_ROPE_THETA = 10000.0
ATT_WIDTH = ATT_HEADS * ATT_HEAD_DIM
KV_WIDTH = ATT_KV_HEADS * ATT_HEAD_DIM
RET_QK_WIDTH = RET_HEADS * RET_KEY_DIM
RET_WIDTH = RET_HEADS * RET_VALUE_DIM
IN_WIDTH = ATT_WIDTH + 2 * KV_WIDTH + 2 * RET_QK_WIDTH + 2 * RET_WIDTH
OFF_AQ = 0
OFF_RQ = OFF_AQ + ATT_WIDTH
OFF_RK = OFF_RQ + RET_QK_WIDTH
OFF_RV = OFF_RK + RET_QK_WIDTH
OFF_RG = OFF_RV + RET_WIDTH
OFF_AK = OFF_RG + RET_WIDTH
OFF_AV = OFF_AK + KV_WIDTH
N_EXPERTS = 32
TOP_K = 4
D_EXPERT = D_MODEL // 2
SWIGLU_ALPHA = 1.702
SWIGLU_LIMIT = 7.0
NORM_EPS = 1e-5
GN_EPS = 1e-6

LANES = 128
VMEM_LIMIT = 56 * 1024 * 1024

IN_TN = 512
EXPERT_TM = 512
EXPERT_HALF = EXPERT_TM // 2
GU_TN = 512
DOWN_TN = 2048
OUT_TN = 2048
KEYS_PAD = 3 * WINDOW
NEG_BIG = -1e30
SLAB_ROWS = D_MODEL // LANES
SLAB_PITCH = 40
PACKED_SLAB_ROWS = SLAB_ROWS // 2
PACKED_SLAB_PITCH = 24
GATHER_GROUP = 64
COMBINE_GROUP = 32


def _cparams(sem):
    return pltpu.CompilerParams(dimension_semantics=sem, vmem_limit_bytes=VMEM_LIMIT)


def _src_tile(j):
    assert 2 * KV_WIDTH == IN_TN and ATT_WIDTH % IN_TN == 0
    n_aq = ATT_WIDTH // IN_TN
    n_tiles = IN_WIDTH // IN_TN
    return jnp.where(j < n_aq, j, jnp.where(j < n_tiles - 1, j + 1, n_aq))


def _chunk_classes():
    cls = []
    for c in range(IN_WIDTH // LANES):
        off = c * LANES
        if off < OFF_RQ or OFF_AK <= off < OFF_AV:
            cls.append("a")
        elif OFF_RQ <= off < OFF_RV:
            cls.append("r")
        else:
            cls.append("n")
    return cls


def _inproj_kernel(x_ref, nw_ref, w_ref, b_ref, aa_ref, ab_ref, ac_ref, ra_ref, rb_ref, o_ref, hn_ref):
    j = pl.program_id(1)

    @pl.when(j == 0)
    def _():
        x = x_ref[...]
        ms = jnp.mean(x * x, axis=-1, keepdims=True)
        hn_ref[...] = (x * lax.rsqrt(ms + NORM_EPS) * nw_ref[...]).astype(BF16)

    y = jnp.dot(hn_ref[...], w_ref[...], preferred_element_type=F32) + b_ref[...]

    cpt = IN_TN // LANES
    classes = _chunk_classes()
    n_tiles = IN_WIDTH // IN_TN
    patterns = {}
    for t in range(n_tiles):
        patterns.setdefault(tuple(classes[t * cpt:(t + 1) * cpt]), []).append(t)

    def emit(pattern):
        for c, kind in enumerate(pattern):
            yc = y[:, c * LANES:(c + 1) * LANES]
            if kind == "a":
                yc = (yc * aa_ref[...] + pltpu.roll(yc, LANES - ROPE_DIM // 2, 1) * ab_ref[...]
                      + pltpu.roll(yc, ROPE_DIM // 2, 1) * ac_ref[...])
            elif kind == "r":
                yc = yc * ra_ref[...] + pltpu.roll(yc, RET_KEY_DIM // 2, 1) * rb_ref[...]
            o_ref[:, c * LANES:(c + 1) * LANES] = yc.astype(o_ref.dtype)

    for pattern, tiles in patterns.items():
        cond = functools.reduce(jnp.logical_or, [j == t for t in tiles])
        pl.when(cond)(functools.partial(emit, pattern))


def _inproj(x, norm_w, w_bf, b, tabs, tm):
    m = x.shape[0]
    aa, ab, ac, ra, rb = tabs
    row = lambda i, j: (i, 0)
    tab_spec = pl.BlockSpec((tm, LANES), row)
    return pl.pallas_call(
        _inproj_kernel,
        grid=(m // tm, IN_WIDTH // IN_TN),
        in_specs=[
            pl.BlockSpec((tm, D_MODEL), row),
            pl.BlockSpec((1, D_MODEL), lambda i, j: (0, 0)),
            pl.BlockSpec((D_MODEL, IN_TN), lambda i, j: (0, _src_tile(j))),
            pl.BlockSpec((1, IN_TN), lambda i, j: (0, _src_tile(j))),
            tab_spec, tab_spec, tab_spec, tab_spec, tab_spec,
        ],
        out_specs=pl.BlockSpec((tm, IN_TN), lambda i, j: (i, j)),
        out_shape=jax.ShapeDtypeStruct((m, IN_WIDTH), BF16),
        scratch_shapes=[pltpu.VMEM((tm, D_MODEL), BF16)],
        compiler_params=_cparams(("parallel", "arbitrary")),
        name="inproj",
    )(x, norm_w, w_bf, b, aa, ab, ac, ra, rb)


def _rope_tables(first_pos, n_pos):
    pos = (first_pos + jnp.arange(n_pos, dtype=jnp.int32)).astype(F32)[:, None]
    lane = np.arange(LANES)
    half = ROPE_DIM // 2
    inv = ROPE_THETA ** (-jnp.arange(half, dtype=F32) / half)
    ang = pos * inv[None, :]
    cos, sin = jnp.cos(ang), jnp.sin(ang)
    c = lane % ATT_HEAD_DIM
    f = c % half
    is_lo = jnp.asarray(c < half)[None, :]
    is_hi = jnp.asarray((c >= half) & (c < ROPE_DIM))[None, :]
    cos_l, sin_l = cos[:, f], sin[:, f]
    aa = jnp.where(is_lo | is_hi, cos_l, 1.0)
    ab = jnp.where(is_lo, -sin_l, 0.0)
    ac = jnp.where(is_hi, sin_l, 0.0)
    rhalf = RET_KEY_DIM // 2
    rinv = RET_ROPE_THETA ** (-jnp.arange(rhalf, dtype=F32) / rhalf)
    rang = pos * rinv[None, :]
    rcos, rsin = jnp.cos(rang), jnp.sin(rang)
    rf = lane % rhalf
    ra = rcos[:, rf]
    rb = jnp.where(jnp.asarray(lane < rhalf)[None, :], -rsin[:, rf], rsin[:, rf])
    return aa, ab, ac, ra, rb


def _attn_kernel(sink_ref, q_ref, ko_ref, vo_ref, kp_ref, vp_ref, km_ref, vm_ref, o_ref):
    n = pl.program_id(0)
    w = WINDOW
    qi = lax.broadcasted_iota(jnp.int32, (w, KEYS_PAD), 0)
    ji = lax.broadcasted_iota(jnp.int32, (w, KEYS_PAD), 1)
    vis_prev = (ji < w) & (ji > qi) & (n > 0)
    vis_own = (ji >= w) & (ji < 2 * w) & (ji - w <= qi)
    vis_meta = (ji >= 2 * w) & (ji < 2 * w + N_META)
    mask = vis_prev | vis_own | vis_meta
    lane = lax.broadcasted_iota(jnp.int32, (KEYS_PAD, LANES), 1)
    lo_half = lane < ATT_HEAD_DIM
    olane = lax.broadcasted_iota(jnp.int32, (w, LANES), 1) < ATT_HEAD_DIM
    scale = ATT_HEAD_DIM ** -0.5

    def spread(prev_ref, own_ref, meta_ref, g):
        c0 = (g // 2) * LANES
        chunk = jnp.concatenate([prev_ref[:, c0:c0 + LANES], own_ref[:, c0:c0 + LANES],
                                 meta_ref[:, c0:c0 + LANES]], axis=0).astype(F32)
        swapped = pltpu.roll(chunk, ATT_HEAD_DIM, 1)
        in_lo, in_hi = (chunk, swapped) if g % 2 == 0 else (swapped, chunk)
        even = jnp.where(lo_half, in_lo, 0.0)
        odd = jnp.where(lo_half, 0.0, in_hi)
        return jnp.concatenate([even, odd], axis=0).astype(BF16)

    pairs = ATT_GROUP // 2
    for g in range(ATT_KV_HEADS):
        kz = spread(kp_ref, ko_ref, km_ref, g)
        vz = spread(vp_ref, vo_ref, vm_ref, g)
        c0 = g * pairs
        q4 = jnp.concatenate([q_ref[:, (c0 + r) * LANES:(c0 + r + 1) * LANES] for r in range(pairs)], axis=0)
        s4 = lax.dot_general(q4, kz, (((1,), (1,)), ((), ())), preferred_element_type=F32) * scale
        ps, dens = [], []
        for r in range(pairs):
            prow, drow = [], []
            for half in range(2):
                sink = sink_ref[2 * (c0 + r) + half]
                s = jnp.where(mask, s4[r * w:(r + 1) * w, half * KEYS_PAD:(half + 1) * KEYS_PAD], NEG_BIG)
                m = jnp.maximum(jnp.max(s, axis=-1, keepdims=True), sink)
                p = jnp.exp(s - m)
                drow.append(jnp.sum(p, axis=-1, keepdims=True) + jnp.exp(sink - m))
                prow.append(p.astype(BF16))
            ps.append(jnp.concatenate(prow, axis=1))
            dens.append(jnp.where(olane, drow[0], drow[1]))
        o4 = jnp.dot(jnp.concatenate(ps, axis=0), vz, preferred_element_type=F32)
        for r in range(pairs):
            o = o4[r * w:(r + 1) * w] / dens[r]
            o_ref[:, (c0 + r) * LANES:(c0 + r + 1) * LANES] = o.astype(o_ref.dtype)


def _attention(z, zm, sinks):
    s = z.shape[0]
    nb = s // WINDOW
    kcol, vcol = OFF_AK // KV_WIDTH, OFF_AV // KV_WIDTH
    prev = lambda n: jnp.maximum(n - 1, 0)
    return pl.pallas_call(
        _attn_kernel,
        grid=(nb,),
        in_specs=[
            pl.BlockSpec(memory_space=pltpu.SMEM),
            pl.BlockSpec((WINDOW, ATT_WIDTH), lambda n: (n, 0)),
            pl.BlockSpec((WINDOW, KV_WIDTH), lambda n: (n, kcol)),
            pl.BlockSpec((WINDOW, KV_WIDTH), lambda n: (n, vcol)),
            pl.BlockSpec((WINDOW, KV_WIDTH), lambda n: (prev(n), kcol)),
            pl.BlockSpec((WINDOW, KV_WIDTH), lambda n: (prev(n), vcol)),
            pl.BlockSpec((WINDOW, KV_WIDTH), lambda n: (0, kcol)),
            pl.BlockSpec((WINDOW, KV_WIDTH), lambda n: (0, vcol)),
        ],
        out_specs=pl.BlockSpec((WINDOW, ATT_WIDTH), lambda n: (n, 0)),
        out_shape=jax.ShapeDtypeStruct((s, ATT_WIDTH), BF16),
        compiler_params=_cparams(("parallel",)),
        name="swa_attention",
    )(sinks, z, z, z, z, z, zm, zm)


def _ret_kernel(gc_ref, q_ref, k_ref, v_ref, g_ref, km_ref, vm_ref, dm_ref, kd_ref, qd_ref, wm_ref, nw_ref,
                o_ref, u_ref):
    c = pl.program_id(0)
    dk, dv = RET_KEY_DIM, RET_VALUE_DIM

    @pl.when(c == 0)
    def _():
        for hh in range(RET_HEADS):
            kmw = (km_ref[:, hh * dk:(hh + 1) * dk].astype(F32) * wm_ref[hh]).T.astype(BF16)
            u_ref[hh] = jnp.dot(kmw, vm_ref[:, hh * dv:(hh + 1) * dv], preferred_element_type=F32)

    for hh in range(RET_HEADS):
        q = q_ref[:, hh * dk:(hh + 1) * dk]
        k = k_ref[:, hh * dk:(hh + 1) * dk]
        v = v_ref[:, hh * dv:(hh + 1) * dv]
        u = u_ref[hh]
        inner = lax.dot_general(q, k, (((1,), (1,)), ((), ())), preferred_element_type=F32) * dm_ref[hh]
        intra = jnp.dot(inner.astype(BF16), v, preferred_element_type=F32)
        cross = jnp.dot(q, u.astype(BF16), preferred_element_type=F32) * qd_ref[hh]
        y = intra + cross
        mu = jnp.mean(y, axis=-1, keepdims=True)
        yc = y - mu
        var = jnp.mean(yc * yc, axis=-1, keepdims=True)
        yn = yc * lax.rsqrt(var + GN_EPS)
        g = g_ref[:, hh * dv:(hh + 1) * dv].astype(F32)
        silu = g * (1.0 / (1.0 + jnp.exp(-g)))
        o_ref[:, hh * dv:(hh + 1) * dv] = (silu * yn * nw_ref[:, hh * dv:(hh + 1) * dv]).astype(o_ref.dtype)
        kdt = (k.astype(F32) * kd_ref[hh]).T.astype(BF16)
        u_ref[hh] = u * gc_ref[hh] + jnp.dot(kdt, v, preferred_element_type=F32)


def _ret_tables():
    scale = RET_KEY_DIM ** -0.5
    log_g = jnp.log1p(-(2.0 ** (-5.0 - jnp.arange(RET_HEADS, dtype=F32))))
    i = jnp.arange(RET_CHUNK, dtype=F32)
    diff = i[:, None] - i[None, :]
    dm = jnp.where(diff >= 0, jnp.exp(jnp.maximum(diff, 0.0)[None] * log_g[:, None, None]), 0.0) * scale
    qd = (jnp.exp((i + 1.0)[None, :] * log_g[:, None]) * scale)[:, :, None]
    kd = jnp.exp((RET_CHUNK - 1.0 - i)[None, :] * log_g[:, None])[:, :, None]
    jm = jnp.arange(RET_CHUNK, dtype=F32)
    wm = jnp.where(jm[None, :] < N_META, jnp.exp((N_META - 1 - jm)[None, :] * log_g[:, None]), 0.0)[:, :, None]
    gc = jnp.exp(RET_CHUNK * log_g)
    return gc, dm, kd, qd, wm


def _retention(z, zm, ret_norm_w):
    s = z.shape[0]
    nc = s // RET_CHUNK
    gc, dm, kd, qd, wm = _ret_tables()
    qc, kc = OFF_RQ // RET_QK_WIDTH, OFF_RK // RET_QK_WIDTH
    vc, gcol = OFF_RV // RET_WIDTH, OFF_RG // RET_WIDTH
    full3 = lambda c: (0, 0, 0)
    vec_spec = pl.BlockSpec((RET_HEADS, RET_CHUNK, 1), full3)
    return pl.pallas_call(
        _ret_kernel,
        grid=(nc,),
        in_specs=[
            pl.BlockSpec(memory_space=pltpu.SMEM),
            pl.BlockSpec((RET_CHUNK, RET_QK_WIDTH), lambda c: (c, qc)),
            pl.BlockSpec((RET_CHUNK, RET_QK_WIDTH), lambda c: (c, kc)),
            pl.BlockSpec((RET_CHUNK, RET_WIDTH), lambda c: (c, vc)),
            pl.BlockSpec((RET_CHUNK, RET_WIDTH), lambda c: (c, gcol)),
            pl.BlockSpec((RET_CHUNK, RET_QK_WIDTH), lambda c: (0, kc)),
            pl.BlockSpec((RET_CHUNK, RET_WIDTH), lambda c: (0, vc)),
            pl.BlockSpec((RET_HEADS, RET_CHUNK, RET_CHUNK), full3),
            vec_spec, vec_spec, vec_spec,
            pl.BlockSpec((1, RET_WIDTH), lambda c: (0, 0)),
        ],
        out_specs=pl.BlockSpec((RET_CHUNK, RET_WIDTH), lambda c: (c, 0)),
        out_shape=jax.ShapeDtypeStruct((s, RET_WIDTH), BF16),
        scratch_shapes=[pltpu.VMEM((RET_HEADS, RET_KEY_DIM, RET_VALUE_DIM), F32)],
        compiler_params=_cparams(("arbitrary",)),
        name="retention",
    )(gc, z, z, z, z, zm, zm, dm, kd, qd, wm, ret_norm_w)


def _store_slabs(slab_ref, rows, value):
    n = slab_ref.shape[1]
    flat = slab_ref.reshape(slab_ref.shape[0] * n, LANES)
    for k in range(n):
        flat[pl.ds(k, rows, stride=n), :] = value[:, k * LANES:(k + 1) * LANES]


def _outproj_kernel(a_ref, r_ref, wt_ref, wb_ref, b_ref, x_ref, o_ref, slab_ref):
    acc = jnp.dot(a_ref[...], wt_ref[...], preferred_element_type=F32)
    acc += jnp.dot(r_ref[...], wb_ref[...], preferred_element_type=F32)
    h = x_ref[...] + acc + b_ref[...]
    o_ref[...] = h
    _store_slabs(slab_ref, h.shape[0], _pack_bf16_pairs(h))


def _packed_columns(c, tile):
    half = tile // 2
    col_hi = (c * LANES // half) * tile + (c * LANES) % half
    return col_hi, col_hi + half


def _outproj(att, ret, w_bf, b, x, tm, tn):
    s = x.shape[0]
    return pl.pallas_call(
        _outproj_kernel,
        grid=(D_MODEL // tn, s // tm),
        in_specs=[
            pl.BlockSpec((tm, ATT_WIDTH), lambda j, i: (i, 0)),
            pl.BlockSpec((tm, RET_WIDTH), lambda j, i: (i, 0)),
            pl.BlockSpec((ATT_WIDTH, tn), lambda j, i: (0, j), pipeline_mode=pl.Buffered(1)),
            pl.BlockSpec((RET_WIDTH, tn), lambda j, i: (1, j), pipeline_mode=pl.Buffered(1)),
            pl.BlockSpec((1, tn), lambda j, i: (0, j)),
            pl.BlockSpec((tm, tn), lambda j, i: (i, j)),
        ],
        out_specs=[pl.BlockSpec((tm, tn), lambda j, i: (i, j)),
                   pl.BlockSpec((tm, tn // (2 * LANES), LANES), lambda j, i: (i, j, 0))],
        out_shape=[jax.ShapeDtypeStruct((s, D_MODEL), F32),
                   jax.ShapeDtypeStruct((s, PACKED_SLAB_ROWS, LANES), U32)],
        compiler_params=_cparams(("arbitrary", "arbitrary")),
        name="outproj",
    )(att, ret, w_bf, w_bf, b, x)


def _router_kernel(h_ref, nw_ref, rw_ref, rb_ref, idx_ref, gate_ref):
    x = h_ref[...]
    ms = jnp.mean(x * x, axis=-1, keepdims=True)
    hn = x * lax.rsqrt(ms + NORM_EPS) * nw_ref[...]
    rw = rw_ref[...]
    rw_hi = rw.astype(BF16)
    rw_lo = (rw - rw_hi.astype(F32)).astype(BF16)
    hn_hi = hn.astype(BF16)
    hn_lo = (hn - hn_hi.astype(F32)).astype(BF16)
    nt = (((1,), (1,)), ((), ()))
    logits = (lax.dot_general(rw_hi, hn_hi, nt, preferred_element_type=F32)
              + lax.dot_general(rw_hi, hn_lo, nt, preferred_element_type=F32)
              + lax.dot_general(rw_lo, hn_hi, nt, preferred_element_type=F32)) + rb_ref[...]
    eid = lax.broadcasted_iota(jnp.int32, logits.shape, 0)
    vals, idxs = [], []
    for _ in range(TOP_K):
        m = jnp.max(logits, axis=0, keepdims=True)
        sel = jnp.min(jnp.where(logits == m, eid, N_EXPERTS), axis=0, keepdims=True)
        vals.append(m)
        idxs.append(sel)
        logits = jnp.where(eid == sel, -jnp.inf, logits)
    es = [jnp.exp(v - vals[0]) for v in vals]
    tot = functools.reduce(lambda a, b: a + b, es)
    idx_ref[...] = jnp.concatenate(idxs, axis=0)
    gate_ref[...] = jnp.concatenate([e / tot for e in es], axis=0)


def _router(h1, norm_w, rw_t, rb, tm):
    s = h1.shape[0]
    return pl.pallas_call(
        _router_kernel,
        grid=(s // tm,),
        in_specs=[
            pl.BlockSpec((tm, D_MODEL), lambda i: (i, 0)),
            pl.BlockSpec((1, D_MODEL), lambda i: (0, 0)),
            pl.BlockSpec((N_EXPERTS, D_MODEL), lambda i: (0, 0)),
            pl.BlockSpec((N_EXPERTS, 1), lambda i: (0, 0)),
        ],
        out_specs=[pl.BlockSpec((TOP_K, tm), lambda i: (0, i)), pl.BlockSpec((TOP_K, tm), lambda i: (0, i))],
        out_shape=[jax.ShapeDtypeStruct((TOP_K, s), jnp.int32), jax.ShapeDtypeStruct((TOP_K, s), F32)],
        compiler_params=_cparams(("parallel",)),
        name="router",
    )(h1, norm_w, rw_t, rb)


def _slab_copy(src_hbm, idx_ref, buf_ref, sems, slot, r):
    return pltpu.make_async_copy(src_hbm.at[idx_ref[0, r]], buf_ref.at[slot, r, pl.ds(0, src_hbm.shape[1]), :],
                                 sems.at[slot])


def _slot_wait(src_hbm, buf_ref, sems, slot):
    n = buf_ref.shape[1]
    pltpu.make_async_copy(src_hbm.at[pl.ds(0, n)], buf_ref.at[slot, :, pl.ds(0, src_hbm.shape[1]), :],
                          sems.at[slot]).wait()


def _slab_chunk(buf_ref, slot, r0, rows, k):
    n_slots, slot_slabs, pitch = buf_ref.shape[:3]
    flat = buf_ref.reshape(n_slots * slot_slabs * pitch, LANES)
    base = (slot * slot_slabs + r0) * pitch + k
    return flat[pl.ds(base, rows, stride=pitch), :]


def _gather_norm_kernel(nused_ref, bv_ref, tok_ref, tokn_ref, h_hbm, nw_ref, o_ref, inv_ref, buf_ref, sems):
    b = pl.program_id(0)
    nu = nused_ref[0]
    tm = o_ref.shape[0]
    slot = b % 2
    grp = GATHER_GROUP
    n_groups = tm // grp

    def live_groups(blk):
        return (bv_ref[blk] + (grp - 1)) // grp

    def issue_block(idx_ref, s, n):
        def step(i, carry):
            r0 = i * grp
            for r in range(grp):
                _slab_copy(h_hbm, idx_ref, buf_ref, sems, s, r0 + r).start()
            return carry
        lax.fori_loop(0, n, step, 0)

    @pl.when(b == 0)
    def _():
        issue_block(tok_ref, 0, live_groups(0))

    @pl.when(b + 1 < nu)
    def _():
        issue_block(tokn_ref, 1 - slot, live_groups(b + 1))

    @pl.when(b < nu)
    def _():
        n_live = live_groups(b)

        def wait_group(i, carry):
            pltpu.make_async_copy(h_hbm.at[pl.ds(0, grp)],
                                  buf_ref.at[slot, pl.ds(0, grp), pl.ds(0, h_hbm.shape[1]), :],
                                  sems.at[slot]).wait()
            return carry
        lax.fori_loop(0, n_live, wait_group, 0)

        def pad_group(i, carry):
            r0 = pl.multiple_of(i * grp, grp)
            o_ref[pl.ds(r0, grp), :] = jnp.zeros((grp, o_ref.shape[1]), o_ref.dtype)
            inv_ref[pl.ds(r0, grp), :] = jnp.zeros((grp, 1), inv_ref.dtype)
            return carry
        lax.fori_loop(n_live, n_groups, pad_group, 0)

        def body(i, carry):
            r0 = pl.multiple_of(i * grp, grp)
            ssq = jnp.zeros((grp, LANES), F32)
            for k in range(PACKED_SLAB_ROWS):
                parts = _unpack_bf16_pairs(_slab_chunk(buf_ref, slot, r0, grp, k))
                for x, col in zip(parts, _packed_columns(k, OUT_TN)):
                    cols = slice(col, col + LANES)
                    o_ref[pl.ds(r0, grp), cols] = (x * nw_ref[:, cols]).astype(o_ref.dtype)
                    ssq = ssq + x * x
            ms = jnp.sum(ssq, axis=-1, keepdims=True) * (1.0 / D_MODEL)
            inv_ref[pl.ds(r0, grp), :] = lax.rsqrt(ms + NORM_EPS)
            return carry
        lax.fori_loop(0, n_live, body, 0)

    @pl.when(b >= nu)
    def _():
        o_ref[...] = jnp.zeros_like(o_ref)
        inv_ref[...] = jnp.zeros_like(inv_ref)


def _gather_norm(n_used, block_valid, row_tok3, h1_slabs, norm_w, tm):
    nblk = row_tok3.shape[0]
    cur = lambda b, nu, bv: (jnp.minimum(b, nu[0] - 1), 0, 0)
    nxt = lambda b, nu, bv: (jnp.minimum(b + 1, nu[0] - 1), 0, 0)
    return pl.pallas_call(
        _gather_norm_kernel,
        grid_spec=pltpu.PrefetchScalarGridSpec(
            num_scalar_prefetch=2,
            grid=(nblk,),
            in_specs=[
                pl.BlockSpec((None, 1, tm), cur, memory_space=pltpu.SMEM),
                pl.BlockSpec((None, 1, tm), nxt, memory_space=pltpu.SMEM),
                pl.BlockSpec(memory_space=pl.ANY),
                pl.BlockSpec((1, D_MODEL), lambda b, nu, bv: (0, 0)),
            ],
            out_specs=[pl.BlockSpec((tm, D_MODEL), lambda b, nu, bv: (b, 0)),
                       pl.BlockSpec((tm, 1), lambda b, nu, bv: (b, 0))],
            scratch_shapes=[pltpu.VMEM((2, tm, PACKED_SLAB_PITCH, LANES), U32), pltpu.SemaphoreType.DMA((2,))],
        ),
        out_shape=[jax.ShapeDtypeStruct((nblk * tm, D_MODEL), BF16),
                   jax.ShapeDtypeStruct((nblk * tm, 1), F32)],
        compiler_params=_cparams(("arbitrary",)),
        name="gather_norm",
    )(n_used, block_valid, row_tok3, row_tok3, h1_slabs, norm_w)


def _swiglu(x, inv, wg, wl, bg, bl):
    glu = jnp.dot(x, wg, preferred_element_type=F32) * inv + bg
    lin = jnp.dot(x, wl, preferred_element_type=F32) * inv + bl
    glu = jnp.minimum(glu, SWIGLU_LIMIT)
    lin = jnp.clip(lin, -SWIGLU_LIMIT, SWIGLU_LIMIT)
    return glu * (1.0 / (1.0 + jnp.exp(-SWIGLU_ALPHA * glu))) * (lin + 1.0)


def _expert_weight_stream(n_cols, bv_ref, be_ref, nx_ref, meta_ref, copies, convert):
    j = pl.program_id(0)
    b = pl.program_id(1)
    e = be_ref[b]
    first = (bv_ref[b] > 0) & ((b == 0) | (e != be_ref[jnp.maximum(b - 1, 0)]))

    @pl.when((j == 0) & (b == 0))
    def _():
        for c in copies(e, 0):
            c.start()

    @pl.when(first)
    def _():
        for c in copies(e, j):
            c.wait()
        convert()
        jn = jnp.where(e == meta_ref[1], j + 1, j)

        @pl.when(jn < n_cols)
        def _():
            for c in copies(nx_ref[b], jn):
                c.start()


def _gateup_kernel(bv_ref, be_ref, nx_ref, meta_ref, x_ref, inv_ref, w_hbm, bg_ref, bl_ref, o_ref, stg_ref,
                   wgb_ref, wlb_ref, sems):
    b = pl.program_id(1)
    valid = bv_ref[b]
    hm = EXPERT_HALF
    nj = D_EXPERT // GU_TN

    def copies(expert, jj):
        out = []
        for half in range(2):
            col = pl.multiple_of((jj + half * nj) * GU_TN, GU_TN)
            out.append(pltpu.make_async_copy(w_hbm.at[expert, :, pl.ds(col, GU_TN)], stg_ref.at[half],
                                             sems.at[half]))
        return out

    def convert():
        wgb_ref[...] = stg_ref[0].astype(BF16)
        wlb_ref[...] = stg_ref[1].astype(BF16)

    _expert_weight_stream(nj, bv_ref, be_ref, nx_ref, meta_ref, copies, convert)

    @pl.when(valid > hm)
    def _():
        act = _swiglu(x_ref[...], inv_ref[...], wgb_ref[...], wlb_ref[...], bg_ref[...], bl_ref[...])
        o_ref[...] = act.astype(o_ref.dtype)

    @pl.when((valid > 0) & (valid <= hm))
    def _():
        act = _swiglu(x_ref[:hm, :], inv_ref[:hm, :], wgb_ref[...], wlb_ref[...], bg_ref[...], bl_ref[...])
        o_ref[:hm, :] = act.astype(o_ref.dtype)
        o_ref[hm:, :] = jnp.zeros((o_ref.shape[0] - hm, o_ref.shape[1]), o_ref.dtype)

    @pl.when(valid == 0)
    def _():
        o_ref[...] = jnp.zeros_like(o_ref)


def _gateup(sched, xb, inv_rows, w_gu, b_gu3, tm):
    nblk = xb.shape[0] // tm
    nj = D_EXPERT // GU_TN
    last = lambda b, meta: jnp.minimum(b, meta[0] - 1)
    return pl.pallas_call(
        _gateup_kernel,
        grid_spec=pltpu.PrefetchScalarGridSpec(
            num_scalar_prefetch=4,
            grid=(nj, nblk),
            in_specs=[
                pl.BlockSpec((tm, D_MODEL), lambda j, b, bv, be, nx, meta: (last(b, meta), 0)),
                pl.BlockSpec((tm, 1), lambda j, b, bv, be, nx, meta: (last(b, meta), 0)),
                pl.BlockSpec(memory_space=pl.ANY),
                pl.BlockSpec((None, 1, GU_TN), lambda j, b, bv, be, nx, meta: (be[b], 0, j)),
                pl.BlockSpec((None, 1, GU_TN), lambda j, b, bv, be, nx, meta: (be[b], 0, j + nj)),
            ],
            out_specs=pl.BlockSpec((tm, GU_TN), lambda j, b, bv, be, nx, meta: (b, j)),
            scratch_shapes=[pltpu.VMEM((2, D_MODEL, GU_TN), F32), pltpu.VMEM((D_MODEL, GU_TN), BF16),
                            pltpu.VMEM((D_MODEL, GU_TN), BF16), pltpu.SemaphoreType.DMA((2,))],
        ),
        out_shape=jax.ShapeDtypeStruct((nblk * tm, D_EXPERT), BF16),
        compiler_params=_cparams(("arbitrary", "arbitrary")),
        name="expert_gateup",
    )(*sched, xb, inv_rows, w_gu, b_gu3, b_gu3)


def _down_kernel(bv_ref, be_ref, nx_ref, meta_ref, a_ref, w_hbm, bias_ref, o_ref, stg_ref, wb_ref, sems):
    b = pl.program_id(1)
    valid = bv_ref[b]
    hm = EXPERT_HALF

    def copies(expert, jj):
        col = pl.multiple_of(jj * DOWN_TN, DOWN_TN)
        return [pltpu.make_async_copy(w_hbm.at[expert, :, pl.ds(col, DOWN_TN)], stg_ref, sems.at[0])]

    def convert():
        wb_ref[...] = stg_ref[...].astype(BF16)

    _expert_weight_stream(D_MODEL // DOWN_TN, bv_ref, be_ref, nx_ref, meta_ref, copies, convert)

    @pl.when(valid > hm)
    def _():
        res = jnp.dot(a_ref[...], wb_ref[...], preferred_element_type=F32) + bias_ref[...]
        _store_slabs(o_ref, o_ref.shape[0], _pack_bf16_pairs(res))

    @pl.when((valid > 0) & (valid <= hm))
    def _():
        res = jnp.dot(a_ref[:hm, :], wb_ref[...], preferred_element_type=F32) + bias_ref[...]
        _store_slabs(o_ref, hm, _pack_bf16_pairs(res))
        o_ref[hm:] = jnp.zeros((o_ref.shape[0] - hm,) + o_ref.shape[1:], o_ref.dtype)

    @pl.when(valid == 0)
    def _():
        o_ref[...] = jnp.zeros_like(o_ref)


def _pack_bf16_pairs(t):
    n = t.shape[1] // 2
    hi = lax.bitcast_convert_type(t[:, :n].astype(BF16).astype(F32), U32)
    lo = lax.bitcast_convert_type(t[:, n:].astype(BF16).astype(F32), U32)
    return hi | (lo >> 16)


def _unpack_bf16_pairs(w):
    hi = lax.bitcast_convert_type(w & jnp.uint32(0xFFFF0000), F32)
    lo = lax.bitcast_convert_type(w << 16, F32)
    return hi, lo


def _down(sched, act, w_down, b_down3, tm):
    nblk = act.shape[0] // tm
    nj = D_MODEL // DOWN_TN
    last = lambda b, meta: jnp.minimum(b, meta[0] - 1)
    return pl.pallas_call(
        _down_kernel,
        grid_spec=pltpu.PrefetchScalarGridSpec(
            num_scalar_prefetch=4,
            grid=(nj, nblk),
            in_specs=[
                pl.BlockSpec((tm, D_EXPERT), lambda j, b, bv, be, nx, meta: (last(b, meta), 0)),
                pl.BlockSpec(memory_space=pl.ANY),
                pl.BlockSpec((None, 1, DOWN_TN), lambda j, b, bv, be, nx, meta: (be[b], 0, j)),
            ],
            out_specs=pl.BlockSpec((tm, DOWN_TN // (2 * LANES), LANES),
                                   lambda j, b, bv, be, nx, meta: (b, j, 0)),
            scratch_shapes=[pltpu.VMEM((D_EXPERT, DOWN_TN), F32), pltpu.VMEM((D_EXPERT, DOWN_TN), BF16),
                            pltpu.SemaphoreType.DMA((1,))],
        ),
        out_shape=jax.ShapeDtypeStruct((nblk * tm, PACKED_SLAB_ROWS, LANES), U32),
        compiler_params=_cparams(("arbitrary", "arbitrary")),
        name="expert_down",
    )(*sched, act, w_down, b_down3)


def _combine_kernel(pos_ref, posn_ref, g_ref, h_ref, nw_ref, y_hbm, o_ref, buf_ref, sems):
    i = pl.program_id(0)
    n_tiles = pl.num_programs(0)
    tq = h_ref.shape[0]
    slot = i % 2
    grp = COMBINE_GROUP
    n_groups = tq // grp

    def issue(idx_ref, s, r0):
        for k in range(TOP_K):
            for r in range(grp):
                _slab_copy(y_hbm, idx_ref, buf_ref, sems, s, k * tq + r0 + r).start()

    @pl.when(i == 0)
    def _():
        def first(t, carry):
            issue(pos_ref, 0, t * grp)
            return carry
        lax.fori_loop(0, n_groups, first, 0)

    _slot_wait(y_hbm, buf_ref, sems, slot)
    has_next = i + 1 < n_tiles

    def body(t, carry):
        r0 = pl.multiple_of(t * grp, grp)

        @pl.when(has_next)
        def _():
            issue(posn_ref, 1 - slot, r0)

        rows = pl.ds(r0, grp)
        g = g_ref[rows, :]
        gk = [jnp.broadcast_to(g[:, k:k + 1], (grp, LANES)) for k in range(TOP_K)]
        ssq = jnp.zeros((grp, LANES), F32)
        for c in range(PACKED_SLAB_ROWS):
            pairs = [_unpack_bf16_pairs(_slab_chunk(buf_ref, slot, k * tq + r0, grp, c)) for k in range(TOP_K)]
            for part, col in enumerate(_packed_columns(c, DOWN_TN)):
                cols = slice(col, col + LANES)
                acc = h_ref[rows, cols]
                for k in range(TOP_K):
                    acc = acc + gk[k] * pairs[k][part]
                o_ref[rows, cols] = acc
                ssq = ssq + acc * acc
        sc = lax.rsqrt(jnp.sum(ssq, axis=-1, keepdims=True) * (1.0 / D_MODEL) + NORM_EPS)
        o_ref[rows, :] = o_ref[rows, :] * sc * nw_ref[...]
        return carry
    lax.fori_loop(0, n_groups, body, 0)


def _combine(pos3, gates, h1, norm_w, yb, tq):
    s = h1.shape[0]
    nt = s // tq
    return pl.pallas_call(
        _combine_kernel,
        grid=(nt,),
        in_specs=[
            pl.BlockSpec((None, 1, TOP_K * tq), lambda i: (i, 0, 0), memory_space=pltpu.SMEM),
            pl.BlockSpec((None, 1, TOP_K * tq), lambda i: (jnp.minimum(i + 1, nt - 1), 0, 0),
                         memory_space=pltpu.SMEM),
            pl.BlockSpec((tq, TOP_K), lambda i: (i, 0)),
            pl.BlockSpec((tq, D_MODEL), lambda i: (i, 0)),
            pl.BlockSpec((1, D_MODEL), lambda i: (0, 0)),
            pl.BlockSpec(memory_space=pl.ANY),
        ],
        out_specs=pl.BlockSpec((tq, D_MODEL), lambda i: (i, 0)),
        out_shape=jax.ShapeDtypeStruct((s, D_MODEL), F32),
        scratch_shapes=[pltpu.VMEM((2, TOP_K * tq, PACKED_SLAB_PITCH, LANES), U32),
                        pltpu.SemaphoreType.DMA((2,))],
        compiler_params=_cparams(("arbitrary",)),
        name="combine_norm",
    )(pos3, pos3, gates, h1, norm_w, yb)


def _routing_metadata(top_idx, tm):
    s = top_idx.shape[0]
    n_assign = s * TOP_K
    nblk = -(-(n_assign + N_EXPERTS * (tm - 1)) // tm)
    e_flat = top_idx.reshape(n_assign)
    order = jnp.argsort(e_flat).astype(jnp.int32)
    rank = jnp.argsort(order).astype(jnp.int32)
    experts = jnp.arange(N_EXPERTS, dtype=jnp.int32)
    counts = jnp.sum((e_flat[:, None] == experts[None, :]).astype(jnp.int32), axis=0)
    padded = ((counts + tm - 1) // tm) * tm
    start_sorted = jnp.cumsum(counts) - counts
    start_pad = jnp.cumsum(padded) - padded
    pos = start_pad[e_flat] + rank - start_sorted[e_flat]
    n_used = jnp.sum(padded) // tm
    blocks = jnp.arange(nblk, dtype=jnp.int32)
    block_exp = jnp.sum((start_pad[None, :] <= (blocks * tm)[:, None]).astype(jnp.int32), axis=1) - 1
    block_exp = block_exp[jnp.minimum(blocks, n_used - 1)]
    block_valid = jnp.clip(counts[block_exp] - (blocks * tm - start_pad[block_exp]), 0, tm)
    block_valid = jnp.where(blocks < n_used, block_valid, 0)
    off = (blocks * tm - start_pad[block_exp])[:, None] + jnp.arange(tm, dtype=jnp.int32)[None, :]
    is_row = (off < counts[block_exp][:, None]) & (blocks < n_used)[:, None]
    src = jnp.clip(start_sorted[block_exp][:, None] + off, 0, n_assign - 1)
    row_tok = jnp.where(is_row, order[src] // TOP_K, 0)
    present = counts > 0
    nxt = experts[None, :] > experts[:, None]
    next_present = jnp.min(jnp.where(nxt & present[None, :], experts[None, :], N_EXPERTS), axis=1)
    first_present = jnp.min(jnp.where(present, experts, N_EXPERTS))
    last_present = jnp.max(jnp.where(present, experts, -1))
    next_present = jnp.where(next_present == N_EXPERTS, first_present, next_present)
    block_next = next_present[block_exp]
    as_i32 = lambda t: t.astype(jnp.int32)
    meta = jnp.stack([as_i32(n_used), as_i32(last_present)])
    sched = (as_i32(block_valid), as_i32(block_exp), as_i32(block_next), meta)
    return (as_i32(n_used).reshape(1), sched, as_i32(row_tok).reshape(nblk, 1, tm),
            as_i32(pos).reshape(s, TOP_K))


def _pick(n, pref):
    t = pref
    while n % t:
        t //= 2
    return t


def kernel(x, meta_tokens, attn_norm_w, w_in, b_in, attn_sinks, ret_norm_w, w_out, b_out, ffn_norm_w, router_w,
           router_b, w_gu, b_gu, w_down, b_down, final_norm_w):
    assert x.shape[0] == 1 and attn_norm_w.shape[0] == 1
    xs = x[0]
    s = xs.shape[0]
    assert s % WINDOW == 0
    tm = _pick(s, 512)

    w_in_bf = w_in[0].astype(BF16)
    b_in2 = b_in[0][None, :]
    nw = attn_norm_w[0][None, :]
    z = _inproj(xs, nw, w_in_bf, b_in2, _rope_tables(N_META, s), tm)
    zm = _inproj(meta_tokens, nw, w_in_bf, b_in2, _rope_tables(0, N_META), N_META)
    zm = jnp.pad(zm, ((0, WINDOW - N_META), (0, 0)))

    att = _attention(z, zm, attn_sinks[0])
    ret = _retention(z, zm, ret_norm_w[0][None, :])
    h1, h1_slabs = _outproj(att, ret, w_out[0].astype(BF16), b_out[0][None, :], xs, tm, OUT_TN)

    ffn_w = ffn_norm_w[0][None, :]
    idx_t, gate_t = _router(h1, ffn_w, router_w[0].T, router_b[0][:, None], tm)
    gates = gate_t.T
    n_used, sched, row_tok3, pos = _routing_metadata(idx_t.T, EXPERT_TM)

    xb, inv_rows = _gather_norm(n_used, sched[0], row_tok3, h1_slabs, ffn_w, EXPERT_TM)
    act = _gateup(sched, xb, inv_rows, w_gu[0], b_gu[0][:, None, :], EXPERT_TM)
    yb = _down(sched, act, w_down[0], b_down[0][:, None, :], EXPERT_TM)

    tq = WINDOW
    pos3 = pos.reshape(s // tq, tq, TOP_K).transpose(0, 2, 1).reshape(s // tq, 1, TOP_K * tq)
    out = _combine(pos3, gates, h1, final_norm_w[None, :], yb, tq)
    return out[None]
```

```python
import functools

import jax
import jax.numpy as jnp
import numpy as np
from jax import lax
from jax.experimental import pallas as pl
from jax.experimental.pallas import tpu as pltpu

F32 = jnp.float32
BF16 = jnp.bfloat16
U32 = jnp.uint32

D_MODEL = 4096
N_META = 16
ATT_HEADS = 32
ATT_KV_HEADS = 4
ATT_HEAD_DIM = 64
ATT_GROUP = ATT_HEADS // ATT_KV_HEADS
WINDOW = 128
ROPE_THETA = 500000.0
ROPE_DIM = ATT_HEAD_DIM // 4
RET_HEADS = 8
RET_KEY_DIM = 128
RET_VALUE_DIM = 256
RET_CHUNK = 128
RET_ROPE_THETA = 10000.0
ATT_WIDTH = ATT_HEADS * ATT_HEAD_DIM
KV_WIDTH = ATT_KV_HEADS * ATT_HEAD_DIM
RET_QK_WIDTH = RET_HEADS * RET_KEY_DIM
RET_WIDTH = RET_HEADS * RET_VALUE_DIM
IN_WIDTH = ATT_WIDTH + 2 * KV_WIDTH + 2 * RET_QK_WIDTH + 2 * RET_WIDTH
OFF_AQ = 0
OFF_RQ = OFF_AQ + ATT_WIDTH
OFF_RK = OFF_RQ + RET_QK_WIDTH
OFF_RV = OFF_RK + RET_QK_WIDTH
OFF_RG = OFF_RV + RET_WIDTH
OFF_AK = OFF_RG + RET_WIDTH
OFF_AV = OFF_AK + KV_WIDTH
N_EXPERTS = 32
TOP_K = 4
D_EXPERT = D_MODEL // 2
SWIGLU_ALPHA = 1.702
SWIGLU_LIMIT = 7.0
NORM_EPS = 1e-5
GN_EPS = 1e-6

LANES = 128
VMEM_LIMIT = 56 * 1024 * 1024

IN_TN = 512
EXPERT_TM = 512
EXPERT_HALF = EXPERT_TM // 2
GU_TN = 512
DOWN_TN = 2048
OUT_TN = 2048
KEYS_PAD = 3 * WINDOW
NEG_BIG = -1e30
SLAB_ROWS = D_MODEL // LANES
SLAB_PITCH = 40
PACKED_SLAB_ROWS = SLAB_ROWS // 2
PACKED_SLAB_PITCH = 24
GATHER_GROUP = 64
COMBINE_GROUP = 32


def _cparams(sem):
    return pltpu.CompilerParams(dimension_semantics=sem, vmem_limit_bytes=VMEM_LIMIT)


def _src_tile(j):
    assert 2 * KV_WIDTH == IN_TN and ATT_WIDTH % IN_TN == 0
    n_aq = ATT_WIDTH // IN_TN
    n_tiles = IN_WIDTH // IN_TN
    return jnp.where(j < n_aq, j, jnp.where(j < n_tiles - 1, j + 1, n_aq))


def _chunk_classes():
    cls = []
    for c in range(IN_WIDTH // LANES):
        off = c * LANES
        if off < OFF_RQ or OFF_AK <= off < OFF_AV:
            cls.append("a")
        elif OFF_RQ <= off < OFF_RV:
            cls.append("r")
        else:
            cls.append("n")
    return cls


def _inproj_kernel(x_ref, nw_ref, w_ref, b_ref, aa_ref, ab_ref, ac_ref, ra_ref, rb_ref, o_ref, hn_ref):
    j = pl.program_id(1)

    @pl.when(j == 0)
    def _():
        x = x_ref[...]
        ms = jnp.mean(x * x, axis=-1, keepdims=True)
        hn_ref[...] = (x * lax.rsqrt(ms + NORM_EPS) * nw_ref[...]).astype(BF16)

    y = jnp.dot(hn_ref[...], w_ref[...], preferred_element_type=F32) + b_ref[...]

    cpt = IN_TN // LANES
    classes = _chunk_classes()
    n_tiles = IN_WIDTH // IN_TN
    patterns = {}
    for t in range(n_tiles):
        patterns.setdefault(tuple(classes[t * cpt:(t + 1) * cpt]), []).append(t)

    def emit(pattern):
        for c, kind in enumerate(pattern):
            yc = y[:, c * LANES:(c + 1) * LANES]
            if kind == "a":
                yc = (yc * aa_ref[...] + pltpu.roll(yc, LANES - ROPE_DIM // 2, 1) * ab_ref[...]
                      + pltpu.roll(yc, ROPE_DIM // 2, 1) * ac_ref[...])
            elif kind == "r":
                yc = yc * ra_ref[...] + pltpu.roll(yc, RET_KEY_DIM // 2, 1) * rb_ref[...]
            o_ref[:, c * LANES:(c + 1) * LANES] = yc.astype(o_ref.dtype)

    for pattern, tiles in patterns.items():
        cond = functools.reduce(jnp.logical_or, [j == t for t in tiles])
        pl.when(cond)(functools.partial(emit, pattern))


def _inproj(x, norm_w, w_bf, b, tabs, tm):
    m = x.shape[0]
    aa, ab, ac, ra, rb = tabs
    row = lambda i, j: (i, 0)
    tab_spec = pl.BlockSpec((tm, LANES), row)
    return pl.pallas_call(
        _inproj_kernel,
        grid=(m // tm, IN_WIDTH // IN_TN),
        in_specs=[
            pl.BlockSpec((tm, D_MODEL), row),
            pl.BlockSpec((1, D_MODEL), lambda i, j: (0, 0)),
            pl.BlockSpec((D_MODEL, IN_TN), lambda i, j: (0, _src_tile(j))),
            pl.BlockSpec((1, IN_TN), lambda i, j: (0, _src_tile(j))),
            tab_spec, tab_spec, tab_spec, tab_spec, tab_spec,
        ],
        out_specs=pl.BlockSpec((tm, IN_TN), lambda i, j: (i, j)),
        out_shape=jax.ShapeDtypeStruct((m, IN_WIDTH), BF16),
        scratch_shapes=[pltpu.VMEM((tm, D_MODEL), BF16)],
        compiler_params=_cparams(("parallel", "arbitrary")),
        name="inproj",
    )(x, norm_w, w_bf, b, aa, ab, ac, ra, rb)


def _rope_tables(first_pos, n_pos):
    pos = (first_pos + jnp.arange(n_pos, dtype=jnp.int32)).astype(F32)[:, None]
    lane = np.arange(LANES)
    half = ROPE_DIM // 2
    inv = ROPE_THETA ** (-jnp.arange(half, dtype=F32) / half)
    ang = pos * inv[None, :]
    cos, sin = jnp.cos(ang), jnp.sin(ang)
    c = lane % ATT_HEAD_DIM
    f = c % half
    is_lo = jnp.asarray(c < half)[None, :]
    is_hi = jnp.asarray((c >= half) & (c < ROPE_DIM))[None, :]
    cos_l, sin_l = cos[:, f], sin[:, f]
    aa = jnp.where(is_lo | is_hi, cos_l, 1.0)
    ab = jnp.where(is_lo, -sin_l, 0.0)
    ac = jnp.where(is_hi, sin_l, 0.0)
    rhalf = RET_KEY_DIM // 2
    rinv = RET_ROPE_THETA ** (-jnp.arange(rhalf, dtype=F32) / rhalf)
    rang = pos * rinv[None, :]
    rcos, rsin = jnp.cos(rang), jnp.sin(rang)
    rf = lane % rhalf
    ra = rcos[:, rf]
    rb = jnp.where(jnp.asarray(lane < rhalf)[None, :], -rsin[:, rf], rsin[:, rf])
    return aa, ab, ac, ra, rb


def _attn_kernel(sink_ref, q_ref, ko_ref, vo_ref, kp_ref, vp_ref, km_ref, vm_ref, o_ref):
    n = pl.program_id(0)
    w = WINDOW
    qi = lax.broadcasted_iota(jnp.int32, (w, KEYS_PAD), 0)
    ji = lax.broadcasted_iota(jnp.int32, (w, KEYS_PAD), 1)
    vis_prev = (ji < w) & (ji > qi) & (n > 0)
    vis_own = (ji >= w) & (ji < 2 * w) & (ji - w <= qi)
    vis_meta = (ji >= 2 * w) & (ji < 2 * w + N_META)
    mask = vis_prev | vis_own | vis_meta
    lane = lax.broadcasted_iota(jnp.int32, (KEYS_PAD, LANES), 1)
    lo_half = lane < ATT_HEAD_DIM
    olane = lax.broadcasted_iota(jnp.int32, (w, LANES), 1) < ATT_HEAD_DIM
    scale = ATT_HEAD_DIM ** -0.5

    def spread(prev_ref, own_ref, meta_ref, g):
        c0 = (g // 2) * LANES
        chunk = jnp.concatenate([prev_ref[:, c0:c0 + LANES], own_ref[:, c0:c0 + LANES],
                                 meta_ref[:, c0:c0 + LANES]], axis=0).astype(F32)
        swapped = pltpu.roll(chunk, ATT_HEAD_DIM, 1)
        in_lo, in_hi = (chunk, swapped) if g % 2 == 0 else (swapped, chunk)
        even = jnp.where(lo_half, in_lo, 0.0)
        odd = jnp.where(lo_half, 0.0, in_hi)
        return jnp.concatenate([even, odd], axis=0).astype(BF16)

    pairs = ATT_GROUP // 2
    for g in range(ATT_KV_HEADS):
        kz = spread(kp_ref, ko_ref, km_ref, g)
        vz = spread(vp_ref, vo_ref, vm_ref, g)
        c0 = g * pairs
        q4 = jnp.concatenate([q_ref[:, (c0 + r) * LANES:(c0 + r + 1) * LANES] for r in range(pairs)], axis=0)
        s4 = lax.dot_general(q4, kz, (((1,), (1,)), ((), ())), preferred_element_type=F32) * scale
        ps, dens = [], []
        for r in range(pairs):
            prow, drow = [], []
            for half in range(2):
                sink = sink_ref[2 * (c0 + r) + half]
                s = jnp.where(mask, s4[r * w:(r + 1) * w, half * KEYS_PAD:(half + 1) * KEYS_PAD], NEG_BIG)
                m = jnp.maximum(jnp.max(s, axis=-1, keepdims=True), sink)
                p = jnp.exp(s - m)
                drow.append(jnp.sum(p, axis=-1, keepdims=True) + jnp.exp(sink - m))
                prow.append(p.astype(BF16))
            ps.append(jnp.concatenate(prow, axis=1))
            dens.append(jnp.where(olane, drow[0], drow[1]))
        o4 = jnp.dot(jnp.concatenate(ps, axis=0), vz, preferred_element_type=F32)
        for r in range(pairs):
            o = o4[r * w:(r + 1) * w] / dens[r]
            o_ref[:, (c0 + r) * LANES:(c0 + r + 1) * LANES] = o.astype(o_ref.dtype)


def _attention(z, zm, sinks):
    s = z.shape[0]
    nb = s // WINDOW
    kcol, vcol = OFF_AK // KV_WIDTH, OFF_AV // KV_WIDTH
    prev = lambda n: jnp.maximum(n - 1, 0)
    return pl.pallas_call(
        _attn_kernel,
        grid=(nb,),
        in_specs=[
            pl.BlockSpec(memory_space=pltpu.SMEM),
            pl.BlockSpec((WINDOW, ATT_WIDTH), lambda n: (n, 0)),
            pl.BlockSpec((WINDOW, KV_WIDTH), lambda n: (n, kcol)),
            pl.BlockSpec((WINDOW, KV_WIDTH), lambda n: (n, vcol)),
            pl.BlockSpec((WINDOW, KV_WIDTH), lambda n: (prev(n), kcol)),
            pl.BlockSpec((WINDOW, KV_WIDTH), lambda n: (prev(n), vcol)),
            pl.BlockSpec((WINDOW, KV_WIDTH), lambda n: (0, kcol)),
            pl.BlockSpec((WINDOW, KV_WIDTH), lambda n: (0, vcol)),
        ],
        out_specs=pl.BlockSpec((WINDOW, ATT_WIDTH), lambda n: (n, 0)),
        out_shape=jax.ShapeDtypeStruct((s, ATT_WIDTH), BF16),
        compiler_params=_cparams(("parallel",)),
        name="swa_attention",
    )(sinks, z, z, z, z, z, zm, zm)


def _ret_kernel(gc_ref, q_ref, k_ref, v_ref, g_ref, km_ref, vm_ref, dm_ref, kd_ref, qd_ref, wm_ref, nw_ref,
                o_ref, u_ref):
    c = pl.program_id(0)
    dk, dv = RET_KEY_DIM, RET_VALUE_DIM

    @pl.when(c == 0)
    def _():
        for hh in range(RET_HEADS):
            kmw = (km_ref[:, hh * dk:(hh + 1) * dk].astype(F32) * wm_ref[hh]).T.astype(BF16)
            u_ref[hh] = jnp.dot(kmw, vm_ref[:, hh * dv:(hh + 1) * dv], preferred_element_type=F32)

    for hh in range(RET_HEADS):
        q = q_ref[:, hh * dk:(hh + 1) * dk]
        k = k_ref[:, hh * dk:(hh + 1) * dk]
        v = v_ref[:, hh * dv:(hh + 1) * dv]
        u = u_ref[hh]
        inner = lax.dot_general(q, k, (((1,), (1,)), ((), ())), preferred_element_type=F32) * dm_ref[hh]
        intra = jnp.dot(inner.astype(BF16), v, preferred_element_type=F32)
        cross = jnp.dot(q, u.astype(BF16), preferred_element_type=F32) * qd_ref[hh]
        y = intra + cross
        mu = jnp.mean(y, axis=-1, keepdims=True)
        yc = y - mu
        var = jnp.mean(yc * yc, axis=-1, keepdims=True)
        yn = yc * lax.rsqrt(var + GN_EPS)
        g = g_ref[:, hh * dv:(hh + 1) * dv].astype(F32)
        silu = g * (1.0 / (1.0 + jnp.exp(-g)))
        o_ref[:, hh * dv:(hh + 1) * dv] = (silu * yn * nw_ref[:, hh * dv:(hh + 1) * dv]).astype(o_ref.dtype)
        kdt = (k.astype(F32) * kd_ref[hh]).T.astype(BF16)
        u_ref[hh] = u * gc_ref[hh] + jnp.dot(kdt, v, preferred_element_type=F32)


def _ret_tables():
    scale = RET_KEY_DIM ** -0.5
    log_g = jnp.log1p(-(2.0 ** (-5.0 - jnp.arange(RET_HEADS, dtype=F32))))
    i = jnp.arange(RET_CHUNK, dtype=F32)
    diff = i[:, None] - i[None, :]
    dm = jnp.where(diff >= 0, jnp.exp(jnp.maximum(diff, 0.0)[None] * log_g[:, None, None]), 0.0) * scale
    qd = (jnp.exp((i + 1.0)[None, :] * log_g[:, None]) * scale)[:, :, None]
    kd = jnp.exp((RET_CHUNK - 1.0 - i)[None, :] * log_g[:, None])[:, :, None]
    jm = jnp.arange(RET_CHUNK, dtype=F32)
    wm = jnp.where(jm[None, :] < N_META, jnp.exp((N_META - 1 - jm)[None, :] * log_g[:, None]), 0.0)[:, :, None]
    gc = jnp.exp(RET_CHUNK * log_g)
    return gc, dm, kd, qd, wm


def _retention(z, zm, ret_norm_w):
    s = z.shape[0]
    nc = s // RET_CHUNK
    gc, dm, kd, qd, wm = _ret_tables()
    qc, kc = OFF_RQ // RET_QK_WIDTH, OFF_RK // RET_QK_WIDTH
    vc, gcol = OFF_RV // RET_WIDTH, OFF_RG // RET_WIDTH
    full3 = lambda c: (0, 0, 0)
    vec_spec = pl.BlockSpec((RET_HEADS, RET_CHUNK, 1), full3)
    return pl.pallas_call(
        _ret_kernel,
        grid=(nc,),
        in_specs=[
            pl.BlockSpec(memory_space=pltpu.SMEM),
            pl.BlockSpec((RET_CHUNK, RET_QK_WIDTH), lambda c: (c, qc)),
            pl.BlockSpec((RET_CHUNK, RET_QK_WIDTH), lambda c: (c, kc)),
            pl.BlockSpec((RET_CHUNK, RET_WIDTH), lambda c: (c, vc)),
            pl.BlockSpec((RET_CHUNK, RET_WIDTH), lambda c: (c, gcol)),
            pl.BlockSpec((RET_CHUNK, RET_QK_WIDTH), lambda c: (0, kc)),
            pl.BlockSpec((RET_CHUNK, RET_WIDTH), lambda c: (0, vc)),
            pl.BlockSpec((RET_HEADS, RET_CHUNK, RET_CHUNK), full3),
            vec_spec, vec_spec, vec_spec,
            pl.BlockSpec((1, RET_WIDTH), lambda c: (0, 0)),
        ],
        out_specs=pl.BlockSpec((RET_CHUNK, RET_WIDTH), lambda c: (c, 0)),
        out_shape=jax.ShapeDtypeStruct((s, RET_WIDTH), BF16),
        scratch_shapes=[pltpu.VMEM((RET_HEADS, RET_KEY_DIM, RET_VALUE_DIM), F32)],
        compiler_params=_cparams(("arbitrary",)),
        name="retention",
    )(gc, z, z, z, z, zm, zm, dm, kd, qd, wm, ret_norm_w)


def _store_slabs(slab_ref, rows, value):
    n = slab_ref.shape[1]
    flat = slab_ref.reshape(slab_ref.shape[0] * n, LANES)
    for k in range(n):
        flat[pl.ds(k, rows, stride=n), :] = value[:, k * LANES:(k + 1) * LANES]


def _outproj_kernel(a_ref, r_ref, wt_ref, wb_ref, b_ref, x_ref, o_ref, slab_ref):
    acc = jnp.dot(a_ref[...], wt_ref[...], preferred_element_type=F32)
    acc += jnp.dot(r_ref[...], wb_ref[...], preferred_element_type=F32)
    h = x_ref[...] + acc + b_ref[...]
    o_ref[...] = h
    _store_slabs(slab_ref, h.shape[0], _pack_bf16_pairs(h))


def _packed_columns(c, tile):
    half = tile // 2
    col_hi = (c * LANES // half) * tile + (c * LANES) % half
    return col_hi, col_hi + half


def _outproj(att, ret, w_bf, b, x, tm, tn):
    s = x.shape[0]
    return pl.pallas_call(
        _outproj_kernel,
        grid=(D_MODEL // tn, s // tm),
        in_specs=[
            pl.BlockSpec((tm, ATT_WIDTH), lambda j, i: (i, 0)),
            pl.BlockSpec((tm, RET_WIDTH), lambda j, i: (i, 0)),
            pl.BlockSpec((ATT_WIDTH, tn), lambda j, i: (0, j), pipeline_mode=pl.Buffered(1)),
            pl.BlockSpec((RET_WIDTH, tn), lambda j, i: (1, j), pipeline_mode=pl.Buffered(1)),
            pl.BlockSpec((1, tn), lambda j, i: (0, j)),
            pl.BlockSpec((tm, tn), lambda j, i: (i, j)),
        ],
        out_specs=[pl.BlockSpec((tm, tn), lambda j, i: (i, j)),
                   pl.BlockSpec((tm, tn // (2 * LANES), LANES), lambda j, i: (i, j, 0))],
        out_shape=[jax.ShapeDtypeStruct((s, D_MODEL), F32),
                   jax.ShapeDtypeStruct((s, PACKED_SLAB_ROWS, LANES), U32)],
        compiler_params=_cparams(("arbitrary", "arbitrary")),
        name="outproj",
    )(att, ret, w_bf, w_bf, b, x)


def _router_kernel(h_ref, nw_ref, rw_ref, rb_ref, idx_ref, gate_ref):
    x = h_ref[...]
    ms = jnp.mean(x * x, axis=-1, keepdims=True)
    hn = x * lax.rsqrt(ms + NORM_EPS) * nw_ref[...]
    rw = rw_ref[...]
    rw_hi = rw.astype(BF16)
    rw_lo = (rw - rw_hi.astype(F32)).astype(BF16)
    hn_hi = hn.astype(BF16)
    hn_lo = (hn - hn_hi.astype(F32)).astype(BF16)
    nt = (((1,), (1,)), ((), ()))
    logits = (lax.dot_general(rw_hi, hn_hi, nt, preferred_element_type=F32)
              + lax.dot_general(rw_hi, hn_lo, nt, preferred_element_type=F32)
              + lax.dot_general(rw_lo, hn_hi, nt, preferred_element_type=F32)) + rb_ref[...]
    eid = lax.broadcasted_iota(jnp.int32, logits.shape, 0)
    vals, idxs = [], []
    for _ in range(TOP_K):
        m = jnp.max(logits, axis=0, keepdims=True)
        sel = jnp.min(jnp.where(logits == m, eid, N_EXPERTS), axis=0, keepdims=True)
        vals.append(m)
        idxs.append(sel)
        logits = jnp.where(eid == sel, -jnp.inf, logits)
    es = [jnp.exp(v - vals[0]) for v in vals]
    tot = functools.reduce(lambda a, b: a + b, es)
    idx_ref[...] = jnp.concatenate(idxs, axis=0)
    gate_ref[...] = jnp.concatenate([e / tot for e in es], axis=0)


def _router(h1, norm_w, rw_t, rb, tm):
    s = h1.shape[0]
    return pl.pallas_call(
        _router_kernel,
        grid=(s // tm,),
        in_specs=[
            pl.BlockSpec((tm, D_MODEL), lambda i: (i, 0)),
            pl.BlockSpec((1, D_MODEL), lambda i: (0, 0)),
            pl.BlockSpec((N_EXPERTS, D_MODEL), lambda i: (0, 0)),
            pl.BlockSpec((N_EXPERTS, 1), lambda i: (0, 0)),
        ],
        out_specs=[pl.BlockSpec((TOP_K, tm), lambda i: (0, i)), pl.BlockSpec((TOP_K, tm), lambda i: (0, i))],
        out_shape=[jax.ShapeDtypeStruct((TOP_K, s), jnp.int32), jax.ShapeDtypeStruct((TOP_K, s), F32)],
        compiler_params=_cparams(("parallel",)),
        name="router",
    )(h1, norm_w, rw_t, rb)


def _slab_copy(src_hbm, idx_ref, buf_ref, sems, slot, r):
    return pltpu.make_async_copy(src_hbm.at[idx_ref[0, r]], buf_ref.at[slot, r, pl.ds(0, src_hbm.shape[1]), :],
                                 sems.at[slot])


def _slot_wait(src_hbm, buf_ref, sems, slot):
    n = buf_ref.shape[1]
    pltpu.make_async_copy(src_hbm.at[pl.ds(0, n)], buf_ref.at[slot, :, pl.ds(0, src_hbm.shape[1]), :],
                          sems.at[slot]).wait()


def _slab_chunk(buf_ref, slot, r0, rows, k):
    n_slots, slot_slabs, pitch = buf_ref.shape[:3]
    flat = buf_ref.reshape(n_slots * slot_slabs * pitch, LANES)
    base = (slot * slot_slabs + r0) * pitch + k
    return flat[pl.ds(base, rows, stride=pitch), :]


def _gather_norm_kernel(nused_ref, bv_ref, tok_ref, tokn_ref, h_hbm, nw_ref, o_ref, inv_ref, buf_ref, sems):
    b = pl.program_id(0)
    nu = nused_ref[0]
    tm = o_ref.shape[0]
    slot = b % 2
    grp = GATHER_GROUP
    n_groups = tm // grp

    def live_groups(blk):
        return (bv_ref[blk] + (grp - 1)) // grp

    def issue_block(idx_ref, s, n):
        def step(i, carry):
            r0 = i * grp
            for r in range(grp):
                _slab_copy(h_hbm, idx_ref, buf_ref, sems, s, r0 + r).start()
            return carry
        lax.fori_loop(0, n, step, 0)

    @pl.when(b == 0)
    def _():
        issue_block(tok_ref, 0, live_groups(0))

    @pl.when(b + 1 < nu)
    def _():
        issue_block(tokn_ref, 1 - slot, live_groups(b + 1))

    @pl.when(b < nu)
    def _():
        n_live = live_groups(b)

        def wait_group(i, carry):
            pltpu.make_async_copy(h_hbm.at[pl.ds(0, grp)],
                                  buf_ref.at[slot, pl.ds(0, grp), pl.ds(0, h_hbm.shape[1]), :],
                                  sems.at[slot]).wait()
            return carry
        lax.fori_loop(0, n_live, wait_group, 0)

        def pad_group(i, carry):
            r0 = pl.multiple_of(i * grp, grp)
            o_ref[pl.ds(r0, grp), :] = jnp.zeros((grp, o_ref.shape[1]), o_ref.dtype)
            inv_ref[pl.ds(r0, grp), :] = jnp.zeros((grp, 1), inv_ref.dtype)
            return carry
        lax.fori_loop(n_live, n_groups, pad_group, 0)

        def body(i, carry):
            r0 = pl.multiple_of(i * grp, grp)
            ssq = jnp.zeros((grp, LANES), F32)
            for k in range(PACKED_SLAB_ROWS):
                parts = _unpack_bf16_pairs(_slab_chunk(buf_ref, slot, r0, grp, k))
                for x, col in zip(parts, _packed_columns(k, OUT_TN)):
                    cols = slice(col, col + LANES)
                    o_ref[pl.ds(r0, grp), cols] = (x * nw_ref[:, cols]).astype(o_ref.dtype)
                    ssq = ssq + x * x
            ms = jnp.sum(ssq, axis=-1, keepdims=True) * (1.0 / D_MODEL)
            inv_ref[pl.ds(r0, grp), :] = lax.rsqrt(ms + NORM_EPS)
            return carry
        lax.fori_loop(0, n_live, body, 0)

    @pl.when(b >= nu)
    def _():
        o_ref[...] = jnp.zeros_like(o_ref)
        inv_ref[...] = jnp.zeros_like(inv_ref)


def _gather_norm(n_used, block_valid, row_tok3, h1_slabs, norm_w, tm):
    nblk = row_tok3.shape[0]
    cur = lambda b, nu, bv: (jnp.minimum(b, nu[0] - 1), 0, 0)
    nxt = lambda b, nu, bv: (jnp.minimum(b + 1, nu[0] - 1), 0, 0)
    return pl.pallas_call(
        _gather_norm_kernel,
        grid_spec=pltpu.PrefetchScalarGridSpec(
            num_scalar_prefetch=2,
            grid=(nblk,),
            in_specs=[
                pl.BlockSpec((None, 1, tm), cur, memory_space=pltpu.SMEM),
                pl.BlockSpec((None, 1, tm), nxt, memory_space=pltpu.SMEM),
                pl.BlockSpec(memory_space=pl.ANY),
                pl.BlockSpec((1, D_MODEL), lambda b, nu, bv: (0, 0)),
            ],
            out_specs=[pl.BlockSpec((tm, D_MODEL), lambda b, nu, bv: (b, 0)),
                       pl.BlockSpec((tm, 1), lambda b, nu, bv: (b, 0))],
            scratch_shapes=[pltpu.VMEM((2, tm, PACKED_SLAB_PITCH, LANES), U32), pltpu.SemaphoreType.DMA((2,))],
        ),
        out_shape=[jax.ShapeDtypeStruct((nblk * tm, D_MODEL), BF16),
                   jax.ShapeDtypeStruct((nblk * tm, 1), F32)],
        compiler_params=_cparams(("arbitrary",)),
        name="gather_norm",
    )(n_used, block_valid, row_tok3, row_tok3, h1_slabs, norm_w)


def _swiglu(x, inv, wg, wl, bg, bl):
    glu = jnp.dot(x, wg, preferred_element_type=F32) * inv + bg
    lin = jnp.dot(x, wl, preferred_element_type=F32) * inv + bl
    glu = jnp.minimum(glu, SWIGLU_LIMIT)
    lin = jnp.clip(lin, -SWIGLU_LIMIT, SWIGLU_LIMIT)
    return glu * (1.0 / (1.0 + jnp.exp(-SWIGLU_ALPHA * glu))) * (lin + 1.0)


def _expert_weight_stream(n_cols, bv_ref, be_ref, nx_ref, meta_ref, copies, convert):
    j = pl.program_id(0)
    b = pl.program_id(1)
    e = be_ref[b]
    first = (bv_ref[b] > 0) & ((b == 0) | (e != be_ref[jnp.maximum(b - 1, 0)]))

    @pl.when((j == 0) & (b == 0))
    def _():
        for c in copies(e, 0):
            c.start()

    @pl.when(first)
    def _():
        for c in copies(e, j):
            c.wait()
        convert()
        jn = jnp.where(e == meta_ref[1], j + 1, j)

        @pl.when(jn < n_cols)
        def _():
            for c in copies(nx_ref[b], jn):
                c.start()


def _gateup_kernel(bv_ref, be_ref, nx_ref, meta_ref, x_ref, inv_ref, w_hbm, bg_ref, bl_ref, o_ref, stg_ref,
                   wgb_ref, wlb_ref, sems):
    b = pl.program_id(1)
    valid = bv_ref[b]
    hm = EXPERT_HALF
    nj = D_EXPERT // GU_TN

    def copies(expert, jj):
        out = []
        for half in range(2):
            col = pl.multiple_of((jj + half * nj) * GU_TN, GU_TN)
            out.append(pltpu.make_async_copy(w_hbm.at[expert, :, pl.ds(col, GU_TN)], stg_ref.at[half],
                                             sems.at[half]))
        return out

    def convert():
        wgb_ref[...] = stg_ref[0].astype(BF16)
        wlb_ref[...] = stg_ref[1].astype(BF16)

    _expert_weight_stream(nj, bv_ref, be_ref, nx_ref, meta_ref, copies, convert)

    @pl.when(valid > hm)
    def _():
        act = _swiglu(x_ref[...], inv_ref[...], wgb_ref[...], wlb_ref[...], bg_ref[...], bl_ref[...])
        o_ref[...] = act.astype(o_ref.dtype)

    @pl.when((valid > 0) & (valid <= hm))
    def _():
        act = _swiglu(x_ref[:hm, :], inv_ref[:hm, :], wgb_ref[...], wlb_ref[...], bg_ref[...], bl_ref[...])
        o_ref[:hm, :] = act.astype(o_ref.dtype)
        o_ref[hm:, :] = jnp.zeros((o_ref.shape[0] - hm, o_ref.shape[1]), o_ref.dtype)

    @pl.when(valid == 0)
    def _():
        o_ref[...] = jnp.zeros_like(o_ref)


def _gateup(sched, xb, inv_rows, w_gu, b_gu3, tm):
    nblk = xb.shape[0] // tm
    nj = D_EXPERT // GU_TN
    last = lambda b, meta: jnp.minimum(b, meta[0] - 1)
    return pl.pallas_call(
        _gateup_kernel,
        grid_spec=pltpu.PrefetchScalarGridSpec(
            num_scalar_prefetch=4,
            grid=(nj, nblk),
            in_specs=[
                pl.BlockSpec((tm, D_MODEL), lambda j, b, bv, be, nx, meta: (last(b, meta), 0)),
                pl.BlockSpec((tm, 1), lambda j, b, bv, be, nx, meta: (last(b, meta), 0)),
                pl.BlockSpec(memory_space=pl.ANY),
                pl.BlockSpec((None, 1, GU_TN), lambda j, b, bv, be, nx, meta: (be[b], 0, j)),
                pl.BlockSpec((None, 1, GU_TN), lambda j, b, bv, be, nx, meta: (be[b], 0, j + nj)),
            ],
            out_specs=pl.BlockSpec((tm, GU_TN), lambda j, b, bv, be, nx, meta: (b, j)),
            scratch_shapes=[pltpu.VMEM((2, D_MODEL, GU_TN), F32), pltpu.VMEM((D_MODEL, GU_TN), BF16),
                            pltpu.VMEM((D_MODEL, GU_TN), BF16), pltpu.SemaphoreType.DMA((2,))],
        ),
        out_shape=jax.ShapeDtypeStruct((nblk * tm, D_EXPERT), BF16),
        compiler_params=_cparams(("arbitrary", "arbitrary")),
        name="expert_gateup",
    )(*sched, xb, inv_rows, w_gu, b_gu3, b_gu3)


def _down_kernel(bv_ref, be_ref, nx_ref, meta_ref, a_ref, w_hbm, bias_ref, o_ref, stg_ref, wb_ref, sems):
    b = pl.program_id(1)
    valid = bv_ref[b]
    hm = EXPERT_HALF

    def copies(expert, jj):
        col = pl.multiple_of(jj * DOWN_TN, DOWN_TN)
        return [pltpu.make_async_copy(w_hbm.at[expert, :, pl.ds(col, DOWN_TN)], stg_ref, sems.at[0])]

    def convert():
        wb_ref[...] = stg_ref[...].astype(BF16)

    _expert_weight_stream(D_MODEL // DOWN_TN, bv_ref, be_ref, nx_ref, meta_ref, copies, convert)

    @pl.when(valid > hm)
    def _():
        res = jnp.dot(a_ref[...], wb_ref[...], preferred_element_type=F32) + bias_ref[...]
        _store_slabs(o_ref, o_ref.shape[0], _pack_bf16_pairs(res))

    @pl.when((valid > 0) & (valid <= hm))
    def _():
        res = jnp.dot(a_ref[:hm, :], wb_ref[...], preferred_element_type=F32) + bias_ref[...]
        _store_slabs(o_ref, hm, _pack_bf16_pairs(res))
        o_ref[hm:] = jnp.zeros((o_ref.shape[0] - hm,) + o_ref.shape[1:], o_ref.dtype)

    @pl.when(valid == 0)
    def _():
        o_ref[...] = jnp.zeros_like(o_ref)


def _pack_bf16_pairs(t):
    n = t.shape[1] // 2
    hi = lax.bitcast_convert_type(t[:, :n].astype(BF16).astype(F32), U32)
    lo = lax.bitcast_convert_type(t[:, n:].astype(BF16).astype(F32), U32)
    return hi | (lo >> 16)


def _unpack_bf16_pairs(w):
    hi = lax.bitcast_convert_type(w & jnp.uint32(0xFFFF0000), F32)
    lo = lax.bitcast_convert_type(w << 16, F32)
    return hi, lo


def _down(sched, act, w_down, b_down3, tm):
    nblk = act.shape[0] // tm
    nj = D_MODEL // DOWN_TN
    last = lambda b, meta: jnp.minimum(b, meta[0] - 1)
    return pl.pallas_call(
        _down_kernel,
        grid_spec=pltpu.PrefetchScalarGridSpec(
            num_scalar_prefetch=4,
            grid=(nj, nblk),
            in_specs=[
                pl.BlockSpec((tm, D_EXPERT), lambda j, b, bv, be, nx, meta: (last(b, meta), 0)),
                pl.BlockSpec(memory_space=pl.ANY),
                pl.BlockSpec((None, 1, DOWN_TN), lambda j, b, bv, be, nx, meta: (be[b], 0, j)),
            ],
            out_specs=pl.BlockSpec((tm, DOWN_TN // (2 * LANES), LANES),
                                   lambda j, b, bv, be, nx, meta: (b, j, 0)),
            scratch_shapes=[pltpu.VMEM((D_EXPERT, DOWN_TN), F32), pltpu.VMEM((D_EXPERT, DOWN_TN), BF16),
                            pltpu.SemaphoreType.DMA((1,))],
        ),
        out_shape=jax.ShapeDtypeStruct((nblk * tm, PACKED_SLAB_ROWS, LANES), U32),
        compiler_params=_cparams(("arbitrary", "arbitrary")),
        name="expert_down",
    )(*sched, act, w_down, b_down3)


def _combine_kernel(pos_ref, posn_ref, g_ref, h_ref, nw_ref, y_hbm, o_ref, buf_ref, sems):
    i = pl.program_id(0)
    n_tiles = pl.num_programs(0)
    tq = h_ref.shape[0]
    slot = i % 2
    grp = COMBINE_GROUP
    n_groups = tq // grp

    def issue(idx_ref, s, r0):
        for k in range(TOP_K):
            for r in range(grp):
                _slab_copy(y_hbm, idx_ref, buf_ref, sems, s, k * tq + r0 + r).start()

    @pl.when(i == 0)
    def _():
        def first(t, carry):
            issue(pos_ref, 0, t * grp)
            return carry
        lax.fori_loop(0, n_groups, first, 0)

    _slot_wait(y_hbm, buf_ref, sems, slot)
    has_next = i + 1 < n_tiles

    def body(t, carry):
        r0 = pl.multiple_of(t * grp, grp)

        @pl.when(has_next)
        def _():
            issue(posn_ref, 1 - slot, r0)

        rows = pl.ds(r0, grp)
        g = g_ref[rows, :]
        gk = [jnp.broadcast_to(g[:, k:k + 1], (grp, LANES)) for k in range(TOP_K)]
        ssq = jnp.zeros((grp, LANES), F32)
        for c in range(PACKED_SLAB_ROWS):
            pairs = [_unpack_bf16_pairs(_slab_chunk(buf_ref, slot, k * tq + r0, grp, c)) for k in range(TOP_K)]
            for part, col in enumerate(_packed_columns(c, DOWN_TN)):
                cols = slice(col, col + LANES)
                acc = h_ref[rows, cols]
                for k in range(TOP_K):
                    acc = acc + gk[k] * pairs[k][part]
                o_ref[rows, cols] = acc
                ssq = ssq + acc * acc
        sc = lax.rsqrt(jnp.sum(ssq, axis=-1, keepdims=True) * (1.0 / D_MODEL) + NORM_EPS)
        o_ref[rows, :] = o_ref[rows, :] * sc * nw_ref[...]
        return carry
    lax.fori_loop(0, n_groups, body, 0)


def _combine(pos3, gates, h1, norm_w, yb, tq):
    s = h1.shape[0]
    nt = s // tq
    return pl.pallas_call(
        _combine_kernel,
        grid=(nt,),
        in_specs=[
            pl.BlockSpec((None, 1, TOP_K * tq), lambda i: (i, 0, 0), memory_space=pltpu.SMEM),
            pl.BlockSpec((None, 1, TOP_K * tq), lambda i: (jnp.minimum(i + 1, nt - 1), 0, 0),
                         memory_space=pltpu.SMEM),
            pl.BlockSpec((tq, TOP_K), lambda i: (i, 0)),
            pl.BlockSpec((tq, D_MODEL), lambda i: (i, 0)),
            pl.BlockSpec((1, D_MODEL), lambda i: (0, 0)),
            pl.BlockSpec(memory_space=pl.ANY),
        ],
        out_specs=pl.BlockSpec((tq, D_MODEL), lambda i: (i, 0)),
        out_shape=jax.ShapeDtypeStruct((s, D_MODEL), F32),
        scratch_shapes=[pltpu.VMEM((2, TOP_K * tq, PACKED_SLAB_PITCH, LANES), U32),
                        pltpu.SemaphoreType.DMA((2,))],
        compiler_params=_cparams(("arbitrary",)),
        name="combine_norm",
    )(pos3, pos3, gates, h1, norm_w, yb)


def _routing_metadata(top_idx, tm):
    s = top_idx.shape[0]
    n_assign = s * TOP_K
    nblk = -(-(n_assign + N_EXPERTS * (tm - 1)) // tm)
    e_flat = top_idx.reshape(n_assign)
    order = jnp.argsort(e_flat).astype(jnp.int32)
    rank = jnp.argsort(order).astype(jnp.int32)
    experts = jnp.arange(N_EXPERTS, dtype=jnp.int32)
    counts = jnp.sum((e_flat[:, None] == experts[None, :]).astype(jnp.int32), axis=0)
    padded = ((counts + tm - 1) // tm) * tm
    start_sorted = jnp.cumsum(counts) - counts
    start_pad = jnp.cumsum(padded) - padded
    pos = start_pad[e_flat] + rank - start_sorted[e_flat]
    n_used = jnp.sum(padded) // tm
    blocks = jnp.arange(nblk, dtype=jnp.int32)
    block_exp = jnp.sum((start_pad[None, :] <= (blocks * tm)[:, None]).astype(jnp.int32), axis=1) - 1
    block_exp = block_exp[jnp.minimum(blocks, n_used - 1)]
    block_valid = jnp.clip(counts[block_exp] - (blocks * tm - start_pad[block_exp]), 0, tm)
    block_valid = jnp.where(blocks < n_used, block_valid, 0)
    off = (blocks * tm - start_pad[block_exp])[:, None] + jnp.arange(tm, dtype=jnp.int32)[None, :]
    is_row = (off < counts[block_exp][:, None]) & (blocks < n_used)[:, None]
    src = jnp.clip(start_sorted[block_exp][:, None] + off, 0, n_assign - 1)
    spread = (blocks[:, None] * tm + jnp.arange(tm, dtype=jnp.int32)[None, :]) % s
    row_tok = jnp.where(is_row, order[src] // TOP_K, spread)
    present = counts > 0
    nxt = experts[None, :] > experts[:, None]
    next_present = jnp.min(jnp.where(nxt & present[None, :], experts[None, :], N_EXPERTS), axis=1)
    first_present = jnp.min(jnp.where(present, experts, N_EXPERTS))
    last_present = jnp.max(jnp.where(present, experts, -1))
    next_present = jnp.where(next_present == N_EXPERTS, first_present, next_present)
    block_next = next_present[block_exp]
    as_i32 = lambda t: t.astype(jnp.int32)
    meta = jnp.stack([as_i32(n_used), as_i32(last_present)])
    sched = (as_i32(block_valid), as_i32(block_exp), as_i32(block_next), meta)
    return (as_i32(n_used).reshape(1), sched, as_i32(row_tok).reshape(nblk, 1, tm),
            as_i32(pos).reshape(s, TOP_K))


def _pick(n, pref):
    t = pref
    while n % t:
        t //= 2
    return t


def kernel(x, meta_tokens, attn_norm_w, w_in, b_in, attn_sinks, ret_norm_w, w_out, b_out, ffn_norm_w, router_w,
           router_b, w_gu, b_gu, w_down, b_down, final_norm_w):
    assert x.shape[0] == 1 and attn_norm_w.shape[0] == 1
    xs = x[0]
    s = xs.shape[0]
    assert s % WINDOW == 0
    tm = _pick(s, 512)

    w_in_bf = w_in[0].astype(BF16)
    b_in2 = b_in[0][None, :]
    nw = attn_norm_w[0][None, :]
    z = _inproj(xs, nw, w_in_bf, b_in2, _rope_tables(N_META, s), tm)
    zm = _inproj(meta_tokens, nw, w_in_bf, b_in2, _rope_tables(0, N_META), N_META)
    zm = jnp.pad(zm, ((0, WINDOW - N_META), (0, 0)))

    att = _attention(z, zm, attn_sinks[0])
    ret = _retention(z, zm, ret_norm_w[0][None, :])
    h1, h1_slabs = _outproj(att, ret, w_out[0].astype(BF16), b_out[0][None, :], xs, tm, OUT_TN)

    ffn_w = ffn_norm_w[0][None, :]
    idx_t, gate_t = _router(h1, ffn_w, router_w[0].T, router_b[0][:, None], tm)
    gates = gate_t.T
    n_used, sched, row_tok3, pos = _routing_metadata(idx_t.T, EXPERT_TM)

    xb, inv_rows = _gather_norm(n_used, sched[0], row_tok3, h1_slabs, ffn_w, EXPERT_TM)
    act = _gateup(sched, xb, inv_rows, w_gu[0], b_gu[0][:, None, :], EXPERT_TM)
    yb = _down(sched, act, w_down[0], b_down[0][:, None, :], EXPERT_TM)

    tq = WINDOW
    pos3 = pos.reshape(s // tq, tq, TOP_K).transpose(0, 2, 1).reshape(s // tq, 1, TOP_K * tq)
    out = _combine(pos3, gates, h1, final_norm_w[None, :], yb, tq)
    return out[None]
```
